```python
import math
import jax
import jax.numpy as jnp
from jax import lax
import numpy as np

D_MODEL = 4096
BATCH = 4
SEQ = 2048
DEPTH = 2
DEC_BATCH = 128
DEC_SEQ = 4
PAST_LEN = 16384
PAGE_SIZE = 128

D_MIX = D_MODEL
N_MIXERS = 4
D_GROUP = D_MIX // N_MIXERS
HEAD_DIM = 256
N_HEADS = D_GROUP // HEAD_DIM
CHUNK = 64
CONV_W = 4
GLA_RANK = 16
GLA_TAU = 16.0
PEER_HEADS = 8
PEER_KEYS = 128
PEER_TOPK = 16
PEER_QDIM = 256
N_EXPERTS = PEER_KEYS * PEER_KEYS
PEER_BLOCK = 64
N_MOD = 6
EPS = 1e-6
NEG_BIG = -1e30
MIN_FORGET = 1e-6

IN_SIZES = (
    D_GROUP, D_GROUP, D_GROUP, D_GROUP, N_HEADS, N_HEADS,
    D_GROUP, D_GROUP, D_GROUP, D_GROUP,
    D_GROUP, D_GROUP, D_GROUP, D_GROUP, N_HEADS, N_HEADS,
    D_GROUP, D_GROUP, D_GROUP, D_GROUP, GLA_RANK,
)
D_IN = sum(IN_SIZES)

kernel_name = 'hybrid_mlstm_hgrn2_gdn_gla_peer_step'


def rmsnorm(x, g):
    xf = x.astype(jnp.float32)
    y = xf * lax.rsqrt(jnp.mean(xf * xf, axis=-1, keepdims=True) + EPS) * g.astype(jnp.float32)
    return y.astype(x.dtype)


def l2norm(t):
    return t * lax.rsqrt(jnp.sum(t * t, axis=-1, keepdims=True) + EPS)


def heads(t):
    return t.reshape(t.shape[:-1] + (N_HEADS, HEAD_DIM))


def to_chunks(t, c):
    b, T, h = t.shape[:3]
    t = t.reshape((b, T // c, c, h) + t.shape[3:])
    return jnp.moveaxis(t, (1, 3), (0, 2))


def from_chunks(t):
    t = jnp.moveaxis(t, (0, 2), (1, 3))
    b, nc, c, h = t.shape[:4]
    return t.reshape((b, nc * c, h) + t.shape[4:])


def causal_masks(c):
    idx = jnp.arange(c)
    return idx[:, None] >= idx[None, :], idx[:, None] > idx[None, :]


def masked_exp(mask, t):
    return jnp.where(mask, jnp.exp(jnp.where(mask, t, 0.0)), 0.0)


def causal_conv(u, buf, w):
    T = u.shape[1]
    cat = jnp.concatenate([buf, u], axis=1)
    out = cat[:, 0:T] * w[0]
    for i in range(1, CONV_W):
        out = out + cat[:, i:i + T] * w[i]
    return out, cat[:, -(CONV_W - 1):]


def mlstm_chunked(q, k, v, ig, lf, C0, n0, m0):
    c = math.gcd(q.shape[1], CHUNK)
    incl, _ = causal_masks(c)

    def step(carry, inp):
        C, n, m = carry
        qb, kb, vb, ib, fb = inp
        F = jnp.cumsum(fb, axis=-1)
        raw = F[..., :, None] - F[..., None, :] + ib[..., None, :]
        a = F + m[..., None]
        m_t = jnp.maximum(a, jnp.max(jnp.where(incl, raw, NEG_BIG), axis=-1))
        s = jnp.einsum('bhtd,bhsd->bhts', qb, kb) * masked_exp(incl, raw - m_t[..., None])
        inter = jnp.exp(a - m_t)
        num = inter[..., None] * jnp.einsum('bhtd,bhde->bhte', qb, C) + jnp.einsum('bhts,bhse->bhte', s, vb)
        den = inter * jnp.einsum('bhtd,bhd->bht', qb, n) + jnp.sum(s, axis=-1)
        h = num / jnp.maximum(jnp.abs(den), jnp.exp(-m_t))[..., None]
        m_end = m_t[..., -1]
        w_end = jnp.exp(F[..., -1:] - F + ib - m_end[..., None])
        dec = jnp.exp(a[..., -1] - m_end)
        C_new = dec[..., None, None] * C + jnp.einsum('bhs,bhsd,bhse->bhde', w_end, kb, vb)
        n_new = dec[..., None] * n + jnp.einsum('bhs,bhsd->bhd', w_end, kb)
        return (C_new, n_new, m_end), h

    xs = tuple(to_chunks(t, c) for t in (q, k, v, ig, lf))
    (C1, n1, m1), h = lax.scan(step, (C0, n0, m0), xs)
    return from_chunks(h), C1, n1, m1


def gla_chunked(q, k, v, lg, S0):
    c = math.gcd(q.shape[1], CHUNK)
    incl, _ = causal_masks(c)

    def step(S, inp):
        qb, kb, vb, gb = inp
        G = jnp.cumsum(gb, axis=-2)
        rel = masked_exp(incl[..., None], G[..., :, None, :] - G[..., None, :, :])
        A = jnp.einsum('bhtd,bhsd,bhtsd->bhts', qb, kb, rel)
        o = jnp.einsum('bhtd,bhde->bhte', qb * jnp.exp(G), S) + jnp.einsum('bhts,bhse->bhte', A, vb)
        G_end = G[..., -1:, :]
        S_new = jnp.exp(G_end[..., 0, :])[..., None] * S + jnp.einsum('bhsd,bhse->bhde', kb * jnp.exp(G_end - G), vb)
        return S_new, o

    xs = tuple(to_chunks(t, c) for t in (q, k, v, lg))
    S1, o = lax.scan(step, S0, xs)
    return from_chunks(o), S1


def gdn_chunked(q, k, v, lg, beta, S0):
    c = math.gcd(q.shape[1], CHUNK)
    incl, strict = causal_masks(c)

    def step(S, inp):
        qb, kb, vb, gb, bb = inp
        G = jnp.cumsum(gb, axis=-1)
        rel = masked_exp(incl, G[..., :, None] - G[..., None, :])
        M = jnp.where(strict, bb[..., :, None] * rel * jnp.einsum('bhtd,bhsd->bhts', kb, kb), 0.0)
        rhs = bb[..., None] * (vb - jnp.exp(G)[..., None] * jnp.einsum('bhtd,bhde->bhte', kb, S))
        u = lax.linalg.triangular_solve(M, rhs, left_side=True, lower=True, unit_diagonal=True)
        qk = jnp.einsum('bhtd,bhsd->bhts', qb, kb) * rel
        o = jnp.exp(G)[..., None] * jnp.einsum('bhtd,bhde->bhte', qb, S) + jnp.einsum('bhts,bhse->bhte', qk, u)
        G_end = G[..., -1]
        S_new = jnp.exp(G_end)[..., None, None] * S + jnp.einsum('bhs,bhsd,bhse->bhde', jnp.exp(G_end[..., None] - G), kb, u)
        return S_new, o

    xs = tuple(to_chunks(t, c) for t in (q, k, v, lg, beta))
    S1, o = lax.scan(step, S0, xs)
    return from_chunks(o), S1


def token_mixers(h, st, l, p):
    f32 = jnp.float32
    mC, mn, mm, s_hgrn, s_gdn, conv_buf, s_gla = [s.astype(f32) for s in st]
    z = (h @ p['w_in'][l]).astype(f32)
    points = [int(s) for s in np.cumsum(IN_SIZES)[:-1]]
    (aq, ak, av, ao, ai, af,
     bq, bfg, bi, bo,
     cq, ck, cv, co, ca, cb,
     dq, dk, dv, dog, dg) = jnp.split(z, points, axis=-1)
    scale = HEAD_DIM ** -0.5

    ig = ai + p['mlstm_b_i'][l].astype(f32)
    lf = jax.nn.log_sigmoid(af + p['mlstm_b_f'][l].astype(f32))
    ha, mC1, mn1, mm1 = mlstm_chunked(heads(aq), heads(ak) * scale, heads(av), ig, lf, mC, mn, mm)

    lbs = jax.nn.softmax(p['hgrn_lb'].astype(f32), axis=0)
    lb = (jnp.cumsum(lbs, axis=0) - lbs[0])[l]
    f_b = lb + (1.0 - lb) * jax.nn.sigmoid(bfg)
    lg_b = jnp.log(jnp.maximum(f_b, MIN_FORGET))
    k_b = (1.0 - lb) * jax.nn.sigmoid(-bfg)
    hb, s_hgrn1 = gla_chunked(heads(jax.nn.silu(bq)), heads(k_b), heads(bi), heads(lg_b), s_hgrn)

    conv, conv_buf1 = causal_conv(jnp.concatenate([cq, ck, cv], axis=-1), conv_buf, p['gdn_conv_w'][l].astype(f32))
    cq2, ck2, cv2 = jnp.split(jax.nn.silu(conv), 3, axis=-1)
    lg_c = -jnp.exp(p['gdn_a_log'][l].astype(f32)) * jax.nn.softplus(ca + p['gdn_dt_bias'][l].astype(f32))
    hc, s_gdn1 = gdn_chunked(l2norm(heads(cq2)) * scale, l2norm(heads(ck2)), heads(cv2), lg_c,
                             jax.nn.sigmoid(cb), s_gdn)

    lg_d = jax.nn.log_sigmoid(dg @ p['gla_w_gate'][l].astype(f32) + p['gla_b_gate'][l].astype(f32)) / GLA_TAU
    hd, s_gla1 = gla_chunked(heads(dq) * scale, heads(dk), heads(dv), heads(lg_d), s_gla)

    o = jnp.concatenate([ha, hb, hc, hd], axis=2)
    o = o * lax.rsqrt(jnp.mean(o * o, axis=-1, keepdims=True) + EPS)
    gate = jnp.concatenate([jax.nn.sigmoid(ao), jax.nn.silu(bo), jax.nn.silu(co), jax.nn.silu(dog)], axis=-1)
    o = o.reshape(gate.shape) * p['out_norm'][l].astype(f32) * gate
    y = o.astype(h.dtype) @ p['w_out'][l]
    return y, (mC1, mn1, mm1, s_hgrn1, s_gdn1, conv_buf1, s_gla1)


def peer(h, w_q, sub_keys, u_tab, v_tab):
    f32 = jnp.float32
    B, T, D = h.shape
    n_tok = B * T
    pad = (-n_tok) % PEER_BLOCK
    blocks = jnp.pad(h.reshape(n_tok, D), ((0, pad), (0, 0))).reshape(-1, PEER_BLOCK, D)
    kk = PEER_TOPK * PEER_TOPK

    def one_block(xb):
        q = (xb @ w_q).astype(f32).reshape(PEER_BLOCK, PEER_HEADS, 2, PEER_QDIM // 2)
        s = jnp.einsum('phzd,hzkd->phzk', q, sub_keys.astype(f32))
        s_top, i_top = lax.top_k(s, PEER_TOPK)
        cand = (s_top[:, :, 0, :, None] + s_top[:, :, 1, None, :]).reshape(PEER_BLOCK, PEER_HEADS, kk)
        cand_id = (i_top[:, :, 0, :, None] * PEER_KEYS + i_top[:, :, 1, None, :]).reshape(PEER_BLOCK, PEER_HEADS, kk)
        best, pos = lax.top_k(cand, PEER_TOPK)
        eid = jnp.take_along_axis(cand_id, pos, axis=-1)
        g = jax.nn.softmax(best, axis=-1)
        u = u_tab[eid]
        v = v_tab[eid]
        act = jax.nn.gelu(jnp.einsum('pd,phkd->phk', xb, u).astype(f32), approximate=False)
        return jnp.einsum('phk,phkd->pd', (g * act).astype(v.dtype), v)

    out = lax.map(one_block, blocks).reshape(-1, D)[:n_tok]
    return out.reshape(B, T, D).astype(h.dtype)


def trunk(x, c, states, p):
    cs = jax.nn.silu(c)
    new_states = [[] for _ in range(len(states))]
    for l in range(DEPTH):
        mod = cs @ p['w_ada'][l] + p['b_ada'][l]
        sh_mix, sc_mix, g_mix, sh_ffn, sc_ffn, g_ffn = [m[:, None, :] for m in jnp.split(mod, N_MOD, axis=-1)]
        h = rmsnorm(x, p['norm_mix'][l]) * (1.0 + sc_mix) + sh_mix
        y, new = token_mixers(h, tuple(s[l] for s in states), l, p)
        x = x + g_mix * y
        h = rmsnorm(x, p['norm_ffn'][l]) * (1.0 + sc_ffn) + sh_ffn
        x = x + g_ffn * peer(h, p['peer_w_q'][l], p['peer_sub_keys'][l], p['peer_u'][l], p['peer_v'][l])
        for lst, s in zip(new_states, new):
            lst.append(s)
    return rmsnorm(x, p['final_norm']), [jnp.stack(s) for s in new_states]


def setup_inputs(seed: int = 0) -> dict:
    key = jax.random.key(seed)
    ks = jax.random.split(key, 32)
    f32 = jnp.float32

    def nrm(k, shape, s):
        return jax.random.normal(k, shape, f32) * s

    mat = (DEPTH, DEC_BATCH, N_HEADS, HEAD_DIM, HEAD_DIM)
    dt = jnp.exp(jax.random.uniform(ks[21], (DEPTH, N_HEADS), f32, math.log(1e-3), math.log(1e-1)))
    return {
        'x_prompt': nrm(ks[0], (BATCH, SEQ, D_MODEL), 1.0),
        'x_sample': nrm(ks[1], (DEC_BATCH, DEC_SEQ, D_MODEL), 1.0),
        'c_prompt': nrm(ks[2], (BATCH, D_MODEL), 1.0),
        'c_sample': nrm(ks[3], (DEC_BATCH, D_MODEL), 1.0),
        'state_mlstm_C': nrm(ks[4], mat, 0.05),
        'state_mlstm_n': nrm(ks[5], (DEPTH, DEC_BATCH, N_HEADS, HEAD_DIM), 0.1),
        'state_mlstm_m': jax.random.uniform(ks[6], (DEPTH, DEC_BATCH, N_HEADS), f32, -1.0, 1.0),
        'state_hgrn': nrm(ks[7], mat, 0.1),
        'state_gdn': nrm(ks[8], mat, 0.1),
        'state_gdn_conv': nrm(ks[9], (DEPTH, DEC_BATCH, CONV_W - 1, 3 * D_GROUP), 1.0),
        'state_gla': nrm(ks[10], mat, 0.1),
        'w_ada': nrm(ks[11], (DEPTH, D_MODEL, N_MOD * D_MODEL), 0.5 * D_MODEL ** -0.5),
        'b_ada': nrm(ks[12], (DEPTH, N_MOD * D_MODEL), 0.02),
        'norm_mix': 1.0 + nrm(ks[13], (DEPTH, D_MODEL), 0.02),
        'norm_ffn': 1.0 + nrm(ks[14], (DEPTH, D_MODEL), 0.02),
        'w_in': nrm(ks[15], (DEPTH, D_MODEL, D_IN), D_MODEL ** -0.5),
        'mlstm_b_i': nrm(ks[16], (DEPTH, N_HEADS), 0.1),
        'mlstm_b_f': jax.random.uniform(ks[17], (DEPTH, N_HEADS), f32, 3.0, 6.0),
        'hgrn_lb': nrm(ks[18], (DEPTH, D_GROUP), 1.0),
        'gdn_conv_w': nrm(ks[19], (DEPTH, CONV_W, 3 * D_GROUP), CONV_W ** -0.5),
        'gdn_a_log': jnp.log(jax.random.uniform(ks[20], (DEPTH, N_HEADS), f32, 1.0, 16.0)),
        'gdn_dt_bias': dt + jnp.log(-jnp.expm1(-dt)),
        'gla_w_gate': nrm(ks[22], (DEPTH, GLA_RANK, D_GROUP), GLA_RANK ** -0.5),
        'gla_b_gate': nrm(ks[23], (DEPTH, D_GROUP), 0.1),
        'out_norm': 1.0 + nrm(ks[24], (DEPTH, D_MIX), 0.02),
        'w_out': nrm(ks[25], (DEPTH, D_MIX, D_MODEL), D_MIX ** -0.5),
        'peer_w_q': nrm(ks[26], (DEPTH, D_MODEL, PEER_HEADS * PEER_QDIM), D_MODEL ** -0.5),
        'peer_sub_keys': nrm(ks[27], (DEPTH, PEER_HEADS, 2, PEER_KEYS, PEER_QDIM // 2), (PEER_QDIM // 2) ** -0.5),
        'peer_u': nrm(ks[28], (DEPTH, N_EXPERTS, D_MODEL), D_MODEL ** -0.5),
        'peer_v': nrm(ks[29], (DEPTH, N_EXPERTS, D_MODEL), PEER_HEADS ** -0.5),
        'final_norm': 1.0 + nrm(ks[30], (D_MODEL,), 0.02),
    }


def reference(x_prompt, x_sample, c_prompt, c_sample,
              state_mlstm_C, state_mlstm_n, state_mlstm_m, state_hgrn, state_gdn, state_gdn_conv, state_gla,
              w_ada, b_ada, norm_mix, norm_ffn, w_in, mlstm_b_i, mlstm_b_f, hgrn_lb,
              gdn_conv_w, gdn_a_log, gdn_dt_bias, gla_w_gate, gla_b_gate, out_norm, w_out,
              peer_w_q, peer_sub_keys, peer_u, peer_v, final_norm):
    params = dict(w_ada=w_ada, b_ada=b_ada, norm_mix=norm_mix, norm_ffn=norm_ffn, w_in=w_in,
                  mlstm_b_i=mlstm_b_i, mlstm_b_f=mlstm_b_f, hgrn_lb=hgrn_lb, gdn_conv_w=gdn_conv_w,
                  gdn_a_log=gdn_a_log, gdn_dt_bias=gdn_dt_bias, gla_w_gate=gla_w_gate, gla_b_gate=gla_b_gate,
                  out_norm=out_norm, w_out=w_out, peer_w_q=peer_w_q, peer_sub_keys=peer_sub_keys,
                  peer_u=peer_u, peer_v=peer_v, final_norm=final_norm)
    f32 = jnp.float32
    bp = x_prompt.shape[0]
    mat = (DEPTH, bp, N_HEADS, HEAD_DIM, HEAD_DIM)
    fresh = (jnp.zeros(mat, f32), jnp.zeros((DEPTH, bp, N_HEADS, HEAD_DIM), f32),
             jnp.zeros((DEPTH, bp, N_HEADS), f32), jnp.zeros(mat, f32), jnp.zeros(mat, f32),
             jnp.zeros((DEPTH, bp, CONV_W - 1, 3 * D_GROUP), f32), jnp.zeros(mat, f32))
    y_prompt, p_states = trunk(x_prompt, c_prompt, fresh, params)
    p_C, p_n, p_m, p_hgrn, p_gdn, p_conv, p_gla = p_states
    past = (state_mlstm_C, state_mlstm_n, state_mlstm_m, state_hgrn, state_gdn, state_gdn_conv, state_gla)
    y_sample, s_states = trunk(x_sample, c_sample, past, params)
    s_C, s_n, s_m, s_hgrn, s_gdn, s_conv, s_gla = s_states
    return (y_prompt, y_sample, p_C, p_n, p_m, p_hgrn, p_gdn, p_conv, p_gla,
            s_C, s_n, s_m, s_hgrn, s_gdn, s_conv, s_gla)
```

```python
import functools
import math

import jax
import jax.numpy as jnp
from jax import lax
from jax.experimental import pallas as pl
from jax.experimental.pallas import tpu as pltpu

F32 = jnp.float32
BF16 = jnp.bfloat16

HEAD_DIM = 256
N_MIXERS = 4
CONV_W = 4
GLA_RANK = 16
GLA_TAU = 16.0
PEER_HEADS = 8
PEER_KEYS = 128
PEER_TOPK = 16
N_MOD = 6
EPS = 1e-6
NEG_BIG = -1e30
MIN_FORGET = 1e-6

LANES = 128
SUBLANES = 8
SMALL_ROWS = 32
CHUNK = 64
SUB = 16
TOKEN_TILE = 512
EXPERT_TILE = 256
VMEM_LIMIT = 56 * 1024 * 1024


def _cparams(sem):
    return pltpu.CompilerParams(dimension_semantics=sem, vmem_limit_bytes=VMEM_LIMIT)


def _dot(a, b):
    return jnp.dot(a, b, preferred_element_type=F32)


def _dot_nt(a, b):
    return lax.dot_general(a, b, (((1,), (1,)), ((), ())), preferred_element_type=F32)


def _dot_tn(a, b):
    return lax.dot_general(a, b, (((0,), (0,)), ((), ())), preferred_element_type=F32)


def _dot_f32(a, b):
    return jnp.dot(a, b, precision=lax.Precision.HIGHEST, preferred_element_type=F32)


def _sigmoid(x):
    return 1.0 / (1.0 + jnp.exp(-x))


def _silu(x):
    return x * _sigmoid(x)


def _log_sigmoid(x):
    return jnp.minimum(x, 0.0) - jnp.log1p(jnp.exp(-jnp.abs(x)))


def _softplus(x):
    return jnp.maximum(x, 0.0) + jnp.log1p(jnp.exp(-jnp.abs(x)))


def _rms_mod(x, nw, sc, sh):
    y = x * lax.rsqrt(jnp.mean(x * x, axis=-1, keepdims=True) + EPS) * nw
    return y * (1.0 + sc) + sh


def _merge(h, gate, onorm):
    hn = h * lax.rsqrt(jnp.mean(h * h, axis=-1, keepdims=True) + EPS)
    return (hn * onorm * gate).astype(BF16)


def _iota2(shape, dim):
    return lax.broadcasted_iota(jnp.int32, shape, dim)


def _row_to_col(r):
    n = r.shape[1]
    eye = _iota2((n, n), 0) == _iota2((n, n), 1)
    return jnp.sum(jnp.where(eye, r, 0.0), axis=1, keepdims=True)


def _cumsum_pair(x_c, x_r, incl, incl_t):
    f_c = jnp.sum(jnp.where(incl, x_r, 0.0), axis=1, keepdims=True)
    f_r = jnp.sum(jnp.where(incl_t, x_c, 0.0), axis=0, keepdims=True)
    return f_c, f_r


def _cumsum_rows(x, tri_b):
    hi = x.astype(BF16)
    r1 = x - hi.astype(F32)
    mid = r1.astype(BF16)
    lo = (r1 - mid.astype(F32)).astype(BF16)
    return _dot(tri_b, hi) + _dot(tri_b, mid) + _dot(tri_b, lo)


def _ada_kernel(c_ref, w_ref, b_ref, o_ref):
    cs = _silu(c_ref[...]).astype(BF16)
    o_ref[...] = _dot(cs, w_ref[...].astype(BF16)) + b_ref[...]


def _ada(c_all, w_ada, b_ada):
    depth, d, n6 = w_ada.shape
    rows = c_all.shape[0]
    tn = 512
    return pl.pallas_call(
        _ada_kernel,
        grid=(depth, n6 // tn),
        in_specs=[
            pl.BlockSpec((rows, d), lambda l, j: (0, 0)),
            pl.BlockSpec((None, d, tn), lambda l, j: (l, 0, j)),
            pl.BlockSpec((None, 1, tn), lambda l, j: (l, 0, j)),
        ],
        out_specs=pl.BlockSpec((None, rows, tn), lambda l, j: (l, 0, j)),
        out_shape=jax.ShapeDtypeStruct((depth, rows, n6), F32),
        compiler_params=_cparams(("arbitrary", "arbitrary")),
    )(c_all, w_ada, b_ada.reshape(depth, 1, n6))


class _Tiling:
    def __init__(self, b, t, d, tile=TOKEN_TILE):
        self.b, self.t, self.d = b, t, d
        self.n = b * t
        self.per_batch = t % LANES == 0
        if self.per_batch:
            self.tm = next(m for m in (tile, 256, LANES) if m <= tile and t % m == 0)
        else:
            self.tm = self.n if self.n <= tile else tile
            assert self.n % self.tm == 0 and self.tm % SUBLANES == 0
        self.tiles = self.n // self.tm

    def mod_array(self, mod_l):
        if self.per_batch:
            return mod_l.reshape(self.b, 1, mod_l.shape[-1])
        return jnp.repeat(mod_l, self.t, axis=0)

    def mod_spec(self, k, grid_rank):
        d, tm, t = self.d, self.tm, self.t
        if self.per_batch:
            if grid_rank == 1:
                return pl.BlockSpec((None, 1, d), lambda i: ((i * tm) // t, 0, k))
            return pl.BlockSpec((None, 1, d), lambda i, j: ((i * tm) // t, 0, k))
        if grid_rank == 1:
            return pl.BlockSpec((tm, d), lambda i: (i, k))
        return pl.BlockSpec((tm, d), lambda i, j: (i, k))


def _in_proj_kernel(x_ref, nw_ref, sc_ref, sh_ref, wb_ref, ws_ref, z_ref, zs_ref, h_scr):
    @pl.when(pl.program_id(1) == 0)
    def _():
        hb = _rms_mod(x_ref[...], nw_ref[...], sc_ref[...], sh_ref[...]).astype(BF16)
        h_scr[...] = hb
        zs_ref[...] = _dot(hb, ws_ref[...])

    z_ref[...] = _dot(h_scr[...], wb_ref[...])


def _in_proj(til, x, nw, mod_arr, w_big, w_small):
    n, d, tm = til.n, til.d, til.tm
    nbig = w_big.shape[1]
    tn = min(1024, nbig)
    return pl.pallas_call(
        _in_proj_kernel,
        grid=(til.tiles, nbig // tn),
        in_specs=[
            pl.BlockSpec((tm, d), lambda i, j: (i, 0)),
            pl.BlockSpec((1, d), lambda i, j: (0, 0)),
            til.mod_spec(1, 2),
            til.mod_spec(0, 2),
            pl.BlockSpec((d, tn), lambda i, j: (0, j)),
            pl.BlockSpec((d, LANES), lambda i, j: (0, 0)),
        ],
        out_specs=[
            pl.BlockSpec((tm, tn), lambda i, j: (i, j)),
            pl.BlockSpec((tm, LANES), lambda i, j: (i, 0)),
        ],
        out_shape=[jax.ShapeDtypeStruct((n, nbig), F32), jax.ShapeDtypeStruct((n, LANES), F32)],
        scratch_shapes=[pltpu.VMEM((tm, d), BF16)],
        compiler_params=_cparams(("arbitrary", "arbitrary")),
    )(x, nw.reshape(1, d), mod_arr, mod_arr, w_big, w_small)


def _mlstm_kernel(*refs, bb, nh, c, n_valid, has_init):
    if has_init:
        (q_ref, k_ref, v_ref, g_ref, zc_ref, zr_ref, br_ref, bc_ref, on_ref, c0_ref, n0_ref, m0_ref,
         o_ref, cs_ref, ns_ref, ms_ref) = refs
    else:
        (q_ref, k_ref, v_ref, g_ref, zc_ref, zr_ref, br_ref, bc_ref, on_ref,
         o_ref, cs_ref, ns_ref, ms_ref) = refs

    @pl.when(pl.program_id(1) == 0)
    def _():
        if has_init:
            cs_ref[...] = c0_ref[...]
            ns_ref[...] = n0_ref[...]
            ms_ref[...] = m0_ref[...]
        else:
            cs_ref[...] = jnp.zeros_like(cs_ref)
            ns_ref[...] = jnp.zeros_like(ns_ref)
            ms_ref[...] = jnp.zeros_like(ms_ref)

    hd = HEAD_DIM
    scale = hd ** -0.5
    ti = _iota2((c, c), 0)
    si = _iota2((c, c), 1)
    incl = si <= ti
    incl_t = ti <= si
    valid_c = _iota2((c, 1), 0) < n_valid
    valid_r = _iota2((1, c), 1) < n_valid
    for b in range(bb):
        zc = zc_ref[b] + br_ref[...]
        zr = zr_ref[b] + bc_ref[...]
        for h in range(nh):
            hs = slice(h * hd, (h + 1) * hd)
            q = q_ref[b, :, hs]
            k = k_ref[b, :, hs] * scale
            v = v_ref[b, :, hs]
            ig_c = zc[:, h:h + 1]
            lf_c = _log_sigmoid(zc[:, nh + h:nh + h + 1])
            ig_r = zr[h:h + 1, :]
            lf_r = _log_sigmoid(zr[nh + h:nh + h + 1, :])
            if n_valid < c:
                ig_c = jnp.where(valid_c, ig_c, NEG_BIG)
                lf_c = jnp.where(valid_c, lf_c, 0.0)
                ig_r = jnp.where(valid_r, ig_r, NEG_BIG)
                lf_r = jnp.where(valid_r, lf_r, 0.0)
            cm = cs_ref[b, h]
            nv = ns_ref[b, h:h + 1, :]
            m0 = ms_ref[b, :, h:h + 1]

            f_c, f_r = _cumsum_pair(lf_c, lf_r, incl, incl_t)
            raw = f_c - f_r + ig_r
            a = f_c + m0
            m_t = jnp.maximum(a, jnp.max(jnp.where(incl, raw, NEG_BIG), axis=1, keepdims=True))
            p = jnp.where(incl, jnp.exp(jnp.where(incl, raw - m_t, 0.0)), 0.0)
            qb = q.astype(BF16)
            kb = k.astype(BF16)
            vb = v.astype(BF16)
            s = _dot_nt(qb, kb) * p
            inter = jnp.exp(a - m_t)
            num = inter * _dot(qb, cm.astype(BF16)) + _dot(s.astype(BF16), vb)
            den = inter * jnp.sum(q * nv, axis=1, keepdims=True) + jnp.sum(s, axis=1, keepdims=True)
            hh = num / jnp.maximum(jnp.abs(den), jnp.exp(-m_t))
            m_end = m_t[c - 1:c, :]
            w_end = jnp.exp(f_c[c - 1:c, :] - f_c + ig_c - m_end)
            dec = jnp.exp(a[c - 1:c, :] - m_end)
            kw = w_end * k
            cs_ref[b, h] = dec * cm + _dot_tn(kw.astype(BF16), vb)
            ns_ref[b, h:h + 1, :] = dec * nv + jnp.sum(kw, axis=0, keepdims=True)
            ms_ref[b, :, h:h + 1] = m_end
            o_ref[b, :, hs] = _merge(hh, _sigmoid(g_ref[b, :, hs]), on_ref[:, hs])


def _gla_unit(q, k, v, lg, s_mat, tri_b):
    c, hd = q.shape
    sb = min(SUB, c)
    nb = c // sb
    g = _cumsum_rows(lg, tri_b)
    o = _dot((q * jnp.exp(g)).astype(BF16), s_mat.astype(BF16))

    if nb > 1:
        qparts, kparts = [], []
        for j in range(nb - 1):
            r1 = (j + 1) * sb
            g_end = g[r1 - 1:r1, :]
            qj = q[r1:, :] * jnp.exp(g[r1:, :] - g_end)
            kj = k[j * sb:r1, :] * jnp.exp(g_end - g[j * sb:r1, :])
            qparts.append(jnp.concatenate([jnp.zeros((r1, hd), F32), qj], axis=0).astype(BF16))
            pieces = [kj]
            if j > 0:
                pieces.insert(0, jnp.zeros((j * sb, hd), F32))
            pieces.append(jnp.zeros((c - r1, hd), F32))
            kparts.append(jnp.concatenate(pieces, axis=0).astype(BF16))
        a_off = _dot_nt(jnp.concatenate(qparts, axis=1), jnp.concatenate(kparts, axis=1))
    else:
        a_off = jnp.zeros((c, c), F32)

    lane = _iota2((sb, c), 1)
    trow = _iota2((sb, 1), 0)
    strips = []
    for i in range(nb):
        r0 = i * sb
        qi, ki, gi = q[r0:r0 + sb, :], k[r0:r0 + sb, :], g[r0:r0 + sb, :]
        strip = a_off[r0:r0 + sb, :]
        for s in range(sb):
            msk = trow >= s
            w = jnp.where(msk, jnp.exp(jnp.where(msk, gi - gi[s:s + 1, :], 0.0)), 0.0)
            col = jnp.sum(qi * ki[s:s + 1, :] * w, axis=1, keepdims=True)
            strip = jnp.where(lane == r0 + s, col, strip)
        strips.append(strip)
    a = strips[0] if nb == 1 else jnp.concatenate(strips, axis=0)
    vb = v.astype(BF16)
    o = o + _dot(a.astype(BF16), vb)

    g_end = g[c - 1:c, :]
    kt = (k * jnp.exp(g_end - g)).astype(BF16)
    s_new = _row_to_col(jnp.exp(g_end)) * s_mat + _dot_tn(kt, vb)
    return o, s_new


def _gla_kernel(*refs, bb, nh, c, n_valid, has_init, kind, layer):
    if kind == "hgrn":
        q_ref, k_ref, v_ref, g_ref, par_ref, on_ref = refs[:6]
        rest = refs[6:]
    else:
        q_ref, k_ref, v_ref, g_ref, zc_ref, wg_ref, bg_ref, on_ref = refs[:8]
        rest = refs[8:]
    if has_init:
        s0_ref, o_ref, st_ref = rest
    else:
        o_ref, st_ref = rest

    @pl.when(pl.program_id(1) == 0)
    def _():
        if has_init:
            st_ref[...] = s0_ref[...]
        else:
            st_ref[...] = jnp.zeros_like(st_ref)

    hd = HEAD_DIM
    tri_b = (_iota2((c, c), 1) <= _iota2((c, c), 0)).astype(BF16)
    valid_c = _iota2((c, 1), 0) < n_valid
    if kind == "hgrn":
        lbp = par_ref[...]
        ex = jnp.exp(lbp - jnp.max(lbp, axis=0, keepdims=True))
        lbs = ex / jnp.sum(ex, axis=0, keepdims=True)
        lb = jnp.zeros_like(lbs[0:1, :])
        for j in range(1, layer + 1):
            lb = lb + lbs[j:j + 1, :]
    for b in range(bb):
        if kind == "gla":
            gate_in = _dot(zc_ref[b].astype(BF16), wg_ref[...].astype(BF16)) + bg_ref[...]
        for h in range(nh):
            hs = slice(h * hd, (h + 1) * hd)
            if kind == "hgrn":
                fg = k_ref[b, :, hs]
                lbh = lb[:, hs]
                f = lbh + (1.0 - lbh) * _sigmoid(fg)
                lg = jnp.log(jnp.maximum(f, MIN_FORGET))
                k = (1.0 - lbh) * _sigmoid(-fg)
                q = _silu(q_ref[b, :, hs])
            else:
                lg = _log_sigmoid(gate_in[:, hs]) / GLA_TAU
                k = k_ref[b, :, hs]
                q = q_ref[b, :, hs] * (hd ** -0.5)
            v = v_ref[b, :, hs]
            if n_valid < c:
                lg = jnp.where(valid_c, lg, 0.0)
                k = jnp.where(valid_c, k, 0.0)
            o, s_new = _gla_unit(q, k, v, lg, st_ref[b, h], tri_b)
            st_ref[b, h] = s_new
            o_ref[b, :, hs] = _merge(o, _silu(g_ref[b, :, hs]), on_ref[:, hs])


def _unit_lower_inverse(m, eye):
    c = m.shape[0]
    t = eye - m
    p = _dot_f32(m, m)
    n_it = int(math.log2(c)) - 1
    for it in range(n_it):
        t = t + _dot_f32(t, p)
        if it < n_it - 1:
            p = _dot_f32(p, p)
    return t


def _gdn_kernel(*refs, bb, nh, c, n_valid, has_init):
    if has_init:
        (q_ref, k_ref, v_ref, g_ref, zc_ref, zr_ref, br_ref, bc_ref, ar_ref, ac_ref, cw_ref, on_ref,
         s0_ref, cv0_ref, o_ref, st_ref, cvo_ref, tail_ref) = refs
    else:
        (q_ref, k_ref, v_ref, g_ref, zc_ref, zr_ref, br_ref, bc_ref, ar_ref, ac_ref, cw_ref, on_ref,
         o_ref, st_ref, cvo_ref, tail_ref) = refs
    t_id = pl.program_id(1)
    n_chunks = pl.num_programs(1)
    tail_rows = SUBLANES
    n_buf = CONV_W - 1

    @pl.when(t_id == 0)
    def _():
        tail_ref[...] = jnp.zeros_like(tail_ref)
        if has_init:
            st_ref[...] = s0_ref[...]
            for b in range(bb):
                for pc in range(3):
                    tail_ref[b, pc, tail_rows - n_buf:tail_rows, :] = cv0_ref[b, :, pc, :]
        else:
            st_ref[...] = jnp.zeros_like(st_ref)

    hd = HEAD_DIM
    scale = hd ** -0.5
    ti = _iota2((c, c), 0)
    si = _iota2((c, c), 1)
    incl = si <= ti
    incl_t = ti <= si
    strict = si < ti
    eye = (si == ti).astype(F32)
    valid_c = _iota2((c, 1), 0) < n_valid
    valid_r = _iota2((1, c), 1) < n_valid
    raw_refs = (q_ref, k_ref, v_ref)
    for b in range(bb):
        conv = []
        for pc in range(3):
            u = raw_refs[pc][b]
            ext = jnp.concatenate([tail_ref[b, pc], u], axis=0)
            acc = u * cw_ref[CONV_W - 1:CONV_W, pc, :]
            for j in range(1, CONV_W):
                shifted = pltpu.roll(ext, j, axis=0)[tail_rows:tail_rows + c, :]
                acc = acc + shifted * cw_ref[CONV_W - 1 - j:CONV_W - j, pc, :]
            conv.append(_silu(acc))
            tail_ref[b, pc] = u[c - tail_rows:c, :]

        @pl.when(t_id == n_chunks - 1)
        def _():
            for pc in range(3):
                cvo_ref[b, :, pc, :] = raw_refs[pc][b, n_valid - n_buf:n_valid, :]

        zc = zc_ref[b] + br_ref[...]
        zr = zr_ref[b] + bc_ref[...]
        for h in range(nh):
            hs = slice(h * hd, (h + 1) * hd)
            q = conv[0][:, hs]
            k = conv[1][:, hs]
            v = conv[2][:, hs]
            q = q * lax.rsqrt(jnp.sum(q * q, axis=1, keepdims=True) + EPS) * scale
            k = k * lax.rsqrt(jnp.sum(k * k, axis=1, keepdims=True) + EPS)
            ca, cb = 2 * nh + h, 3 * nh + h
            lg_c = -jnp.exp(ar_ref[:, ca:ca + 1]) * _softplus(zc[:, ca:ca + 1])
            lg_r = -jnp.exp(ac_ref[ca:ca + 1, :]) * _softplus(zr[ca:ca + 1, :])
            beta = _sigmoid(zc[:, cb:cb + 1])
            if n_valid < c:
                lg_c = jnp.where(valid_c, lg_c, 0.0)
                lg_r = jnp.where(valid_r, lg_r, 0.0)
                beta = jnp.where(valid_c, beta, 0.0)
            s_mat = st_ref[b, h]

            g_c, g_r = _cumsum_pair(lg_c, lg_r, incl, incl_t)
            eg = jnp.exp(g_c)
            rel = jnp.where(incl, jnp.exp(jnp.where(incl, g_c - g_r, 0.0)), 0.0)
            qb = q.astype(BF16)
            kb = k.astype(BF16)
            kq_s = _dot(jnp.concatenate([kb, qb], axis=0), s_mat.astype(BF16))
            m = jnp.where(strict, beta * rel * _dot_nt(kb, kb), 0.0)
            rhs = beta * (v - eg * kq_s[:c, :])
            u = _dot_f32(_unit_lower_inverse(m, eye), rhs)
            ub = u.astype(BF16)
            qk = _dot_nt(qb, kb) * rel
            o = eg * kq_s[c:, :] + _dot(qk.astype(BF16), ub)
            g_end = g_c[c - 1:c, :]
            kd = (jnp.exp(g_end - g_c) * k).astype(BF16)
            st_ref[b, h] = jnp.exp(g_end) * s_mat + _dot_tn(kd, ub)
            o_ref[b, :, hs] = _merge(o, _silu(g_ref[b, :, hs]), on_ref[:, hs])


def _mixers(b, t, dg, z_big, zs, states, prm, layer):
    nh = dg // HEAD_DIM
    hd = HEAD_DIM
    n = b * t
    tp = -(-t // SUBLANES) * SUBLANES
    c = min(CHUNK, tp)
    assert tp % c == 0 and (tp == t or tp == c) and t >= CONV_W - 1 and 4 * nh + GLA_RANK <= SMALL_ROWS
    nc = tp // c
    n_valid = c - (tp - t)
    has_init = states is not None
    bb = 2 if (has_init and b % 2 == 0) else 1

    z3 = z_big.reshape(b, t, 16 * dg)
    zs3 = zs.reshape(b, t, LANES)
    if tp != t:
        z3 = jnp.pad(z3, ((0, 0), (0, tp - t), (0, 0)))
        zs3 = jnp.pad(zs3, ((0, 0), (0, tp - t), (0, 0)))
    zr4 = jnp.swapaxes(zs3[:, :, :SMALL_ROWS].reshape(b, nc, c, SMALL_ROWS), 2, 3)

    grid = (b // bb, nc)
    piece = lambda p: pl.BlockSpec((bb, c, dg), lambda i, j: (i, j, p))
    zc_spec = pl.BlockSpec((bb, c, LANES), lambda i, j: (i, j, 0))
    zr_spec = pl.BlockSpec((bb, None, SMALL_ROWS, c), lambda i, j: (i, j, 0, 0))
    full2 = lambda a: pl.BlockSpec(a.shape, lambda i, j: (0,) * a.ndim)
    mat_spec = pl.BlockSpec((bb, nh, hd, hd), lambda i, j: (i, 0, 0, 0))
    o_spec = pl.BlockSpec((bb, c, dg), lambda i, j: (i, j, 0))
    o_shape = jax.ShapeDtypeStruct((b, tp, dg), BF16)
    mat_shape = jax.ShapeDtypeStruct((b, nh, hd, hd), F32)
    cp = _cparams(("arbitrary", "arbitrary"))
    common = dict(bb=bb, nh=nh, c=c, n_valid=n_valid, has_init=has_init)
    on = prm["out_norm"][layer].reshape(N_MIXERS, 1, dg)
    br, bc, ar, ac = prm["bias_row"][layer], prm["bias_col"][layer], prm["alog_row"][layer], prm["alog_col"][layer]

    n_spec = pl.BlockSpec((bb, nh, hd), lambda i, j: (i, 0, 0))
    m_spec = pl.BlockSpec((bb, 1, nh), lambda i, j: (i, 0, 0))
    ins = [z3, z3, z3, z3, zs3, zr4, br, bc, on[0]]
    specs = [piece(0), piece(1), piece(2), piece(3), zc_spec, zr_spec, full2(br), full2(bc), full2(on[0])]
    if has_init:
        ins += [states[0], states[1], states[2].reshape(b, 1, nh)]
        specs += [mat_spec, n_spec, m_spec]
    o_a, m_c, m_n, m_m = pl.pallas_call(
        functools.partial(_mlstm_kernel, **common),
        grid=grid, in_specs=specs,
        out_specs=[o_spec, mat_spec, n_spec, m_spec],
        out_shape=[o_shape, mat_shape, jax.ShapeDtypeStruct((b, nh, hd), F32),
                   jax.ShapeDtypeStruct((b, 1, nh), F32)],
        compiler_params=cp,
    )(*ins)

    lbp = prm["hgrn_lb"]
    ins = [z3, z3, z3, z3, lbp, on[1]]
    specs = [piece(4), piece(5), piece(6), piece(7), full2(lbp), full2(on[1])]
    if has_init:
        ins.append(states[3])
        specs.append(mat_spec)
    o_b, s_hgrn = pl.pallas_call(
        functools.partial(_gla_kernel, kind="hgrn", layer=layer, **common),
        grid=grid, in_specs=specs, out_specs=[o_spec, mat_spec], out_shape=[o_shape, mat_shape],
        compiler_params=cp,
    )(*ins)

    cw = prm["gdn_conv_w"][layer].reshape(CONV_W, 3, dg)
    cv_spec = pl.BlockSpec((bb, CONV_W - 1, 3, dg), lambda i, j: (i, 0, 0, 0))
    ins = [z3, z3, z3, z3, zs3, zr4, br, bc, ar, ac, cw, on[2]]
    specs = [piece(8), piece(9), piece(10), piece(11), zc_spec, zr_spec, full2(br), full2(bc), full2(ar),
             full2(ac), full2(cw), full2(on[2])]
    if has_init:
        ins += [states[4], states[5].reshape(b, CONV_W - 1, 3, dg)]
        specs += [mat_spec, cv_spec]
    o_c, s_gdn, s_conv = pl.pallas_call(
        functools.partial(_gdn_kernel, **common),
        grid=grid, in_specs=specs,
        out_specs=[o_spec, mat_spec, cv_spec],
        out_shape=[o_shape, mat_shape, jax.ShapeDtypeStruct((b, CONV_W - 1, 3, dg), F32)],
        scratch_shapes=[pltpu.VMEM((bb, 3, SUBLANES, dg), F32)],
        compiler_params=cp,
    )(*ins)

    wg, bg = prm["gla_w_pad"][layer], prm["gla_b_gate"][layer].reshape(1, dg)
    ins = [z3, z3, z3, z3, zs3, wg, bg, on[3]]
    specs = [piece(12), piece(13), piece(14), piece(15), zc_spec, full2(wg), full2(bg), full2(on[3])]
    if has_init:
        ins.append(states[6])
        specs.append(mat_spec)
    o_d, s_gla = pl.pallas_call(
        functools.partial(_gla_kernel, kind="gla", layer=layer, **common),
        grid=grid, in_specs=specs, out_specs=[o_spec, mat_spec], out_shape=[o_shape, mat_shape],
        compiler_params=cp,
    )(*ins)

    outs = [o[:, :t, :].reshape(n, dg) for o in (o_a, o_b, o_c, o_d)]
    new_states = (m_c, m_n, m_m.reshape(b, nh), s_hgrn, s_gdn, s_conv.reshape(b, CONV_W - 1, 3 * dg), s_gla)
    return outs, new_states


def _out_proj_kernel(oa_ref, ob_ref, oc_ref, od_ref, w_ref, x_ref, g_ref, y_ref):
    acc = _dot(oa_ref[...], w_ref[0])
    acc = acc + _dot(ob_ref[...], w_ref[1])
    acc = acc + _dot(oc_ref[...], w_ref[2])
    acc = acc + _dot(od_ref[...], w_ref[3])
    y_ref[...] = x_ref[...] + g_ref[...] * acc


def _out_proj(til, outs, w_out4, x, mod_arr):
    n, d, tm = til.n, til.d, til.tm
    dg = d // N_MIXERS
    tn = min(1024, d)
    o_spec = pl.BlockSpec((tm, dg), lambda i, j: (i, 0))
    if til.per_batch:
        g_spec = pl.BlockSpec((None, 1, tn), lambda i, j: ((i * tm) // til.t, 0, (2 * d) // tn + j))
    else:
        g_spec = pl.BlockSpec((tm, tn), lambda i, j: (i, (2 * d) // tn + j))
    return pl.pallas_call(
        _out_proj_kernel,
        grid=(til.tiles, d // tn),
        in_specs=[o_spec, o_spec, o_spec, o_spec,
                  pl.BlockSpec((N_MIXERS, dg, tn), lambda i, j: (0, 0, j)),
                  pl.BlockSpec((tm, tn), lambda i, j: (i, j)),
                  g_spec],
        out_specs=pl.BlockSpec((tm, tn), lambda i, j: (i, j)),
        out_shape=jax.ShapeDtypeStruct((n, d), F32),
        compiler_params=_cparams(("arbitrary", "arbitrary")),
    )(*outs, w_out4, x, mod_arr)


def _top_desc(s, count):
    rows = float(s.shape[0])
    ri = _iota2(s.shape, 0).astype(F32)
    vals = []
    for r in range(count):
        mx = jnp.max(s, axis=0, keepdims=True)
        vals.append(mx)
        if r < count - 1:
            first = jnp.min(jnp.where(s == mx, ri, rows), axis=0, keepdims=True)
            s = jnp.where(ri == first, -jnp.inf, s)
    return vals


def _cand_pairs():
    return [(a, b) for a in range(PEER_TOPK) for b in range(PEER_TOPK) if (a + 1) * (b + 1) <= PEER_TOPK]


def _route_kernel(x_ref, nw_ref, sc_ref, sh_ref, wq_ref, key_ref,
                  h2_ref, s1_ref, s2_ref, e1_ref, e2_ref, tau_ref, cand_ref):
    @pl.when(pl.program_id(1) == 0)
    def _():
        h2_ref[...] = _rms_mod(x_ref[...], nw_ref[...], sc_ref[...], sh_ref[...]).astype(BF16)

    half = PEER_KEYS
    q = _dot(h2_ref[...], wq_ref[...])
    s1 = _dot_nt(key_ref[0].astype(BF16), q[:, :half].astype(BF16))
    s2 = _dot_nt(key_ref[1].astype(BF16), q[:, half:].astype(BF16))
    v1 = _top_desc(s1, PEER_TOPK)
    v2 = _top_desc(s2, PEER_TOPK)
    pairs = _cand_pairs()
    cand_ref[...] = jnp.full(cand_ref.shape, -jnp.inf, F32)
    for r, (a, b) in enumerate(pairs):
        cand_ref[r:r + 1, :] = v1[a] + v2[b]
    best = _top_desc(cand_ref[...], PEER_TOPK)
    zsum = jnp.zeros_like(best[0])
    for r in range(PEER_TOPK):
        zsum = zsum + jnp.exp(best[r] - best[0])
    s1_ref[...] = s1
    s2_ref[...] = s2
    e1_ref[...] = jnp.exp(s1 - v1[0]) / zsum
    e2_ref[...] = jnp.exp(s2 - v2[0])
    tau_ref[...] = best[PEER_TOPK - 1]


def _route(til, x1, nw, mod_arr, w_q, sub_keys):
    n, d, tm = til.n, til.d, til.tm
    qd = w_q.shape[1] // PEER_HEADS
    n_cand = -(-len(_cand_pairs()) // SUBLANES) * SUBLANES
    tok = pl.BlockSpec((None, PEER_KEYS, tm), lambda i, h: (h, 0, i))
    tok_shape = jax.ShapeDtypeStruct((PEER_HEADS, PEER_KEYS, n), F32)
    return pl.pallas_call(
        _route_kernel,
        grid=(til.tiles, PEER_HEADS),
        in_specs=[
            pl.BlockSpec((tm, d), lambda i, h: (i, 0)),
            pl.BlockSpec((1, d), lambda i, h: (0, 0)),
            til.mod_spec(4, 2),
            til.mod_spec(3, 2),
            pl.BlockSpec((d, qd), lambda i, h: (0, h)),
            pl.BlockSpec((None, 2, PEER_KEYS, qd // 2), lambda i, h: (h, 0, 0, 0)),
        ],
        out_specs=[
            pl.BlockSpec((tm, d), lambda i, h: (i, 0)),
            tok, tok, tok, tok,
            pl.BlockSpec((None, 1, tm), lambda i, h: (h, 0, i)),
        ],
        out_shape=[jax.ShapeDtypeStruct((n, d), BF16), tok_shape, tok_shape, tok_shape, tok_shape,
                   jax.ShapeDtypeStruct((PEER_HEADS, 1, n), F32)],
        scratch_shapes=[pltpu.VMEM((n_cand, tm), F32)],
        compiler_params=_cparams(("arbitrary", "arbitrary")),
    )(x1, nw.reshape(1, d), mod_arr, mod_arr, w_q, sub_keys)


def _peer_kernel(h2_ref, u_ref, v_ref, s1_ref, s2_ref, e1_ref, e2_ref, tau_ref, o_ref, *, te):
    e = pl.program_id(1)

    @pl.when(e == 0)
    def _():
        o_ref[...] = jnp.zeros_like(o_ref)

    act = _dot_nt(u_ref[...], h2_ref[...])
    gel = 0.5 * act * (1.0 + lax.erf(act * (2.0 ** -0.5)))
    groups = te // PEER_KEYS
    parts = []
    for ii in range(groups):
        row = e * groups + ii
        acc = jnp.zeros((PEER_KEYS, act.shape[1]), F32)
        for h in range(PEER_HEADS):
            sm = s1_ref[h, pl.ds(row, 1), :] + s2_ref[h]
            sel = jnp.where(sm >= tau_ref[h], e2_ref[h], 0.0)
            acc = acc + sel * e1_ref[h, pl.ds(row, 1), :]
        parts.append(acc * gel[ii * PEER_KEYS:(ii + 1) * PEER_KEYS, :])
    p = parts[0] if groups == 1 else jnp.concatenate(parts, axis=0)
    o_ref[...] += _dot_tn(p.astype(BF16), v_ref[...])


def _peer(til, h2, routing, u_tab, v_tab):
    n, d, tm = til.n, til.d, til.tm
    ne = u_tab.shape[0]
    te = EXPERT_TILE
    tok = pl.BlockSpec((PEER_HEADS, PEER_KEYS, tm), lambda i, e: (0, 0, i))
    return pl.pallas_call(
        functools.partial(_peer_kernel, te=te),
        grid=(til.tiles, ne // te),
        in_specs=[
            pl.BlockSpec((tm, d), lambda i, e: (i, 0)),
            pl.BlockSpec((te, d), lambda i, e: (e, 0)),
            pl.BlockSpec((te, d), lambda i, e: (e, 0)),
            tok, tok, tok, tok,
            pl.BlockSpec((PEER_HEADS, 1, tm), lambda i, e: (0, 0, i)),
        ],
        out_specs=pl.BlockSpec((tm, d), lambda i, e: (i, 0)),
        out_shape=jax.ShapeDtypeStruct((n, d), F32),
        compiler_params=_cparams(("arbitrary", "arbitrary")),
    )(h2, u_tab, v_tab, *routing)


def _residual_kernel(x_ref, p_ref, g_ref, o_ref):
    o_ref[...] = x_ref[...] + g_ref[...] * p_ref[...]


def _residual_norm_kernel(x_ref, p_ref, g_ref, nw_ref, o_ref):
    x = x_ref[...] + g_ref[...] * p_ref[...]
    o_ref[...] = x * lax.rsqrt(jnp.mean(x * x, axis=-1, keepdims=True) + EPS) * nw_ref[...]


def _residual(til, x1, p, mod_arr, final_norm):
    til = _Tiling(til.b, til.t, til.d, tile=256)
    n, d, tm = til.n, til.d, til.tm
    row = pl.BlockSpec((tm, d), lambda i: (i, 0))
    ins = [x1, p, mod_arr]
    specs = [row, row, til.mod_spec(5, 1)]
    body = _residual_kernel
    if final_norm is not None:
        ins.append(final_norm.reshape(1, d))
        specs.append(pl.BlockSpec((1, d), lambda i: (0, 0)))
        body = _residual_norm_kernel
    return pl.pallas_call(
        body, grid=(til.tiles,), in_specs=specs, out_specs=row,
        out_shape=jax.ShapeDtypeStruct((n, d), F32),
        compiler_params=_cparams(("arbitrary",)),
    )(*ins)


def _trunk(x, mod, states, prm):
    b, t, d = x.shape
    depth = mod.shape[0]
    dg = d // N_MIXERS
    til = _Tiling(b, t, d)
    xf = x.reshape(b * t, d)
    new_states = [[] for _ in range(7)]
    for l in range(depth):
        mod_arr = til.mod_array(mod[l])
        z_big, zs = _in_proj(til, xf, prm["norm_mix"][l], mod_arr, prm["w_big"][l], prm["w_small"][l])
        st_l = None if states is None else tuple(s[l] for s in states)
        outs, new = _mixers(b, t, dg, z_big, zs, st_l, prm, l)
        x1 = _out_proj(til, outs, prm["w_out4"][l], xf, mod_arr)
        h2, *routing = _route(til, x1, prm["norm_ffn"][l], mod_arr, prm["w_q"][l], prm["peer_sub_keys"][l])
        p = _peer(til, h2, routing, prm["peer_u"][l], prm["peer_v"][l])
        xf = _residual(til, x1, p, mod_arr, prm["final_norm"] if l == depth - 1 else None)
        for lst, s in zip(new_states, new):
            lst.append(s)
    return xf.reshape(b, t, d), [jnp.stack(s) for s in new_states]


def _prepare(w_in, mlstm_b_i, mlstm_b_f, gdn_a_log, gdn_dt_bias, gla_w_gate, d):
    depth = w_in.shape[0]
    dg = d // N_MIXERS
    nh = dg // HEAD_DIM
    sizes = (dg, dg, dg, dg, nh, nh, dg, dg, dg, dg, dg, dg, dg, dg, nh, nh, dg, dg, dg, dg, GLA_RANK)
    offs = [0]
    for s in sizes:
        offs.append(offs[-1] + s)
    col = lambda i: w_in[:, :, offs[i]:offs[i + 1]]
    big = [0, 1, 2, 3, 6, 7, 8, 9, 10, 11, 12, 13, 16, 17, 18, 19]
    small = [4, 5, 14, 15, 20]
    w_big = jnp.concatenate([col(i) for i in big], axis=2).astype(BF16)
    n_small = 4 * nh + GLA_RANK
    w_small = jnp.concatenate([col(i) for i in small] + [jnp.zeros((depth, d, LANES - n_small), F32)],
                              axis=2).astype(BF16)
    zeros = lambda k: jnp.zeros((depth, k), F32)
    bias = jnp.concatenate([mlstm_b_i, mlstm_b_f, gdn_dt_bias, zeros(LANES - 3 * nh)], axis=1)
    alog = jnp.concatenate([zeros(2 * nh), gdn_a_log, zeros(LANES - 3 * nh)], axis=1)
    gla_w_pad = jnp.concatenate(
        [jnp.zeros((depth, 4 * nh, dg), F32), gla_w_gate, jnp.zeros((depth, LANES - n_small, dg), F32)], axis=1)
    return dict(
        w_big=w_big, w_small=w_small,
        bias_row=bias.reshape(depth, 1, LANES), bias_col=bias[:, :SMALL_ROWS].reshape(depth, SMALL_ROWS, 1),
        alog_row=alog.reshape(depth, 1, LANES), alog_col=alog[:, :SMALL_ROWS].reshape(depth, SMALL_ROWS, 1),
        gla_w_pad=gla_w_pad)


def kernel(x_prompt, x_sample, c_prompt, c_sample, state_mlstm_C, state_mlstm_n, state_mlstm_m, state_hgrn, state_gdn, state_gdn_conv, state_gla, w_ada, b_ada, norm_mix, norm_ffn, w_in, mlstm_b_i, mlstm_b_f, hgrn_lb, gdn_conv_w, gdn_a_log, gdn_dt_bias, gla_w_gate, gla_b_gate, out_norm, w_out, peer_w_q, peer_sub_keys, peer_u, peer_v, final_norm):
    depth, d = norm_mix.shape
    dg = d // N_MIXERS
    assert d % (N_MIXERS * HEAD_DIM) == 0
    prm = _prepare(w_in, mlstm_b_i, mlstm_b_f, gdn_a_log, gdn_dt_bias, gla_w_gate, d)
    prm.update(
        norm_mix=norm_mix, norm_ffn=norm_ffn, hgrn_lb=hgrn_lb, gdn_conv_w=gdn_conv_w, gla_b_gate=gla_b_gate,
        out_norm=out_norm, final_norm=final_norm, peer_sub_keys=peer_sub_keys,
        w_out4=w_out.astype(BF16).reshape(depth, N_MIXERS, dg, d),
        w_q=peer_w_q.astype(BF16), peer_u=peer_u.astype(BF16), peer_v=peer_v.astype(BF16))

    bp, bs = c_prompt.shape[0], c_sample.shape[0]
    rows = -(-(bp + bs) // SUBLANES) * SUBLANES
    c_all = jnp.concatenate([c_prompt, c_sample, jnp.zeros((rows - bp - bs, d), F32)], axis=0)
    mod = _ada(c_all, w_ada, b_ada)

    y_prompt, p_states = _trunk(x_prompt, mod[:, :bp], None, prm)
    past = (state_mlstm_C, state_mlstm_n, state_mlstm_m, state_hgrn, state_gdn, state_gdn_conv, state_gla)
    y_sample, s_states = _trunk(x_sample, mod[:, bp:bp + bs], past, prm)
    return (y_prompt, y_sample, *p_states, *s_states)
```

```python
import functools
import math

import jax
import jax.numpy as jnp
from jax import lax
from jax.experimental import pallas as pl
from jax.experimental.pallas import tpu as pltpu

F32 = jnp.float32
BF16 = jnp.bfloat16

HEAD_DIM = 256
N_MIXERS = 4
CONV_W = 4
GLA_RANK = 16
GLA_TAU = 16.0
PEER_HEADS = 8
PEER_KEYS = 128
PEER_TOPK = 16
N_MOD = 6
EPS = 1e-6
NEG_BIG = -1e30
MIN_FORGET = 1e-6

LANES = 128
SUBLANES = 8
SMALL_ROWS = 32
CHUNK = 64
SUB = 16
TOKEN_TILE = 512
EXPERT_TILE = 512
VMEM_LIMIT = 56 * 1024 * 1024


def _cparams(sem):
    return pltpu.CompilerParams(dimension_semantics=sem, vmem_limit_bytes=VMEM_LIMIT)


def _dot(a, b):
    return jnp.dot(a, b, preferred_element_type=F32)


def _dot_nt(a, b):
    return lax.dot_general(a, b, (((1,), (1,)), ((), ())), preferred_element_type=F32)


def _dot_tn(a, b):
    return lax.dot_general(a, b, (((0,), (0,)), ((), ())), preferred_element_type=F32)


def _dot_f32(a, b):
    return jnp.dot(a, b, precision=lax.Precision.HIGHEST, preferred_element_type=F32)


def _sigmoid(x):
    return 1.0 / (1.0 + jnp.exp(-x))


def _silu(x):
    return x * _sigmoid(x)


def _log_sigmoid(x):
    return jnp.minimum(x, 0.0) - jnp.log1p(jnp.exp(-jnp.abs(x)))


def _softplus(x):
    return jnp.maximum(x, 0.0) + jnp.log1p(jnp.exp(-jnp.abs(x)))


def _rms_mod(x, nw, sc, sh):
    y = x * lax.rsqrt(jnp.mean(x * x, axis=-1, keepdims=True) + EPS) * nw
    return y * (1.0 + sc) + sh


def _merge(h, gate, onorm):
    hn = h * lax.rsqrt(jnp.mean(h * h, axis=-1, keepdims=True) + EPS)
    return (hn * onorm * gate).astype(BF16)


def _iota2(shape, dim):
    return lax.broadcasted_iota(jnp.int32, shape, dim)


def _row_to_col(r):
    n = r.shape[1]
    eye = _iota2((n, n), 0) == _iota2((n, n), 1)
    return jnp.sum(jnp.where(eye, r, 0.0), axis=1, keepdims=True)


def _cumsum_pair(x_c, x_r, incl, incl_t):
    f_c = jnp.sum(jnp.where(incl, x_r, 0.0), axis=1, keepdims=True)
    f_r = jnp.sum(jnp.where(incl_t, x_c, 0.0), axis=0, keepdims=True)
    return f_c, f_r


def _cumsum_rows(x, tri_b):
    hi = x.astype(BF16)
    r1 = x - hi.astype(F32)
    mid = r1.astype(BF16)
    lo = (r1 - mid.astype(F32)).astype(BF16)
    return _dot(tri_b, hi) + _dot(tri_b, mid) + _dot(tri_b, lo)


def _ada_kernel(c_ref, w_ref, b_ref, o_ref):
    cs = _silu(c_ref[...]).astype(BF16)
    o_ref[...] = _dot(cs, w_ref[...].astype(BF16)) + b_ref[...]


def _ada(c_all, w_ada, b_ada):
    depth, d, n6 = w_ada.shape
    rows = c_all.shape[0]
    tn = 512
    return pl.pallas_call(
        _ada_kernel,
        grid=(depth, n6 // tn),
        in_specs=[
            pl.BlockSpec((rows, d), lambda l, j: (0, 0)),
            pl.BlockSpec((None, d, tn), lambda l, j: (l, 0, j)),
            pl.BlockSpec((None, 1, tn), lambda l, j: (l, 0, j)),
        ],
        out_specs=pl.BlockSpec((None, rows, tn), lambda l, j: (l, 0, j)),
        out_shape=jax.ShapeDtypeStruct((depth, rows, n6), F32),
        compiler_params=_cparams(("arbitrary", "arbitrary")),
    )(c_all, w_ada, b_ada.reshape(depth, 1, n6))


class _Tiling:
    def __init__(self, b, t, d, tile=TOKEN_TILE):
        self.b, self.t, self.d = b, t, d
        self.n = b * t
        self.per_batch = t % LANES == 0
        if self.per_batch:
            self.tm = next(m for m in (tile, 256, LANES) if m <= tile and t % m == 0)
        else:
            self.tm = self.n if self.n <= tile else tile
            assert self.n % self.tm == 0 and self.tm % SUBLANES == 0
        self.tiles = self.n // self.tm

    def mod_array(self, mod_l):
        if self.per_batch:
            return mod_l.reshape(self.b, 1, mod_l.shape[-1])
        return jnp.repeat(mod_l, self.t, axis=0)

    def mod_spec(self, k, grid_rank):
        d, tm, t = self.d, self.tm, self.t
        if self.per_batch:
            if grid_rank == 1:
                return pl.BlockSpec((None, 1, d), lambda i: ((i * tm) // t, 0, k))
            return pl.BlockSpec((None, 1, d), lambda i, j: ((i * tm) // t, 0, k))
        if grid_rank == 1:
            return pl.BlockSpec((tm, d), lambda i: (i, k))
        return pl.BlockSpec((tm, d), lambda i, j: (i, k))


def _in_proj_kernel(x_ref, nw_ref, sc_ref, sh_ref, wb_ref, ws_ref, z_ref, zs_ref, h_scr):
    @pl.when(pl.program_id(1) == 0)
    def _():
        hb = _rms_mod(x_ref[...], nw_ref[...], sc_ref[...], sh_ref[...]).astype(BF16)
        h_scr[...] = hb
        zs_ref[...] = _dot(hb, ws_ref[...])

    z_ref[...] = _dot(h_scr[...], wb_ref[...])


def _in_proj(til, x, nw, mod_arr, w_big, w_small):
    n, d, tm = til.n, til.d, til.tm
    nbig = w_big.shape[1]
    tn = min(1024, nbig)
    return pl.pallas_call(
        _in_proj_kernel,
        grid=(til.tiles, nbig // tn),
        in_specs=[
            pl.BlockSpec((tm, d), lambda i, j: (i, 0)),
            pl.BlockSpec((1, d), lambda i, j: (0, 0)),
            til.mod_spec(1, 2),
            til.mod_spec(0, 2),
            pl.BlockSpec((d, tn), lambda i, j: (0, j)),
            pl.BlockSpec((d, LANES), lambda i, j: (0, 0)),
        ],
        out_specs=[
            pl.BlockSpec((tm, tn), lambda i, j: (i, j)),
            pl.BlockSpec((tm, LANES), lambda i, j: (i, 0)),
        ],
        out_shape=[jax.ShapeDtypeStruct((n, nbig), F32), jax.ShapeDtypeStruct((n, LANES), F32)],
        scratch_shapes=[pltpu.VMEM((tm, d), BF16)],
        compiler_params=_cparams(("arbitrary", "arbitrary")),
    )(x, nw.reshape(1, d), mod_arr, mod_arr, w_big, w_small)


def _mlstm_kernel(*refs, bb, nh, c, n_valid, has_init, n_alias):
    q_ref, k_ref, v_ref, g_ref, zc_ref, zr_ref, br_ref, bc_ref, on_ref = refs[:9]
    n_in = 9
    if has_init:
        c0_ref, n0_ref, m0_ref = refs[9:12]
        n_in = 12
    o_ref, cs_ref, ns_ref, ms_ref = refs[n_in + n_alias:]

    @pl.when(pl.program_id(1) == 0)
    def _():
        if has_init:
            cs_ref[...] = c0_ref[...]
            ns_ref[...] = n0_ref[...]
            ms_ref[...] = m0_ref[...]
        else:
            cs_ref[...] = jnp.zeros_like(cs_ref)
            ns_ref[...] = jnp.zeros_like(ns_ref)
            ms_ref[...] = jnp.zeros_like(ms_ref)

    hd = HEAD_DIM
    scale = hd ** -0.5
    ti = _iota2((c, c), 0)
    si = _iota2((c, c), 1)
    incl = si <= ti
    incl_t = ti <= si
    valid_c = _iota2((c, 1), 0) < n_valid
    valid_r = _iota2((1, c), 1) < n_valid
    for b in range(bb):
        zc = zc_ref[b] + br_ref[...]
        zr = zr_ref[b] + bc_ref[...]
        for h in range(nh):
            hs = slice(h * hd, (h + 1) * hd)
            q = q_ref[b, :, hs]
            k = k_ref[b, :, hs] * scale
            v = v_ref[b, :, hs]
            ig_c = zc[:, h:h + 1]
            lf_c = _log_sigmoid(zc[:, nh + h:nh + h + 1])
            ig_r = zr[h:h + 1, :]
            lf_r = _log_sigmoid(zr[nh + h:nh + h + 1, :])
            if n_valid < c:
                ig_c = jnp.where(valid_c, ig_c, NEG_BIG)
                lf_c = jnp.where(valid_c, lf_c, 0.0)
                ig_r = jnp.where(valid_r, ig_r, NEG_BIG)
                lf_r = jnp.where(valid_r, lf_r, 0.0)
            cm = cs_ref[b, h]
            nv = ns_ref[b, h:h + 1, :]
            m0 = ms_ref[b, :, h:h + 1]

            f_c, f_r = _cumsum_pair(lf_c, lf_r, incl, incl_t)
            raw = f_c - f_r + ig_r
            a = f_c + m0
            m_t = jnp.maximum(a, jnp.max(jnp.where(incl, raw, NEG_BIG), axis=1, keepdims=True))
            p = jnp.where(incl, jnp.exp(jnp.where(incl, raw - m_t, 0.0)), 0.0)
            qb = q.astype(BF16)
            kb = k.astype(BF16)
            vb = v.astype(BF16)
            s = _dot_nt(qb, kb) * p
            inter = jnp.exp(a - m_t)
            num = inter * _dot(qb, cm.astype(BF16)) + _dot(s.astype(BF16), vb)
            den = inter * jnp.sum(q * nv, axis=1, keepdims=True) + jnp.sum(s, axis=1, keepdims=True)
            hh = num / jnp.maximum(jnp.abs(den), jnp.exp(-m_t))
            m_end = m_t[c - 1:c, :]
            w_end = jnp.exp(f_c[c - 1:c, :] - f_c + ig_c - m_end)
            dec = jnp.exp(a[c - 1:c, :] - m_end)
            kw = w_end * k
            cs_ref[b, h] = dec * cm + _dot_tn(kw.astype(BF16), vb)
            ns_ref[b, h:h + 1, :] = dec * nv + jnp.sum(kw, axis=0, keepdims=True)
            ms_ref[b, :, h:h + 1] = m_end
            o_ref[b, :, hs] = _merge(hh, _sigmoid(g_ref[b, :, hs]), on_ref[:, hs])


def _gla_unit(q, k, v, lg, s_mat, tri_b):
    c, hd = q.shape
    sb = min(SUB, c)
    nb = c // sb
    g = _cumsum_rows(lg, tri_b)
    o = _dot((q * jnp.exp(g)).astype(BF16), s_mat.astype(BF16))

    if nb > 1:
        qparts, kparts = [], []
        for j in range(nb - 1):
            r1 = (j + 1) * sb
            g_end = g[r1 - 1:r1, :]
            qj = q[r1:, :] * jnp.exp(g[r1:, :] - g_end)
            kj = k[j * sb:r1, :] * jnp.exp(g_end - g[j * sb:r1, :])
            qparts.append(jnp.concatenate([jnp.zeros((r1, hd), F32), qj], axis=0).astype(BF16))
            pieces = [kj]
            if j > 0:
                pieces.insert(0, jnp.zeros((j * sb, hd), F32))
            pieces.append(jnp.zeros((c - r1, hd), F32))
            kparts.append(jnp.concatenate(pieces, axis=0).astype(BF16))
        a_off = _dot_nt(jnp.concatenate(qparts, axis=1), jnp.concatenate(kparts, axis=1))
    else:
        a_off = jnp.zeros((c, c), F32)

    lane = _iota2((sb, c), 1)
    trow = _iota2((sb, 1), 0)
    strips = []
    for i in range(nb):
        r0 = i * sb
        qi, ki, gi = q[r0:r0 + sb, :], k[r0:r0 + sb, :], g[r0:r0 + sb, :]
        strip = a_off[r0:r0 + sb, :]
        for s in range(sb):
            msk = trow >= s
            w = jnp.where(msk, jnp.exp(jnp.where(msk, gi - gi[s:s + 1, :], 0.0)), 0.0)
            col = jnp.sum(qi * ki[s:s + 1, :] * w, axis=1, keepdims=True)
            strip = jnp.where(lane == r0 + s, col, strip)
        strips.append(strip)
    a = strips[0] if nb == 1 else jnp.concatenate(strips, axis=0)
    vb = v.astype(BF16)
    o = o + _dot(a.astype(BF16), vb)

    g_end = g[c - 1:c, :]
    kt = (k * jnp.exp(g_end - g)).astype(BF16)
    s_new = _row_to_col(jnp.exp(g_end)) * s_mat + _dot_tn(kt, vb)
    return o, s_new


def _gla_kernel(*refs, bb, nh, c, n_valid, has_init, n_alias, kind, layer):
    if kind == "hgrn":
        q_ref, k_ref, v_ref, g_ref, par_ref, on_ref = refs[:6]
        rest = refs[6:]
    else:
        q_ref, k_ref, v_ref, g_ref, zc_ref, wg_ref, bg_ref, on_ref = refs[:8]
        rest = refs[8:]
    if has_init:
        s0_ref = rest[0]
        rest = rest[1:]
    o_ref, st_ref = rest[n_alias:]

    @pl.when(pl.program_id(1) == 0)
    def _():
        if has_init:
            st_ref[...] = s0_ref[...]
        else:
            st_ref[...] = jnp.zeros_like(st_ref)

    hd = HEAD_DIM
    tri_b = (_iota2((c, c), 1) <= _iota2((c, c), 0)).astype(BF16)
    valid_c = _iota2((c, 1), 0) < n_valid
    if kind == "hgrn":
        lbp = par_ref[...]
        ex = jnp.exp(lbp - jnp.max(lbp, axis=0, keepdims=True))
        lbs = ex / jnp.sum(ex, axis=0, keepdims=True)
        lb = jnp.zeros_like(lbs[0:1, :])
        for j in range(1, layer + 1):
            lb = lb + lbs[j:j + 1, :]
    for b in range(bb):
        if kind == "gla":
            gate_in = _dot(zc_ref[b].astype(BF16), wg_ref[...].astype(BF16)) + bg_ref[...]
        for h in range(nh):
            hs = slice(h * hd, (h + 1) * hd)
            if kind == "hgrn":
                fg = k_ref[b, :, hs]
                lbh = lb[:, hs]
                f = lbh + (1.0 - lbh) * _sigmoid(fg)
                lg = jnp.log(jnp.maximum(f, MIN_FORGET))
                k = (1.0 - lbh) * _sigmoid(-fg)
                q = _silu(q_ref[b, :, hs])
            else:
                lg = _log_sigmoid(gate_in[:, hs]) / GLA_TAU
                k = k_ref[b, :, hs]
                q = q_ref[b, :, hs] * (hd ** -0.5)
            v = v_ref[b, :, hs]
            if n_valid < c:
                lg = jnp.where(valid_c, lg, 0.0)
                k = jnp.where(valid_c, k, 0.0)
            o, s_new = _gla_unit(q, k, v, lg, st_ref[b, h], tri_b)
            st_ref[b, h] = s_new
            o_ref[b, :, hs] = _merge(o, _silu(g_ref[b, :, hs]), on_ref[:, hs])


def _gdn_kernel(*refs, bb, nh, c, n_valid, has_init, n_alias):
    (q_ref, k_ref, v_ref, g_ref, zc_ref, zr_ref, br_ref, bc_ref, ar_ref, ac_ref, cw_ref, on_ref) = refs[:12]
    n_in = 12
    if has_init:
        s0_ref, cv0_ref = refs[12:14]
        n_in = 14
    o_ref, st_ref, cvo_ref, tail_ref = refs[n_in + n_alias:]
    t_id = pl.program_id(1)
    n_chunks = pl.num_programs(1)
    tail_rows = SUBLANES
    n_buf = CONV_W - 1

    @pl.when(t_id == 0)
    def _():
        tail_ref[...] = jnp.zeros_like(tail_ref)
        if has_init:
            st_ref[...] = s0_ref[...]
            for b in range(bb):
                for pc in range(3):
                    tail_ref[b, pc, tail_rows - n_buf:tail_rows, :] = cv0_ref[b, :, pc, :]
        else:
            st_ref[...] = jnp.zeros_like(st_ref)

    hd = HEAD_DIM
    scale = hd ** -0.5
    ti = _iota2((c, c), 0)
    si = _iota2((c, c), 1)
    incl = si <= ti
    incl_t = ti <= si
    strict = si < ti
    eye = (si == ti).astype(F32)
    valid_c = _iota2((c, 1), 0) < n_valid
    valid_r = _iota2((1, c), 1) < n_valid
    raw_refs = (q_ref, k_ref, v_ref)
    for b in range(bb):
        conv = []
        for pc in range(3):
            u = raw_refs[pc][b]
            ext = jnp.concatenate([tail_ref[b, pc], u], axis=0)
            acc = u * cw_ref[CONV_W - 1:CONV_W, pc, :]
            for j in range(1, CONV_W):
                shifted = pltpu.roll(ext, j, axis=0)[tail_rows:tail_rows + c, :]
                acc = acc + shifted * cw_ref[CONV_W - 1 - j:CONV_W - j, pc, :]
            conv.append(_silu(acc))
            tail_ref[b, pc] = u[c - tail_rows:c, :]

        @pl.when(t_id == n_chunks - 1)
        def _():
            for pc in range(3):
                cvo_ref[b, :, pc, :] = raw_refs[pc][b, n_valid - n_buf:n_valid, :]

        zc = zc_ref[b] + br_ref[...]
        zr = zr_ref[b] + bc_ref[...]
        units = []
        for h in range(nh):
            hs = slice(h * hd, (h + 1) * hd)
            q = conv[0][:, hs]
            k = conv[1][:, hs]
            v = conv[2][:, hs]
            q = q * lax.rsqrt(jnp.sum(q * q, axis=1, keepdims=True) + EPS) * scale
            k = k * lax.rsqrt(jnp.sum(k * k, axis=1, keepdims=True) + EPS)
            ca, cb = 2 * nh + h, 3 * nh + h
            lg_c = -jnp.exp(ar_ref[:, ca:ca + 1]) * _softplus(zc[:, ca:ca + 1])
            lg_r = -jnp.exp(ac_ref[ca:ca + 1, :]) * _softplus(zr[ca:ca + 1, :])
            beta = _sigmoid(zc[:, cb:cb + 1])
            if n_valid < c:
                lg_c = jnp.where(valid_c, lg_c, 0.0)
                lg_r = jnp.where(valid_r, lg_r, 0.0)
                beta = jnp.where(valid_c, beta, 0.0)
            s_mat = st_ref[b, h]

            g_c, g_r = _cumsum_pair(lg_c, lg_r, incl, incl_t)
            eg = jnp.exp(g_c)
            rel = jnp.where(incl, jnp.exp(jnp.where(incl, g_c - g_r, 0.0)), 0.0)
            qb = q.astype(BF16)
            kb = k.astype(BF16)
            kq_s = _dot(jnp.concatenate([kb, qb], axis=0), s_mat.astype(BF16))
            m = jnp.where(strict, beta * rel * _dot_nt(kb, kb), 0.0)
            rhs = beta * (v - eg * kq_s[:c, :])
            qk = (_dot_nt(qb, kb) * rel).astype(BF16)
            units.append(dict(hs=hs, k=k, s_mat=s_mat, g_c=g_c, eg=eg, m=m, rhs=rhs, qk=qk, qs=kq_s[c:, :]))

        tinv = [eye - un["m"] for un in units]
        pw = [_dot_f32(un["m"], un["m"]) for un in units]
        n_it = int(math.log2(c)) - 1
        for it in range(n_it):
            tinv = [t + _dot_f32(t, p) for t, p in zip(tinv, pw)]
            if it < n_it - 1:
                pw = [_dot_f32(p, p) for p in pw]

        for un, t in zip(units, tinv):
            ub = _dot_f32(t, un["rhs"]).astype(BF16)
            o = un["eg"] * un["qs"] + _dot(un["qk"], ub)
            g_end = un["g_c"][c - 1:c, :]
            kd = (jnp.exp(g_end - un["g_c"]) * un["k"]).astype(BF16)
            h = un["hs"].start // hd
            st_ref[b, h] = jnp.exp(g_end) * un["s_mat"] + _dot_tn(kd, ub)
            o_ref[b, :, un["hs"]] = _merge(o, _silu(g_ref[b, :, un["hs"]]), on_ref[:, un["hs"]])


def _mixers(b, t, dg, z_big, zs, states, prev, prm, layer, depth):
    nh = dg // HEAD_DIM
    hd = HEAD_DIM
    n = b * t
    tp = -(-t // SUBLANES) * SUBLANES
    c = min(CHUNK, tp)
    assert tp % c == 0 and (tp == t or tp == c) and t >= CONV_W - 1 and 4 * nh + GLA_RANK <= SMALL_ROWS
    nc = tp // c
    n_valid = c - (tp - t)
    has_init = states is not None
    bb = 2 if (has_init and b % 2 == 0) else 1

    z3 = z_big.reshape(b, t, 16 * dg)
    zs3 = zs.reshape(b, t, LANES)
    if tp != t:
        z3 = jnp.pad(z3, ((0, 0), (0, tp - t), (0, 0)))
        zs3 = jnp.pad(zs3, ((0, 0), (0, tp - t), (0, 0)))
    zr4 = jnp.swapaxes(zs3[:, :, :SMALL_ROWS].reshape(b, nc, c, SMALL_ROWS), 2, 3)

    grid = (b // bb, nc)
    piece = lambda p: pl.BlockSpec((bb, c, dg), lambda i, j: (i, j, p))
    zc_spec = pl.BlockSpec((bb, c, LANES), lambda i, j: (i, j, 0))
    zr_spec = pl.BlockSpec((bb, None, SMALL_ROWS, c), lambda i, j: (i, j, 0, 0))
    full2 = lambda a: pl.BlockSpec(a.shape, lambda i, j: (0,) * a.ndim)
    mat_spec = pl.BlockSpec((None, bb, nh, hd, hd), lambda i, j: (layer, i, 0, 0, 0))
    n_spec = pl.BlockSpec((None, bb, nh, hd), lambda i, j: (layer, i, 0, 0))
    m_spec = pl.BlockSpec((None, bb, 1, nh), lambda i, j: (layer, i, 0, 0))
    cv_spec = pl.BlockSpec((None, bb, CONV_W - 1, 3, dg), lambda i, j: (layer, i, 0, 0, 0))
    mat_shape = jax.ShapeDtypeStruct((depth, b, nh, hd, hd), F32)
    n_shape = jax.ShapeDtypeStruct((depth, b, nh, hd), F32)
    m_shape = jax.ShapeDtypeStruct((depth, b, 1, nh), F32)
    cv_shape = jax.ShapeDtypeStruct((depth, b, CONV_W - 1, 3, dg), F32)
    any_spec = pl.BlockSpec(memory_space=pl.ANY)
    o_spec = pl.BlockSpec((bb, c, dg), lambda i, j: (i, j, 0))
    o_shape = jax.ShapeDtypeStruct((b, tp, dg), BF16)
    cp = _cparams(("arbitrary", "arbitrary"))
    on = prm["out_norm"][layer].reshape(N_MIXERS, 1, dg)
    br, bc, ar, ac = prm["bias_row"][layer], prm["bias_col"][layer], prm["alog_row"][layer], prm["alog_col"][layer]

    def call(body, ins, specs, init, state_specs, state_shapes, prev_arrays, scratch=()):
        ins, specs = list(ins), list(specs)
        if has_init:
            ins += init
            specs += state_specs
        n_alias = 0 if prev_arrays is None else len(prev_arrays)
        aliases = {}
        if n_alias:
            aliases = {len(ins) + k: 1 + k for k in range(n_alias)}
            ins += list(prev_arrays)
            specs += [any_spec] * n_alias
        return pl.pallas_call(
            functools.partial(body, bb=bb, nh=nh, c=c, n_valid=n_valid, has_init=has_init, n_alias=n_alias),
            grid=grid, in_specs=specs, out_specs=[o_spec] + list(state_specs),
            out_shape=[o_shape] + list(state_shapes), scratch_shapes=list(scratch),
            input_output_aliases=aliases, compiler_params=cp,
        )(*ins)

    st = states
    pv = prev
    o_a, m_c, m_n, m_m = call(
        _mlstm_kernel, [z3, z3, z3, z3, zs3, zr4, br, bc, on[0]],
        [piece(0), piece(1), piece(2), piece(3), zc_spec, zr_spec, full2(br), full2(bc), full2(on[0])],
        None if st is None else [st[0], st[1], st[2].reshape(depth, b, 1, nh)],
        [mat_spec, n_spec, m_spec], [mat_shape, n_shape, m_shape],
        None if pv is None else [pv[0], pv[1], pv[2]])

    lbp = prm["hgrn_lb"]
    o_b, s_hgrn = call(
        functools.partial(_gla_kernel, kind="hgrn", layer=layer), [z3, z3, z3, z3, lbp, on[1]],
        [piece(4), piece(5), piece(6), piece(7), full2(lbp), full2(on[1])],
        None if st is None else [st[3]], [mat_spec], [mat_shape], None if pv is None else [pv[3]])

    cw = prm["gdn_conv_w"][layer].reshape(CONV_W, 3, dg)
    o_c, s_gdn, s_conv = call(
        _gdn_kernel, [z3, z3, z3, z3, zs3, zr4, br, bc, ar, ac, cw, on[2]],
        [piece(8), piece(9), piece(10), piece(11), zc_spec, zr_spec, full2(br), full2(bc), full2(ar),
         full2(ac), full2(cw), full2(on[2])],
        None if st is None else [st[4], st[5].reshape(depth, b, CONV_W - 1, 3, dg)],
        [mat_spec, cv_spec], [mat_shape, cv_shape], None if pv is None else [pv[4], pv[5]],
        scratch=[pltpu.VMEM((bb, 3, SUBLANES, dg), F32)])

    wg, bg = prm["gla_w_pad"][layer], prm["gla_b_gate"][layer].reshape(1, dg)
    o_d, s_gla = call(
        functools.partial(_gla_kernel, kind="gla", layer=layer), [z3, z3, z3, z3, zs3, wg, bg, on[3]],
        [piece(12), piece(13), piece(14), piece(15), zc_spec, full2(wg), full2(bg), full2(on[3])],
        None if st is None else [st[6]], [mat_spec], [mat_shape], None if pv is None else [pv[6]])

    outs = [o[:, :t, :].reshape(n, dg) for o in (o_a, o_b, o_c, o_d)]
    return outs, (m_c, m_n, m_m, s_hgrn, s_gdn, s_conv, s_gla)


def _out_proj_kernel(oa_ref, ob_ref, oc_ref, od_ref, w_ref, x_ref, g_ref, y_ref):
    acc = _dot(oa_ref[...], w_ref[0])
    acc = acc + _dot(ob_ref[...], w_ref[1])
    acc = acc + _dot(oc_ref[...], w_ref[2])
    acc = acc + _dot(od_ref[...], w_ref[3])
    y_ref[...] = x_ref[...] + g_ref[...] * acc


def _out_proj(til, outs, w_out4, x, mod_arr):
    n, d, tm = til.n, til.d, til.tm
    dg = d // N_MIXERS
    tn = min(1024, d)
    o_spec = pl.BlockSpec((tm, dg), lambda i, j: (i, 0))
    if til.per_batch:
        g_spec = pl.BlockSpec((None, 1, tn), lambda i, j: ((i * tm) // til.t, 0, (2 * d) // tn + j))
    else:
        g_spec = pl.BlockSpec((tm, tn), lambda i, j: (i, (2 * d) // tn + j))
    return pl.pallas_call(
        _out_proj_kernel,
        grid=(til.tiles, d // tn),
        in_specs=[o_spec, o_spec, o_spec, o_spec,
                  pl.BlockSpec((N_MIXERS, dg, tn), lambda i, j: (0, 0, j)),
                  pl.BlockSpec((tm, tn), lambda i, j: (i, j)),
                  g_spec],
        out_specs=pl.BlockSpec((tm, tn), lambda i, j: (i, j)),
        out_shape=jax.ShapeDtypeStruct((n, d), F32),
        compiler_params=_cparams(("arbitrary", "arbitrary")),
    )(*outs, w_out4, x, mod_arr)


def _top_desc(s, count):
    rows = float(s.shape[0])
    ri = _iota2(s.shape, 0).astype(F32)
    vals = []
    for r in range(count):
        mx = jnp.max(s, axis=0, keepdims=True)
        vals.append(mx)
        if r < count - 1:
            first = jnp.min(jnp.where(s == mx, ri, rows), axis=0, keepdims=True)
            s = jnp.where(ri == first, -jnp.inf, s)
    return vals


def _cand_pairs():
    return [(a, b) for a in range(PEER_TOPK) for b in range(PEER_TOPK) if (a + 1) * (b + 1) <= PEER_TOPK]


def _route_kernel(x_ref, nw_ref, sc_ref, sh_ref, wq_ref, key_ref,
                  h2t_ref, s1_ref, s2_ref, e1_ref, e2_ref, tau_ref, cand_ref, h2_scr):
    @pl.when(pl.program_id(1) == 0)
    def _():
        h2 = _rms_mod(x_ref[...], nw_ref[...], sc_ref[...], sh_ref[...])
        h2_scr[...] = h2.astype(BF16)
        h2t_ref[...] = h2.T.astype(BF16)

    half = PEER_KEYS
    q = _dot(h2_scr[...], wq_ref[...])
    s1 = _dot_nt(key_ref[0].astype(BF16), q[:, :half].astype(BF16))
    s2 = _dot_nt(key_ref[1].astype(BF16), q[:, half:].astype(BF16))
    v1 = _top_desc(s1, PEER_TOPK)
    v2 = _top_desc(s2, PEER_TOPK)
    pairs = _cand_pairs()
    cand_ref[...] = jnp.full(cand_ref.shape, -jnp.inf, F32)
    for r, (a, b) in enumerate(pairs):
        cand_ref[r:r + 1, :] = v1[a] + v2[b]
    best = _top_desc(cand_ref[...], PEER_TOPK)
    zsum = jnp.zeros_like(best[0])
    for r in range(PEER_TOPK):
        zsum = zsum + jnp.exp(best[r] - best[0])
    s1_ref[...] = s1
    s2_ref[...] = s2
    e1_ref[...] = jnp.exp(s1 - v1[0]) / zsum
    e2_ref[...] = jnp.exp(s2 - v2[0])
    tau_ref[...] = best[PEER_TOPK - 1]


def _route(til, x1, nw, mod_arr, w_q, sub_keys):
    n, d, tm = til.n, til.d, til.tm
    qd = w_q.shape[1] // PEER_HEADS
    n_cand = -(-len(_cand_pairs()) // SUBLANES) * SUBLANES
    tok = pl.BlockSpec((None, PEER_KEYS, tm), lambda i, h: (h, 0, i))
    tok_shape = jax.ShapeDtypeStruct((PEER_HEADS, PEER_KEYS, n), F32)
    return pl.pallas_call(
        _route_kernel,
        grid=(til.tiles, PEER_HEADS),
        in_specs=[
            pl.BlockSpec((tm, d), lambda i, h: (i, 0)),
            pl.BlockSpec((1, d), lambda i, h: (0, 0)),
            til.mod_spec(4, 2),
            til.mod_spec(3, 2),
            pl.BlockSpec((d, qd), lambda i, h: (0, h)),
            pl.BlockSpec((None, 2, PEER_KEYS, qd // 2), lambda i, h: (h, 0, 0, 0)),
        ],
        out_specs=[
            pl.BlockSpec((d, tm), lambda i, h: (0, i)),
            tok, tok, tok, tok,
            pl.BlockSpec((None, 1, tm), lambda i, h: (h, 0, i)),
        ],
        out_shape=[jax.ShapeDtypeStruct((d, n), BF16), tok_shape, tok_shape, tok_shape, tok_shape,
                   jax.ShapeDtypeStruct((PEER_HEADS, 1, n), F32)],
        scratch_shapes=[pltpu.VMEM((n_cand, tm), F32), pltpu.VMEM((tm, d), BF16)],
        compiler_params=_cparams(("arbitrary", "arbitrary")),
    )(x1, nw.reshape(1, d), mod_arr, mod_arr, w_q, sub_keys)


def _peer_kernel(h2t_ref, u_ref, vt_ref, s1_ref, s2_ref, e1_ref, e2_ref, tau_ref, o_ref, act_scr, p_scr, *, te, n_et):
    e = pl.program_id(1)

    @pl.when(e == 0)
    def _():
        o_ref[...] = jnp.zeros_like(o_ref)
        act_scr[...] = jnp.zeros_like(act_scr)
        p_scr[...] = jnp.zeros_like(p_scr)

    o_ref[...] += _dot(vt_ref[...], p_scr[...])

    groups = te // PEER_KEYS
    tile = jnp.clip(e - 1, 0, n_et - 1)
    for ii in range(groups):
        rows = slice(ii * PEER_KEYS, (ii + 1) * PEER_KEYS)
        act = act_scr[rows, :]
        gel = 0.5 * act * (1.0 + lax.erf(act * (2.0 ** -0.5)))
        row = tile * groups + ii
        acc = jnp.zeros(act.shape, F32)
        for h in range(PEER_HEADS):
            sm = s1_ref[h, pl.ds(row, 1), :] + s2_ref[h]
            sel = jnp.where(sm >= tau_ref[h], e2_ref[h], 0.0)
            acc = acc + sel * e1_ref[h, pl.ds(row, 1), :]
        p_scr[rows, :] = (acc * gel).astype(BF16)

    act_scr[...] = _dot(u_ref[...], h2t_ref[...])


def _peer(til, h2, routing, u_tab, vt_tab):
    n, d, tm = til.n, til.d, til.tm
    ne = u_tab.shape[0]
    te = EXPERT_TILE
    n_et = ne // te
    once = pl.Buffered(1)
    tok = pl.BlockSpec((PEER_HEADS, PEER_KEYS, tm), lambda i, e: (0, 0, i), pipeline_mode=once)
    return pl.pallas_call(
        functools.partial(_peer_kernel, te=te, n_et=n_et),
        grid=(til.tiles, n_et + 2),
        in_specs=[
            pl.BlockSpec((d, tm), lambda i, e: (0, i), pipeline_mode=once),
            pl.BlockSpec((te, d), lambda i, e: (jnp.minimum(e, n_et - 1), 0)),
            pl.BlockSpec((d, te), lambda i, e: (0, jnp.clip(e - 2, 0, n_et - 1))),
            tok, tok, tok, tok,
            pl.BlockSpec((PEER_HEADS, 1, tm), lambda i, e: (0, 0, i), pipeline_mode=once),
        ],
        out_specs=pl.BlockSpec((d, tm), lambda i, e: (0, i), pipeline_mode=once),
        out_shape=jax.ShapeDtypeStruct((d, n), F32),
        scratch_shapes=[pltpu.VMEM((te, tm), F32), pltpu.VMEM((te, tm), BF16)],
        compiler_params=_cparams(("arbitrary", "arbitrary")),
    )(h2, u_tab, vt_tab, *routing)


def _residual_kernel(x_ref, pt_ref, g_ref, o_ref):
    o_ref[...] = x_ref[...] + g_ref[...] * pt_ref[...].T


def _residual_norm_kernel(x_ref, pt_ref, g_ref, nw_ref, o_ref):
    x = x_ref[...] + g_ref[...] * pt_ref[...].T
    o_ref[...] = x * lax.rsqrt(jnp.mean(x * x, axis=-1, keepdims=True) + EPS) * nw_ref[...]


def _residual(til, x1, p, mod_arr, final_norm):
    til = _Tiling(til.b, til.t, til.d, tile=256)
    n, d, tm = til.n, til.d, til.tm
    row = pl.BlockSpec((tm, d), lambda i: (i, 0))
    ins = [x1, p, mod_arr]
    specs = [row, pl.BlockSpec((d, tm), lambda i: (0, i)), til.mod_spec(5, 1)]
    body = _residual_kernel
    if final_norm is not None:
        ins.append(final_norm.reshape(1, d))
        specs.append(pl.BlockSpec((1, d), lambda i: (0, 0)))
        body = _residual_norm_kernel
    return pl.pallas_call(
        body, grid=(til.tiles,), in_specs=specs, out_specs=row,
        out_shape=jax.ShapeDtypeStruct((n, d), F32),
        compiler_params=_cparams(("arbitrary",)),
    )(*ins)


def _trunk(x, mod, states, prm):
    b, t, d = x.shape
    depth = mod.shape[0]
    dg = d // N_MIXERS
    til = _Tiling(b, t, d)
    xf = x.reshape(b * t, d)
    new = None
    for l in range(depth):
        mod_arr = til.mod_array(mod[l])
        z_big, zs = _in_proj(til, xf, prm["norm_mix"][l], mod_arr, prm["w_big"][l], prm["w_small"][l])
        outs, new = _mixers(b, t, dg, z_big, zs, states, new, prm, l, depth)
        x1 = _out_proj(til, outs, prm["w_out4"][l], xf, mod_arr)
        h2, *routing = _route(til, x1, prm["norm_ffn"][l], mod_arr, prm["w_q"][l], prm["peer_sub_keys"][l])
        p = _peer(til, h2, routing, prm["peer_u"][l], prm["peer_vt"][l])
        xf = _residual(til, x1, p, mod_arr, prm["final_norm"] if l == depth - 1 else None)
    m_c, m_n, m_m, s_hgrn, s_gdn, s_conv, s_gla = new
    nh = dg // HEAD_DIM
    new_states = [m_c, m_n, m_m.reshape(depth, b, nh), s_hgrn, s_gdn,
                  s_conv.reshape(depth, b, CONV_W - 1, 3 * dg), s_gla]
    return xf.reshape(b, t, d), new_states


def _prepare(w_in, mlstm_b_i, mlstm_b_f, gdn_a_log, gdn_dt_bias, gla_w_gate, d):
    depth = w_in.shape[0]
    dg = d // N_MIXERS
    nh = dg // HEAD_DIM
    sizes = (dg, dg, dg, dg, nh, nh, dg, dg, dg, dg, dg, dg, dg, dg, nh, nh, dg, dg, dg, dg, GLA_RANK)
    offs = [0]
    for s in sizes:
        offs.append(offs[-1] + s)
    col = lambda i: w_in[:, :, offs[i]:offs[i + 1]]
    big = [0, 1, 2, 3, 6, 7, 8, 9, 10, 11, 12, 13, 16, 17, 18, 19]
    small = [4, 5, 14, 15, 20]
    w_big = jnp.concatenate([col(i) for i in big], axis=2).astype(BF16)
    n_small = 4 * nh + GLA_RANK
    w_small = jnp.concatenate([col(i) for i in small] + [jnp.zeros((depth, d, LANES - n_small), F32)],
                              axis=2).astype(BF16)
    zeros = lambda k: jnp.zeros((depth, k), F32)
    bias = jnp.concatenate([mlstm_b_i, mlstm_b_f, gdn_dt_bias, zeros(LANES - 3 * nh)], axis=1)
    alog = jnp.concatenate([zeros(2 * nh), gdn_a_log, zeros(LANES - 3 * nh)], axis=1)
    gla_w_pad = jnp.concatenate(
        [jnp.zeros((depth, 4 * nh, dg), F32), gla_w_gate, jnp.zeros((depth, LANES - n_small, dg), F32)], axis=1)
    return dict(
        w_big=w_big, w_small=w_small,
        bias_row=bias.reshape(depth, 1, LANES), bias_col=bias[:, :SMALL_ROWS].reshape(depth, SMALL_ROWS, 1),
        alog_row=alog.reshape(depth, 1, LANES), alog_col=alog[:, :SMALL_ROWS].reshape(depth, SMALL_ROWS, 1),
        gla_w_pad=gla_w_pad)


def kernel(x_prompt, x_sample, c_prompt, c_sample, state_mlstm_C, state_mlstm_n, state_mlstm_m, state_hgrn, state_gdn, state_gdn_conv, state_gla, w_ada, b_ada, norm_mix, norm_ffn, w_in, mlstm_b_i, mlstm_b_f, hgrn_lb, gdn_conv_w, gdn_a_log, gdn_dt_bias, gla_w_gate, gla_b_gate, out_norm, w_out, peer_w_q, peer_sub_keys, peer_u, peer_v, final_norm):
    depth, d = norm_mix.shape
    dg = d // N_MIXERS
    assert d % (N_MIXERS * HEAD_DIM) == 0
    prm = _prepare(w_in, mlstm_b_i, mlstm_b_f, gdn_a_log, gdn_dt_bias, gla_w_gate, d)
    prm.update(
        norm_mix=norm_mix, norm_ffn=norm_ffn, hgrn_lb=hgrn_lb, gdn_conv_w=gdn_conv_w, gla_b_gate=gla_b_gate,
        out_norm=out_norm, final_norm=final_norm, peer_sub_keys=peer_sub_keys,
        w_out4=w_out.astype(BF16).reshape(depth, N_MIXERS, dg, d),
        w_q=peer_w_q.astype(BF16), peer_u=peer_u.astype(BF16),
        peer_vt=jnp.swapaxes(peer_v, 1, 2).astype(BF16))

    bp, bs = c_prompt.shape[0], c_sample.shape[0]
    rows = -(-(bp + bs) // SUBLANES) * SUBLANES
    c_all = jnp.concatenate([c_prompt, c_sample, jnp.zeros((rows - bp - bs, d), F32)], axis=0)
    mod = _ada(c_all, w_ada, b_ada)

    y_prompt, p_states = _trunk(x_prompt, mod[:, :bp], None, prm)
    past = (state_mlstm_C, state_mlstm_n, state_mlstm_m, state_hgrn, state_gdn, state_gdn_conv, state_gla)
    y_sample, s_states = _trunk(x_sample, mod[:, bp:bp + bs], past, prm)
    return (y_prompt, y_sample, *p_states, *s_states)
```

```python
import functools
import math

import jax
import jax.numpy as jnp
from jax import lax
from jax.experimental import pallas as pl
from jax.experimental.pallas import tpu as pltpu

F32 = jnp.float32
BF16 = jnp.bfloat16

HEAD_DIM = 256
N_MIXERS = 4
CONV_W = 4
GLA_RANK = 16
GLA_TAU = 16.0
PEER_HEADS = 8
PEER_KEYS = 128
PEER_TOPK = 16
N_MOD = 6
EPS = 1e-6
NEG_BIG = -1e30
MIN_FORGET = 1e-6

LANES = 128
SUBLANES = 8
SMALL_ROWS = 32
CHUNK = 64
SUB = 16
TOKEN_TILE = 512
EXPERT_TILE = 512
VMEM_LIMIT = 56 * 1024 * 1024


def _cparams(sem):
    return pltpu.CompilerParams(dimension_semantics=sem, vmem_limit_bytes=VMEM_LIMIT)


def _dot(a, b):
    return jnp.dot(a, b, preferred_element_type=F32)


def _dot_nt(a, b):
    return lax.dot_general(a, b, (((1,), (1,)), ((), ())), preferred_element_type=F32)


def _dot_tn(a, b):
    return lax.dot_general(a, b, (((0,), (0,)), ((), ())), preferred_element_type=F32)


def _dot_f32(a, b):
    return jnp.dot(a, b, precision=lax.Precision.HIGHEST, preferred_element_type=F32)


def _sigmoid(x):
    return 1.0 / (1.0 + jnp.exp(-x))


def _silu(x):
    return x * _sigmoid(x)


def _log_sigmoid(x):
    return jnp.minimum(x, 0.0) - jnp.log1p(jnp.exp(-jnp.abs(x)))


def _softplus(x):
    return jnp.maximum(x, 0.0) + jnp.log1p(jnp.exp(-jnp.abs(x)))


def _rms_mod(x, nw, sc, sh):
    y = x * lax.rsqrt(jnp.mean(x * x, axis=-1, keepdims=True) + EPS) * nw
    return y * (1.0 + sc) + sh


def _rms_mod_rows(x_ref, nw_ref, sc_ref, sh_ref, out_ref, rows=LANES):
    tm = x_ref.shape[0]
    step = rows if tm % rows == 0 else tm
    for r in range(0, tm, step):
        sl = slice(r, r + step)
        sc = sc_ref[...] if sc_ref.shape[0] == 1 else sc_ref[sl, :]
        sh = sh_ref[...] if sh_ref.shape[0] == 1 else sh_ref[sl, :]
        out_ref[sl, :] = _rms_mod(x_ref[sl, :], nw_ref[...], sc, sh).astype(out_ref.dtype)


def _merge(h, gate, onorm):
    hn = h * lax.rsqrt(jnp.mean(h * h, axis=-1, keepdims=True) + EPS)
    return (hn * onorm * gate).astype(BF16)


def _iota2(shape, dim):
    return lax.broadcasted_iota(jnp.int32, shape, dim)


def _row_to_col(r):
    n = r.shape[1]
    eye = _iota2((n, n), 0) == _iota2((n, n), 1)
    return jnp.sum(jnp.where(eye, r, 0.0), axis=1, keepdims=True)


def _interleave(units):
    units = list(units)
    while units:
        alive = []
        for u in units:
            try:
                next(u)
                alive.append(u)
            except StopIteration:
                pass
        units = alive


def _cumsum_pair(x_c, x_r, incl, incl_t):
    f_c = jnp.sum(jnp.where(incl, x_r, 0.0), axis=1, keepdims=True)
    f_r = jnp.sum(jnp.where(incl_t, x_c, 0.0), axis=0, keepdims=True)
    return f_c, f_r


def _cumsum_rows(x, tri_b):
    hi = x.astype(BF16)
    r1 = x - hi.astype(F32)
    mid = r1.astype(BF16)
    lo = (r1 - mid.astype(F32)).astype(BF16)
    return _dot(tri_b, hi) + _dot(tri_b, mid) + _dot(tri_b, lo)


def _ada_kernel(c_ref, w_ref, b_ref, o_ref):
    cs = _silu(c_ref[...]).astype(BF16)
    o_ref[...] = _dot(cs, w_ref[...].astype(BF16)) + b_ref[...]


def _ada(c_all, w_ada, b_ada):
    depth, d, n6 = w_ada.shape
    rows = c_all.shape[0]
    tn = 512
    return pl.pallas_call(
        _ada_kernel,
        grid=(depth, n6 // tn),
        in_specs=[
            pl.BlockSpec((rows, d), lambda l, j: (0, 0)),
            pl.BlockSpec((None, d, tn), lambda l, j: (l, 0, j)),
            pl.BlockSpec((None, 1, tn), lambda l, j: (l, 0, j)),
        ],
        out_specs=pl.BlockSpec((None, rows, tn), lambda l, j: (l, 0, j)),
        out_shape=jax.ShapeDtypeStruct((depth, rows, n6), F32),
        compiler_params=_cparams(("arbitrary", "arbitrary")),
    )(c_all, w_ada, b_ada.reshape(depth, 1, n6))


class _Tiling:
    def __init__(self, b, t, d, tile=TOKEN_TILE):
        self.b, self.t, self.d = b, t, d
        self.n = b * t
        self.per_batch = t % LANES == 0
        if self.per_batch:
            self.tm = next(m for m in (tile, 256, LANES) if m <= tile and t % m == 0)
        else:
            self.tm = self.n if self.n <= tile else tile
            assert self.n % self.tm == 0 and self.tm % SUBLANES == 0
        self.tiles = self.n // self.tm

    def mod_array(self, mod_l):
        if self.per_batch:
            return mod_l.reshape(self.b, 1, mod_l.shape[-1])
        return jnp.repeat(mod_l, self.t, axis=0)

    def mod_spec(self, k, grid_rank):
        d, tm, t = self.d, self.tm, self.t
        if self.per_batch:
            if grid_rank == 1:
                return pl.BlockSpec((None, 1, d), lambda i: ((i * tm) // t, 0, k))
            return pl.BlockSpec((None, 1, d), lambda i, j: ((i * tm) // t, 0, k))
        if grid_rank == 1:
            return pl.BlockSpec((tm, d), lambda i: (i, k))
        return pl.BlockSpec((tm, d), lambda i, j: (i, k))


def _in_proj_kernel(x_ref, nw_ref, sc_ref, sh_ref, w_ref, z_ref, zs_ref, h_scr, prev_scr, *, nh, tiles_per_piece):
    j = pl.program_id(1)
    n_tiles = pl.num_programs(1) - 1
    tn = z_ref.shape[1]
    width = tn + LANES
    groups = ((4, 0, 2 * nh), (12, 2 * nh, 2 * nh), (16, 4 * nh, GLA_RANK))

    def emit(head):
        piece = (j - 1) // tiles_per_piece
        amount = jnp.where(piece < 4, 0, width - jnp.where(piece < 12, 2 * nh, 4 * nh))
        zz = jnp.concatenate([prev_scr[...], head], axis=1)
        z_ref[...] = pltpu.roll(zz, amount, axis=1)[:, :tn]
        for hi, s, n_gate in groups:
            @pl.when(j == hi * tiles_per_piece)
            def _():
                lane = _iota2(head.shape, 1)
                zs_ref[...] = jnp.where((lane >= s) & (lane < s + n_gate), head, zs_ref[...])

    @pl.when(j == 0)
    def _():
        _rms_mod_rows(x_ref, nw_ref, sc_ref, sh_ref, h_scr)
        zs_ref[...] = jnp.zeros_like(zs_ref)
        prev_scr[...] = _dot(h_scr[...], w_ref[...])

    @pl.when((j > 0) & (j < n_tiles))
    def _():
        za = _dot(h_scr[...], w_ref[...])
        emit(za[:, :LANES])
        prev_scr[...] = za

    @pl.when(j == n_tiles)
    def _():
        emit(_dot(h_scr[...], w_ref[:, :LANES]))


def _in_proj(til, x, nw, mod_arr, w_in_b, layer):
    if not til.per_batch:
        til = _Tiling(til.b, til.t, til.d, tile=256)
    n, d, tm = til.n, til.d, til.tm
    dg = d // N_MIXERS
    nh = dg // HEAD_DIM
    nbig = 16 * dg
    tn = min(1024, dg)
    assert 4 * nh + GLA_RANK <= LANES and w_in_b.shape[2] == nbig + 4 * nh + GLA_RANK
    return pl.pallas_call(
        functools.partial(_in_proj_kernel, nh=nh, tiles_per_piece=dg // tn),
        grid=(til.tiles, nbig // tn + 1),
        in_specs=[
            pl.BlockSpec((tm, d), lambda i, j: (i, 0)),
            pl.BlockSpec((None, 1, d), lambda i, j: (layer, 0, 0)),
            til.mod_spec(1, 2),
            til.mod_spec(0, 2),
            pl.BlockSpec((None, d, tn), lambda i, j: (layer, 0, j)),
        ],
        out_specs=[
            pl.BlockSpec((tm, tn), lambda i, j: (i, jnp.maximum(j - 1, 0))),
            pl.BlockSpec((tm, LANES), lambda i, j: (i, 0)),
        ],
        out_shape=[jax.ShapeDtypeStruct((n, nbig), F32), jax.ShapeDtypeStruct((n, LANES), F32)],
        scratch_shapes=[pltpu.VMEM((tm, d), BF16), pltpu.VMEM((tm, tn), F32)],
        compiler_params=_cparams(("arbitrary", "arbitrary")),
    )(x, nw, mod_arr, mod_arr, w_in_b)


def _mlstm_kernel(*refs, bb, nh, c, n_valid, has_init, n_alias):
    q_ref, k_ref, v_ref, g_ref, zc_ref, zr_ref, br_ref, bc_ref, on_ref = refs[:9]
    n_in = 9
    if has_init:
        c0_ref, n0_ref, m0_ref = refs[9:12]
        n_in = 12
    o_ref, cs_ref, ns_ref, ms_ref = refs[n_in + n_alias:]

    @pl.when(pl.program_id(1) == 0)
    def _():
        if has_init:
            cs_ref[...] = c0_ref[...]
            ns_ref[...] = n0_ref[...]
            ms_ref[...] = m0_ref[...]
        else:
            cs_ref[...] = jnp.zeros_like(cs_ref)
            ns_ref[...] = jnp.zeros_like(ns_ref)
            ms_ref[...] = jnp.zeros_like(ms_ref)

    hd = HEAD_DIM
    scale = hd ** -0.5
    ti = _iota2((c, c), 0)
    si = _iota2((c, c), 1)
    incl = si <= ti
    incl_t = ti <= si
    valid_c = _iota2((c, 1), 0) < n_valid
    valid_r = _iota2((1, c), 1) < n_valid
    def unit(b, h, zc, zr):
        hs = slice(h * hd, (h + 1) * hd)
        q = q_ref[b, :, hs]
        k = k_ref[b, :, hs] * scale
        v = v_ref[b, :, hs]
        ig_c = zc[:, h:h + 1]
        lf_c = _log_sigmoid(zc[:, nh + h:nh + h + 1])
        ig_r = zr[h:h + 1, :]
        lf_r = _log_sigmoid(zr[nh + h:nh + h + 1, :])
        if n_valid < c:
            ig_c = jnp.where(valid_c, ig_c, NEG_BIG)
            lf_c = jnp.where(valid_c, lf_c, 0.0)
            ig_r = jnp.where(valid_r, ig_r, NEG_BIG)
            lf_r = jnp.where(valid_r, lf_r, 0.0)
        cm = cs_ref[b, h]
        nv = ns_ref[b, h:h + 1, :]
        m0 = ms_ref[b, :, h:h + 1]
        f_c, f_r = _cumsum_pair(lf_c, lf_r, incl, incl_t)
        yield
        raw = f_c - f_r + ig_r
        a = f_c + m0
        m_t = jnp.maximum(a, jnp.max(jnp.where(incl, raw, NEG_BIG), axis=1, keepdims=True))
        p = jnp.where(incl, jnp.exp(jnp.where(incl, raw - m_t, 0.0)), 0.0)
        qb = q.astype(BF16)
        kb = k.astype(BF16)
        vb = v.astype(BF16)
        yield
        s = _dot_nt(qb, kb) * p
        inter = jnp.exp(a - m_t)
        qc = _dot(qb, cm.astype(BF16))
        yield
        num = inter * qc + _dot(s.astype(BF16), vb)
        den = inter * jnp.sum(q * nv, axis=1, keepdims=True) + jnp.sum(s, axis=1, keepdims=True)
        m_end = m_t[c - 1:c, :]
        w_end = jnp.exp(f_c[c - 1:c, :] - f_c + ig_c - m_end)
        dec = jnp.exp(a[c - 1:c, :] - m_end)
        kw = w_end * k
        yield
        hh = num / jnp.maximum(jnp.abs(den), jnp.exp(-m_t))
        cs_ref[b, h] = dec * cm + _dot_tn(kw.astype(BF16), vb)
        ns_ref[b, h:h + 1, :] = dec * nv + jnp.sum(kw, axis=0, keepdims=True)
        ms_ref[b, :, h:h + 1] = m_end
        o_ref[b, :, hs] = _merge(hh, _sigmoid(g_ref[b, :, hs]), on_ref[:, hs])

    for b in range(bb):
        zc = zc_ref[b] + br_ref[...]
        zr = zr_ref[b] + bc_ref[...]
        _interleave([unit(b, h, zc, zr) for h in range(nh)])


def _gla_unit(q, k, v, lg, s_mat, tri_b, emit):
    c, hd = q.shape
    sb = min(SUB, c)
    nb = c // sb
    g = _cumsum_rows(lg, tri_b)
    yield
    o = _dot((q * jnp.exp(g)).astype(BF16), s_mat.astype(BF16))

    if nb > 1:
        qparts, kparts = [], []
        for j in range(nb - 1):
            r1 = (j + 1) * sb
            g_end = g[r1 - 1:r1, :]
            qj = q[r1:, :] * jnp.exp(g[r1:, :] - g_end)
            kj = k[j * sb:r1, :] * jnp.exp(g_end - g[j * sb:r1, :])
            qparts.append(jnp.concatenate([jnp.zeros((r1, hd), F32), qj], axis=0).astype(BF16))
            pieces = [kj]
            if j > 0:
                pieces.insert(0, jnp.zeros((j * sb, hd), F32))
            pieces.append(jnp.zeros((c - r1, hd), F32))
            kparts.append(jnp.concatenate(pieces, axis=0).astype(BF16))
        a_off = _dot_nt(jnp.concatenate(qparts, axis=1), jnp.concatenate(kparts, axis=1))
    else:
        a_off = jnp.zeros((c, c), F32)
    yield

    lane = _iota2((sb, c), 1)
    trow = _iota2((sb, 1), 0)
    strips = []
    for i in range(nb):
        r0 = i * sb
        qi, ki, gi = q[r0:r0 + sb, :], k[r0:r0 + sb, :], g[r0:r0 + sb, :]
        strip = a_off[r0:r0 + sb, :]
        for s in range(sb):
            msk = trow >= s
            w = jnp.where(msk, jnp.exp(jnp.where(msk, gi - gi[s:s + 1, :], 0.0)), 0.0)
            col = jnp.sum(qi * ki[s:s + 1, :] * w, axis=1, keepdims=True)
            strip = jnp.where(lane == r0 + s, col, strip)
        strips.append(strip)
        yield
    a = strips[0] if nb == 1 else jnp.concatenate(strips, axis=0)
    vb = v.astype(BF16)
    o = o + _dot(a.astype(BF16), vb)

    g_end = g[c - 1:c, :]
    kt = (k * jnp.exp(g_end - g)).astype(BF16)
    yield
    emit(o, _row_to_col(jnp.exp(g_end)) * s_mat + _dot_tn(kt, vb))


def _gla_kernel(*refs, bb, nh, c, n_valid, has_init, n_alias, kind, layer):
    if kind == "hgrn":
        q_ref, k_ref, v_ref, g_ref, par_ref, on_ref = refs[:6]
        rest = refs[6:]
    else:
        q_ref, k_ref, v_ref, g_ref, zc_ref, wg_ref, bg_ref, on_ref = refs[:8]
        rest = refs[8:]
    if has_init:
        s0_ref = rest[0]
        rest = rest[1:]
    o_ref, st_ref = rest[n_alias:]

    @pl.when(pl.program_id(1) == 0)
    def _():
        if has_init:
            st_ref[...] = s0_ref[...]
        else:
            st_ref[...] = jnp.zeros_like(st_ref)

    hd = HEAD_DIM
    tri_b = (_iota2((c, c), 1) <= _iota2((c, c), 0)).astype(BF16)
    valid_c = _iota2((c, 1), 0) < n_valid
    if kind == "hgrn":
        lbp = par_ref[...]
        ex = jnp.exp(lbp - jnp.max(lbp, axis=0, keepdims=True))
        lbs = ex / jnp.sum(ex, axis=0, keepdims=True)
        lb = jnp.zeros_like(lbs[0:1, :])
        for j in range(1, layer + 1):
            lb = lb + lbs[j:j + 1, :]
    def finish(b, h, hs):
        def emit(o, s_new):
            st_ref[b, h] = s_new
            o_ref[b, :, hs] = _merge(o, _silu(g_ref[b, :, hs]), on_ref[:, hs])
        return emit

    for b in range(bb):
        if kind == "gla":
            gate_in = _dot(zc_ref[b].astype(BF16), wg_ref[...].astype(BF16)) + bg_ref[...]
        units = []
        for h in range(nh):
            hs = slice(h * hd, (h + 1) * hd)
            if kind == "hgrn":
                fg = k_ref[b, :, hs]
                lbh = lb[:, hs]
                f = lbh + (1.0 - lbh) * _sigmoid(fg)
                lg = jnp.log(jnp.maximum(f, MIN_FORGET))
                k = (1.0 - lbh) * _sigmoid(-fg)
                q = _silu(q_ref[b, :, hs])
            else:
                lg = _log_sigmoid(gate_in[:, hs]) / GLA_TAU
                k = k_ref[b, :, hs]
                q = q_ref[b, :, hs] * (hd ** -0.5)
            v = v_ref[b, :, hs]
            if n_valid < c:
                lg = jnp.where(valid_c, lg, 0.0)
                k = jnp.where(valid_c, k, 0.0)
            units.append(_gla_unit(q, k, v, lg, st_ref[b, h], tri_b, finish(b, h, hs)))
        _interleave(units)


def _gdn_kernel(*refs, bb, nh, c, n_valid, has_init, n_alias):
    (q_ref, k_ref, v_ref, g_ref, zc_ref, zr_ref, br_ref, bc_ref, ar_ref, ac_ref, cw_ref, on_ref) = refs[:12]
    n_in = 12
    if has_init:
        s0_ref, cv0_ref = refs[12:14]
        n_in = 14
    o_ref, st_ref, cvo_ref, tail_ref = refs[n_in + n_alias:]
    t_id = pl.program_id(1)
    n_chunks = pl.num_programs(1)
    tail_rows = SUBLANES
    n_buf = CONV_W - 1

    @pl.when(t_id == 0)
    def _():
        tail_ref[...] = jnp.zeros_like(tail_ref)
        if has_init:
            st_ref[...] = s0_ref[...]
            for b in range(bb):
                for pc in range(3):
                    tail_ref[b, pc, tail_rows - n_buf:tail_rows, :] = cv0_ref[b, :, pc, :]
        else:
            st_ref[...] = jnp.zeros_like(st_ref)

    hd = HEAD_DIM
    scale = hd ** -0.5
    ti = _iota2((c, c), 0)
    si = _iota2((c, c), 1)
    incl = si <= ti
    incl_t = ti <= si
    strict = si < ti
    eye = (si == ti).astype(F32)
    valid_c = _iota2((c, 1), 0) < n_valid
    valid_r = _iota2((1, c), 1) < n_valid
    raw_refs = (q_ref, k_ref, v_ref)
    for b in range(bb):
        conv = []
        for pc in range(3):
            u = raw_refs[pc][b]
            ext = jnp.concatenate([tail_ref[b, pc], u], axis=0)
            acc = u * cw_ref[CONV_W - 1:CONV_W, pc, :]
            for j in range(1, CONV_W):
                shifted = pltpu.roll(ext, j, axis=0)[tail_rows:tail_rows + c, :]
                acc = acc + shifted * cw_ref[CONV_W - 1 - j:CONV_W - j, pc, :]
            conv.append(_silu(acc))
            tail_ref[b, pc] = u[c - tail_rows:c, :]

        @pl.when(t_id == n_chunks - 1)
        def _():
            for pc in range(3):
                cvo_ref[b, :, pc, :] = raw_refs[pc][b, n_valid - n_buf:n_valid, :]

        zc = zc_ref[b] + br_ref[...]
        zr = zr_ref[b] + bc_ref[...]
        units = []
        for h in range(nh):
            hs = slice(h * hd, (h + 1) * hd)
            q = conv[0][:, hs]
            k = conv[1][:, hs]
            v = conv[2][:, hs]
            q = q * lax.rsqrt(jnp.sum(q * q, axis=1, keepdims=True) + EPS) * scale
            k = k * lax.rsqrt(jnp.sum(k * k, axis=1, keepdims=True) + EPS)
            ca, cb = 2 * nh + h, 3 * nh + h
            lg_c = -jnp.exp(ar_ref[:, ca:ca + 1]) * _softplus(zc[:, ca:ca + 1])
            lg_r = -jnp.exp(ac_ref[ca:ca + 1, :]) * _softplus(zr[ca:ca + 1, :])
            beta = _sigmoid(zc[:, cb:cb + 1])
            if n_valid < c:
                lg_c = jnp.where(valid_c, lg_c, 0.0)
                lg_r = jnp.where(valid_r, lg_r, 0.0)
                beta = jnp.where(valid_c, beta, 0.0)
            s_mat = st_ref[b, h]

            g_c, g_r = _cumsum_pair(lg_c, lg_r, incl, incl_t)
            eg = jnp.exp(g_c)
            rel = jnp.where(incl, jnp.exp(jnp.where(incl, g_c - g_r, 0.0)), 0.0)
            qb = q.astype(BF16)
            kb = k.astype(BF16)
            kq_s = _dot(jnp.concatenate([kb, qb], axis=0), s_mat.astype(BF16))
            m = jnp.where(strict, beta * rel * _dot_nt(kb, kb), 0.0)
            rhs = beta * (v - eg * kq_s[:c, :])
            qk = (_dot_nt(qb, kb) * rel).astype(BF16)
            units.append(dict(hs=hs, k=k, s_mat=s_mat, g_c=g_c, eg=eg, m=m, rhs=rhs, qk=qk, qs=kq_s[c:, :]))

        tinv = [eye - un["m"] for un in units]
        pw = [_dot_f32(un["m"], un["m"]) for un in units]
        n_it = int(math.log2(c)) - 1
        for it in range(n_it):
            tinv = [t + _dot_f32(t, p) for t, p in zip(tinv, pw)]
            if it < n_it - 1:
                pw = [_dot_f32(p, p) for p in pw]

        for un, t in zip(units, tinv):
            ub = _dot_f32(t, un["rhs"]).astype(BF16)
            o = un["eg"] * un["qs"] + _dot(un["qk"], ub)
            g_end = un["g_c"][c - 1:c, :]
            kd = (jnp.exp(g_end - un["g_c"]) * un["k"]).astype(BF16)
            h = un["hs"].start // hd
            st_ref[b, h] = jnp.exp(g_end) * un["s_mat"] + _dot_tn(kd, ub)
            o_ref[b, :, un["hs"]] = _merge(o, _silu(g_ref[b, :, un["hs"]]), on_ref[:, un["hs"]])


def _mixers(b, t, dg, z_big, zs, states, prev, prm, layer, depth):
    nh = dg // HEAD_DIM
    hd = HEAD_DIM
    n = b * t
    tp = -(-t // SUBLANES) * SUBLANES
    c = min(CHUNK, tp)
    assert tp % c == 0 and (tp == t or tp == c) and t >= CONV_W - 1 and 4 * nh + GLA_RANK <= SMALL_ROWS
    nc = tp // c
    n_valid = c - (tp - t)
    has_init = states is not None
    bb = 2 if (has_init and b % 2 == 0) else 1

    z3 = z_big.reshape(b, t, 16 * dg)
    zs3 = zs.reshape(b, t, LANES)
    if tp != t:
        z3 = jnp.pad(z3, ((0, 0), (0, tp - t), (0, 0)))
        zs3 = jnp.pad(zs3, ((0, 0), (0, tp - t), (0, 0)))
    zr4 = jnp.swapaxes(zs3[:, :, :SMALL_ROWS].reshape(b, nc, c, SMALL_ROWS), 2, 3)

    grid = (b // bb, nc)
    piece = lambda p: pl.BlockSpec((bb, c, dg), lambda i, j: (i, j, p))
    zc_spec = pl.BlockSpec((bb, c, LANES), lambda i, j: (i, j, 0))
    zr_spec = pl.BlockSpec((bb, None, SMALL_ROWS, c), lambda i, j: (i, j, 0, 0))
    full2 = lambda a: pl.BlockSpec(a.shape, lambda i, j: (0,) * a.ndim)
    mat_spec = pl.BlockSpec((None, bb, nh, hd, hd), lambda i, j: (layer, i, 0, 0, 0))
    n_spec = pl.BlockSpec((None, bb, nh, hd), lambda i, j: (layer, i, 0, 0))
    m_spec = pl.BlockSpec((None, bb, 1, nh), lambda i, j: (layer, i, 0, 0))
    cv_spec = pl.BlockSpec((None, bb, CONV_W - 1, 3, dg), lambda i, j: (layer, i, 0, 0, 0))
    mat_shape = jax.ShapeDtypeStruct((depth, b, nh, hd, hd), F32)
    n_shape = jax.ShapeDtypeStruct((depth, b, nh, hd), F32)
    m_shape = jax.ShapeDtypeStruct((depth, b, 1, nh), F32)
    cv_shape = jax.ShapeDtypeStruct((depth, b, CONV_W - 1, 3, dg), F32)
    any_spec = pl.BlockSpec(memory_space=pl.ANY)
    o_spec = pl.BlockSpec((bb, c, dg), lambda i, j: (i, j, 0))
    o_shape = jax.ShapeDtypeStruct((b, tp, dg), BF16)
    cp = _cparams(("arbitrary", "arbitrary"))
    on = prm["out_norm"][layer].reshape(N_MIXERS, 1, dg)
    br, bc, ar, ac = prm["bias_row"][layer], prm["bias_col"][layer], prm["alog_row"][layer], prm["alog_col"][layer]

    def call(body, ins, specs, init, state_specs, state_shapes, prev_arrays, scratch=()):
        ins, specs = list(ins), list(specs)
        if has_init:
            ins += init
            specs += state_specs
        n_alias = 0 if prev_arrays is None else len(prev_arrays)
        aliases = {}
        if n_alias:
            aliases = {len(ins) + k: 1 + k for k in range(n_alias)}
            ins += list(prev_arrays)
            specs += [any_spec] * n_alias
        return pl.pallas_call(
            functools.partial(body, bb=bb, nh=nh, c=c, n_valid=n_valid, has_init=has_init, n_alias=n_alias),
            grid=grid, in_specs=specs, out_specs=[o_spec] + list(state_specs),
            out_shape=[o_shape] + list(state_shapes), scratch_shapes=list(scratch),
            input_output_aliases=aliases, compiler_params=cp,
        )(*ins)

    st = states
    pv = prev
    o_a, m_c, m_n, m_m = call(
        _mlstm_kernel, [z3, z3, z3, z3, zs3, zr4, br, bc, on[0]],
        [piece(0), piece(1), piece(2), piece(3), zc_spec, zr_spec, full2(br), full2(bc), full2(on[0])],
        None if st is None else [st[0], st[1], st[2].reshape(depth, b, 1, nh)],
        [mat_spec, n_spec, m_spec], [mat_shape, n_shape, m_shape],
        None if pv is None else [pv[0], pv[1], pv[2]])

    lbp = prm["hgrn_lb"]
    o_b, s_hgrn = call(
        functools.partial(_gla_kernel, kind="hgrn", layer=layer), [z3, z3, z3, z3, lbp, on[1]],
        [piece(4), piece(5), piece(6), piece(7), full2(lbp), full2(on[1])],
        None if st is None else [st[3]], [mat_spec], [mat_shape], None if pv is None else [pv[3]])

    cw = prm["gdn_conv_w"][layer].reshape(CONV_W, 3, dg)
    o_c, s_gdn, s_conv = call(
        _gdn_kernel, [z3, z3, z3, z3, zs3, zr4, br, bc, ar, ac, cw, on[2]],
        [piece(8), piece(9), piece(10), piece(11), zc_spec, zr_spec, full2(br), full2(bc), full2(ar),
         full2(ac), full2(cw), full2(on[2])],
        None if st is None else [st[4], st[5].reshape(depth, b, CONV_W - 1, 3, dg)],
        [mat_spec, cv_spec], [mat_shape, cv_shape], None if pv is None else [pv[4], pv[5]],
        scratch=[pltpu.VMEM((bb, 3, SUBLANES, dg), F32)])

    wg, bg = prm["gla_w_pad"][layer], prm["gla_b_gate"][layer].reshape(1, dg)
    o_d, s_gla = call(
        functools.partial(_gla_kernel, kind="gla", layer=layer), [z3, z3, z3, z3, zs3, wg, bg, on[3]],
        [piece(12), piece(13), piece(14), piece(15), zc_spec, full2(wg), full2(bg), full2(on[3])],
        None if st is None else [st[6]], [mat_spec], [mat_shape], None if pv is None else [pv[6]])

    outs = [o[:, :t, :].reshape(n, dg) for o in (o_a, o_b, o_c, o_d)]
    return outs, (m_c, m_n, m_m, s_hgrn, s_gdn, s_conv, s_gla)


def _out_proj_kernel(oa_ref, ob_ref, oc_ref, od_ref, w_ref, x_ref, g_ref, y_ref):
    acc = _dot(oa_ref[...], w_ref[0])
    acc = acc + _dot(ob_ref[...], w_ref[1])
    acc = acc + _dot(oc_ref[...], w_ref[2])
    acc = acc + _dot(od_ref[...], w_ref[3])
    y_ref[...] = x_ref[...] + g_ref[...] * acc


def _out_proj(til, outs, w_out4, x, mod_arr, layer):
    n, d, tm = til.n, til.d, til.tm
    dg = d // N_MIXERS
    tn = min(1024, d)
    o_spec = pl.BlockSpec((tm, dg), lambda i, j: (i, 0))
    if til.per_batch:
        g_spec = pl.BlockSpec((None, 1, tn), lambda i, j: ((i * tm) // til.t, 0, (2 * d) // tn + j))
    else:
        g_spec = pl.BlockSpec((tm, tn), lambda i, j: (i, (2 * d) // tn + j))
    return pl.pallas_call(
        _out_proj_kernel,
        grid=(til.tiles, d // tn),
        in_specs=[o_spec, o_spec, o_spec, o_spec,
                  pl.BlockSpec((None, N_MIXERS, dg, tn), lambda i, j: (layer, 0, 0, j)),
                  pl.BlockSpec((tm, tn), lambda i, j: (i, j)),
                  g_spec],
        out_specs=pl.BlockSpec((tm, tn), lambda i, j: (i, j)),
        out_shape=jax.ShapeDtypeStruct((n, d), F32),
        compiler_params=_cparams(("arbitrary", "arbitrary")),
    )(*outs, w_out4, x, mod_arr)


def _top_desc(s, count):
    rows = float(s.shape[0])
    ri = _iota2(s.shape, 0).astype(F32)
    vals = []
    for r in range(count):
        mx = jnp.max(s, axis=0, keepdims=True)
        vals.append(mx)
        if r < count - 1:
            first = jnp.min(jnp.where(s == mx, ri, rows), axis=0, keepdims=True)
            s = jnp.where(ri == first, -jnp.inf, s)
    return vals


def _cand_pairs():
    return [(a, b) for a in range(PEER_TOPK) for b in range(PEER_TOPK) if (a + 1) * (b + 1) <= PEER_TOPK]


def _route_kernel(x_ref, nw_ref, sc_ref, sh_ref, wq_ref, key_ref,
                  h2t_ref, s1_ref, s2_ref, e1_ref, e2_ref, tau_ref, cand_ref, h2_scr):
    @pl.when(pl.program_id(1) == 0)
    def _():
        h2 = _rms_mod(x_ref[...], nw_ref[...], sc_ref[...], sh_ref[...])
        h2_scr[...] = h2.astype(BF16)
        h2t_ref[...] = h2.T.astype(BF16)

    half = PEER_KEYS
    q = _dot(h2_scr[...], wq_ref[...])
    s1 = _dot_nt(key_ref[0].astype(BF16), q[:, :half].astype(BF16))
    s2 = _dot_nt(key_ref[1].astype(BF16), q[:, half:].astype(BF16))
    v1 = _top_desc(s1, PEER_TOPK)
    v2 = _top_desc(s2, PEER_TOPK)
    pairs = _cand_pairs()
    cand_ref[...] = jnp.full(cand_ref.shape, -jnp.inf, F32)
    for r, (a, b) in enumerate(pairs):
        cand_ref[r:r + 1, :] = v1[a] + v2[b]
    best = _top_desc(cand_ref[...], PEER_TOPK)
    zsum = jnp.zeros_like(best[0])
    for r in range(PEER_TOPK):
        zsum = zsum + jnp.exp(best[r] - best[0])
    s1_ref[...] = s1
    s2_ref[...] = s2
    e1_ref[...] = jnp.exp(s1 - v1[0]) / zsum
    e2_ref[...] = jnp.exp(s2 - v2[0])
    tau_ref[...] = best[PEER_TOPK - 1]


def _route(til, x1, nw, mod_arr, w_q, sub_keys, layer):
    n, d, tm = til.n, til.d, til.tm
    qd = w_q.shape[2] // PEER_HEADS
    n_cand = -(-len(_cand_pairs()) // SUBLANES) * SUBLANES
    tok = pl.BlockSpec((None, PEER_KEYS, tm), lambda i, h: (h, 0, i))
    tok_shape = jax.ShapeDtypeStruct((PEER_HEADS, PEER_KEYS, n), F32)
    return pl.pallas_call(
        _route_kernel,
        grid=(til.tiles, PEER_HEADS),
        in_specs=[
            pl.BlockSpec((tm, d), lambda i, h: (i, 0)),
            pl.BlockSpec((None, 1, d), lambda i, h: (layer, 0, 0)),
            til.mod_spec(4, 2),
            til.mod_spec(3, 2),
            pl.BlockSpec((None, d, qd), lambda i, h: (layer, 0, h)),
            pl.BlockSpec((None, None, 2, PEER_KEYS, qd // 2), lambda i, h: (layer, h, 0, 0, 0)),
        ],
        out_specs=[
            pl.BlockSpec((d, tm), lambda i, h: (0, i)),
            tok, tok, tok, tok,
            pl.BlockSpec((None, 1, tm), lambda i, h: (h, 0, i)),
        ],
        out_shape=[jax.ShapeDtypeStruct((d, n), BF16), tok_shape, tok_shape, tok_shape, tok_shape,
                   jax.ShapeDtypeStruct((PEER_HEADS, 1, n), F32)],
        scratch_shapes=[pltpu.VMEM((n_cand, tm), F32), pltpu.VMEM((tm, d), BF16)],
        compiler_params=_cparams(("arbitrary", "arbitrary")),
    )(x1, nw, mod_arr, mod_arr, w_q, sub_keys)


def _peer_kernel(h2t_ref, u_ref, vt_ref, s1_ref, s2_ref, e1_ref, e2_ref, tau_ref, o_ref, act_scr, p_scr, *, te, n_et):
    e = pl.program_id(1)

    @pl.when(e == 0)
    def _():
        o_ref[...] = jnp.zeros_like(o_ref)
        act_scr[...] = jnp.zeros_like(act_scr)
        p_scr[...] = jnp.zeros_like(p_scr)

    o_ref[...] += _dot(vt_ref[...], p_scr[...])

    groups = te // PEER_KEYS
    tile = jnp.clip(e - 1, 0, n_et - 1)
    for ii in range(groups):
        rows = slice(ii * PEER_KEYS, (ii + 1) * PEER_KEYS)
        act = act_scr[rows, :]
        gel = 0.5 * act * (1.0 + lax.erf(act * (2.0 ** -0.5)))
        row = tile * groups + ii
        acc = jnp.zeros(act.shape, F32)
        for h in range(PEER_HEADS):
            sm = s1_ref[h, pl.ds(row, 1), :] + s2_ref[h]
            sel = jnp.where(sm >= tau_ref[h], e2_ref[h], 0.0)
            acc = acc + sel * e1_ref[h, pl.ds(row, 1), :]
        p_scr[rows, :] = (acc * gel).astype(BF16)

    act_scr[...] = _dot(u_ref[...], h2t_ref[...])


def _peer(til, h2, routing, u_tab, vt_tab, layer):
    n, d, tm = til.n, til.d, til.tm
    ne = u_tab.shape[1]
    te = EXPERT_TILE
    n_et = ne // te
    once = pl.Buffered(1)
    tok = pl.BlockSpec((PEER_HEADS, PEER_KEYS, tm), lambda i, e: (0, 0, i), pipeline_mode=once)
    return pl.pallas_call(
        functools.partial(_peer_kernel, te=te, n_et=n_et),
        grid=(til.tiles, n_et + 2),
        in_specs=[
            pl.BlockSpec((d, tm), lambda i, e: (0, i), pipeline_mode=once),
            pl.BlockSpec((None, te, d), lambda i, e: (layer, jnp.minimum(e, n_et - 1), 0)),
            pl.BlockSpec((None, d, te), lambda i, e: (layer, 0, jnp.clip(e - 2, 0, n_et - 1))),
            tok, tok, tok, tok,
            pl.BlockSpec((PEER_HEADS, 1, tm), lambda i, e: (0, 0, i), pipeline_mode=once),
        ],
        out_specs=pl.BlockSpec((d, tm), lambda i, e: (0, i), pipeline_mode=once),
        out_shape=jax.ShapeDtypeStruct((d, n), F32),
        scratch_shapes=[pltpu.VMEM((te, tm), F32), pltpu.VMEM((te, tm), BF16)],
        compiler_params=_cparams(("arbitrary", "arbitrary")),
    )(h2, u_tab, vt_tab, *routing)


def _residual_kernel(x_ref, pt_ref, g_ref, o_ref):
    o_ref[...] = x_ref[...] + g_ref[...] * pt_ref[...].T


def _residual_norm_kernel(x_ref, pt_ref, g_ref, nw_ref, o_ref):
    x = x_ref[...] + g_ref[...] * pt_ref[...].T
    o_ref[...] = x * lax.rsqrt(jnp.mean(x * x, axis=-1, keepdims=True) + EPS) * nw_ref[...]


def _residual(til, x1, p, mod_arr, final_norm):
    til = _Tiling(til.b, til.t, til.d, tile=256)
    n, d, tm = til.n, til.d, til.tm
    row = pl.BlockSpec((tm, d), lambda i: (i, 0))
    ins = [x1, p, mod_arr]
    specs = [row, pl.BlockSpec((d, tm), lambda i: (0, i)), til.mod_spec(5, 1)]
    body = _residual_kernel
    if final_norm is not None:
        ins.append(final_norm.reshape(1, d))
        specs.append(pl.BlockSpec((1, d), lambda i: (0, 0)))
        body = _residual_norm_kernel
    return pl.pallas_call(
        body, grid=(til.tiles,), in_specs=specs, out_specs=row,
        out_shape=jax.ShapeDtypeStruct((n, d), F32),
        compiler_params=_cparams(("arbitrary",)),
    )(*ins)


def _trunk(x, mod, states, prm):
    b, t, d = x.shape
    depth = mod.shape[0]
    dg = d // N_MIXERS
    til = _Tiling(b, t, d)
    xf = x.reshape(b * t, d)
    new = None
    for l in range(depth):
        mod_arr = til.mod_array(mod[l])
        z_big, zs = _in_proj(til, xf, prm["norm_mix"], mod_arr, prm["w_in_b"], l)
        outs, new = _mixers(b, t, dg, z_big, zs, states, new, prm, l, depth)
        x1 = _out_proj(til, outs, prm["w_out4"], xf, mod_arr, l)
        h2, *routing = _route(til, x1, prm["norm_ffn"], mod_arr, prm["w_q"], prm["peer_sub_keys"], l)
        p = _peer(til, h2, routing, prm["peer_u"], prm["peer_vt"], l)
        xf = _residual(til, x1, p, mod_arr, prm["final_norm"] if l == depth - 1 else None)
    m_c, m_n, m_m, s_hgrn, s_gdn, s_conv, s_gla = new
    nh = dg // HEAD_DIM
    new_states = [m_c, m_n, m_m.reshape(depth, b, nh), s_hgrn, s_gdn,
                  s_conv.reshape(depth, b, CONV_W - 1, 3 * dg), s_gla]
    return xf.reshape(b, t, d), new_states


def _prepare(mlstm_b_i, mlstm_b_f, gdn_a_log, gdn_dt_bias, gla_w_gate, d):
    depth = mlstm_b_i.shape[0]
    dg = d // N_MIXERS
    nh = dg // HEAD_DIM
    n_small = 4 * nh + GLA_RANK
    zeros = lambda k: jnp.zeros((depth, k), F32)
    bias = jnp.concatenate([mlstm_b_i, mlstm_b_f, gdn_dt_bias, zeros(LANES - 3 * nh)], axis=1)
    alog = jnp.concatenate([zeros(2 * nh), gdn_a_log, zeros(LANES - 3 * nh)], axis=1)
    gla_w_pad = jnp.concatenate(
        [jnp.zeros((depth, 4 * nh, dg), F32), gla_w_gate, jnp.zeros((depth, LANES - n_small, dg), F32)], axis=1)
    return dict(
        bias_row=bias.reshape(depth, 1, LANES), bias_col=bias[:, :SMALL_ROWS].reshape(depth, SMALL_ROWS, 1),
        alog_row=alog.reshape(depth, 1, LANES), alog_col=alog[:, :SMALL_ROWS].reshape(depth, SMALL_ROWS, 1),
        gla_w_pad=gla_w_pad)


def kernel(x_prompt, x_sample, c_prompt, c_sample, state_mlstm_C, state_mlstm_n, state_mlstm_m, state_hgrn, state_gdn, state_gdn_conv, state_gla, w_ada, b_ada, norm_mix, norm_ffn, w_in, mlstm_b_i, mlstm_b_f, hgrn_lb, gdn_conv_w, gdn_a_log, gdn_dt_bias, gla_w_gate, gla_b_gate, out_norm, w_out, peer_w_q, peer_sub_keys, peer_u, peer_v, final_norm):
    depth, d = norm_mix.shape
    dg = d // N_MIXERS
    assert d % (N_MIXERS * HEAD_DIM) == 0
    prm = _prepare(mlstm_b_i, mlstm_b_f, gdn_a_log, gdn_dt_bias, gla_w_gate, d)
    prm.update(
        norm_mix=norm_mix.reshape(depth, 1, d), norm_ffn=norm_ffn.reshape(depth, 1, d), hgrn_lb=hgrn_lb,
        gdn_conv_w=gdn_conv_w, gla_b_gate=gla_b_gate, out_norm=out_norm, final_norm=final_norm,
        peer_sub_keys=peer_sub_keys, w_in_b=w_in.astype(BF16),
        w_out4=w_out.astype(BF16).reshape(depth, N_MIXERS, dg, d),
        w_q=peer_w_q.astype(BF16), peer_u=peer_u.astype(BF16),
        peer_vt=jnp.swapaxes(peer_v, 1, 2).astype(BF16))

    bp, bs = c_prompt.shape[0], c_sample.shape[0]
    rows = -(-(bp + bs) // SUBLANES) * SUBLANES
    c_all = jnp.concatenate([c_prompt, c_sample, jnp.zeros((rows - bp - bs, d), F32)], axis=0)
    mod = _ada(c_all, w_ada, b_ada)

    y_prompt, p_states = _trunk(x_prompt, mod[:, :bp], None, prm)
    past = (state_mlstm_C, state_mlstm_n, state_mlstm_m, state_hgrn, state_gdn, state_gdn_conv, state_gla)
    y_sample, s_states = _trunk(x_sample, mod[:, bp:bp + bs], past, prm)
    return (y_prompt, y_sample, *p_states, *s_states)
```

```python
import functools
import math

import jax
import jax.numpy as jnp
from jax import lax
from jax.experimental import pallas as pl
from jax.experimental.pallas import tpu as pltpu

F32 = jnp.float32
BF16 = jnp.bfloat16

HEAD_DIM = 256
N_MIXERS = 4
CONV_W = 4
GLA_RANK = 16
GLA_TAU = 16.0
PEER_HEADS = 8
PEER_KEYS = 128
PEER_TOPK = 16
N_MOD = 6
EPS = 1e-6
NEG_BIG = -1e30
MIN_FORGET = 1e-6

LANES = 128
SUBLANES = 8
SMALL_ROWS = 32
CHUNK = 128
SUB = 8
INV_BLOCK = 16
TOKEN_TILE = 512
EXPERT_TILE = 512
VMEM_LIMIT = 56 * 1024 * 1024


def _cparams(sem):
    return pltpu.CompilerParams(dimension_semantics=sem, vmem_limit_bytes=VMEM_LIMIT)


def _dot(a, b):
    return jnp.dot(a, b, preferred_element_type=F32)


def _dot_nt(a, b):
    return lax.dot_general(a, b, (((1,), (1,)), ((), ())), preferred_element_type=F32)


def _dot_tn(a, b):
    return lax.dot_general(a, b, (((0,), (0,)), ((), ())), preferred_element_type=F32)


def _split2(a):
    hi = a.astype(BF16)
    return hi, (a - hi.astype(F32)).astype(BF16)


def _dot_f32(a, b):
    a_hi, a_lo = _split2(a)
    b_hi, b_lo = _split2(b)
    return _dot(a_hi, b_hi) + (_dot(a_hi, b_lo) + _dot(a_lo, b_hi))


def _sigmoid(x):
    return 1.0 / (1.0 + jnp.exp(-x))


def _silu(x):
    return x * _sigmoid(x)


def _log_sigmoid(x):
    return jnp.minimum(x, 0.0) - jnp.log1p(jnp.exp(-jnp.abs(x)))


def _softplus(x):
    return jnp.maximum(x, 0.0) + jnp.log1p(jnp.exp(-jnp.abs(x)))


def _rms_mod(x, nw, sc, sh):
    y = x * lax.rsqrt(jnp.mean(x * x, axis=-1, keepdims=True) + EPS) * nw
    return y * (1.0 + sc) + sh


def _rms_mod_rows(x_ref, nw_ref, sc_ref, sh_ref, out_ref, rows=LANES):
    tm = x_ref.shape[0]
    step = rows if tm % rows == 0 else tm
    for r in range(0, tm, step):
        sl = slice(r, r + step)
        sc = sc_ref[...] if sc_ref.shape[0] == 1 else sc_ref[sl, :]
        sh = sh_ref[...] if sh_ref.shape[0] == 1 else sh_ref[sl, :]
        out_ref[sl, :] = _rms_mod(x_ref[sl, :], nw_ref[...], sc, sh).astype(out_ref.dtype)


def _merge(h, gate, onorm):
    hn = h * lax.rsqrt(jnp.mean(h * h, axis=-1, keepdims=True) + EPS)
    return (hn * onorm * gate).astype(BF16)


def _iota2(shape, dim):
    return lax.broadcasted_iota(jnp.int32, shape, dim)


def _row_to_col(r):
    n = r.shape[1]
    eye = _iota2((n, n), 0) == _iota2((n, n), 1)
    return jnp.sum(jnp.where(eye, r, 0.0), axis=1, keepdims=True)


def _interleave(units):
    units = list(units)
    while units:
        alive = []
        for u in units:
            try:
                next(u)
                alive.append(u)
            except StopIteration:
                pass
        units = alive


def _cumsum_pair(x_c, x_r, incl, incl_t):
    f_c = jnp.sum(jnp.where(incl, x_r, 0.0), axis=1, keepdims=True)
    f_r = jnp.sum(jnp.where(incl_t, x_c, 0.0), axis=0, keepdims=True)
    return f_c, f_r


def _cumsum_rows(x, tri_b):
    hi = x.astype(BF16)
    r1 = x - hi.astype(F32)
    mid = r1.astype(BF16)
    lo = (r1 - mid.astype(F32)).astype(BF16)
    return _dot(tri_b, hi) + _dot(tri_b, mid) + _dot(tri_b, lo)


def _ada_kernel(c_ref, w_ref, b_ref, o_ref):
    cs = _silu(c_ref[...]).astype(BF16)
    o_ref[...] = _dot(cs, w_ref[...].astype(BF16)) + b_ref[...]


def _ada(c_all, w_ada, b_ada):
    depth, d, n6 = w_ada.shape
    rows = c_all.shape[0]
    tn = 512
    return pl.pallas_call(
        _ada_kernel,
        grid=(depth, n6 // tn),
        in_specs=[
            pl.BlockSpec((rows, d), lambda l, j: (0, 0)),
            pl.BlockSpec((None, d, tn), lambda l, j: (l, 0, j)),
            pl.BlockSpec((None, 1, tn), lambda l, j: (l, 0, j)),
        ],
        out_specs=pl.BlockSpec((None, rows, tn), lambda l, j: (l, 0, j)),
        out_shape=jax.ShapeDtypeStruct((depth, rows, n6), F32),
        compiler_params=_cparams(("arbitrary", "arbitrary")),
    )(c_all, w_ada, b_ada.reshape(depth, 1, n6))


class _Tiling:
    def __init__(self, b, t, d, tile=TOKEN_TILE):
        self.b, self.t, self.d = b, t, d
        self.n = b * t
        self.per_batch = t % LANES == 0
        if self.per_batch:
            self.tm = next(m for m in (tile, 256, LANES) if m <= tile and t % m == 0)
        else:
            self.tm = self.n if self.n <= tile else tile
            assert self.n % self.tm == 0 and self.tm % SUBLANES == 0
        self.tiles = self.n // self.tm

    def mod_array(self, mod_l):
        if self.per_batch:
            return mod_l.reshape(self.b, 1, mod_l.shape[-1])
        return jnp.repeat(mod_l, self.t, axis=0)

    def mod_spec(self, k, grid_rank):
        d, tm, t = self.d, self.tm, self.t
        if self.per_batch:
            if grid_rank == 1:
                return pl.BlockSpec((None, 1, d), lambda i: ((i * tm) // t, 0, k))
            return pl.BlockSpec((None, 1, d), lambda i, j: ((i * tm) // t, 0, k))
        if grid_rank == 1:
            return pl.BlockSpec((tm, d), lambda i: (i, k))
        return pl.BlockSpec((tm, d), lambda i, j: (i, k))


def _in_proj_kernel(x_ref, nw_ref, sc_ref, sh_ref, w_ref, z_ref, zs_ref, h_scr, prev_scr, *, nh, tiles_per_piece):
    j = pl.program_id(1)
    n_tiles = pl.num_programs(1) - 1
    tn = z_ref.shape[1]
    width = tn + LANES
    groups = ((4, 0, 2 * nh), (12, 2 * nh, 2 * nh), (16, 4 * nh, GLA_RANK))

    def emit(head):
        piece = (j - 1) // tiles_per_piece
        amount = jnp.where(piece < 4, 0, width - jnp.where(piece < 12, 2 * nh, 4 * nh))
        zz = jnp.concatenate([prev_scr[...], head], axis=1)
        z_ref[...] = pltpu.roll(zz, amount, axis=1)[:, :tn]
        for hi, s, n_gate in groups:
            @pl.when(j == hi * tiles_per_piece)
            def _():
                lane = _iota2(head.shape, 1)
                zs_ref[...] = jnp.where((lane >= s) & (lane < s + n_gate), head, zs_ref[...])

    @pl.when(j == 0)
    def _():
        _rms_mod_rows(x_ref, nw_ref, sc_ref, sh_ref, h_scr)
        zs_ref[...] = jnp.zeros_like(zs_ref)
        prev_scr[...] = _dot(h_scr[...], w_ref[...])

    @pl.when((j > 0) & (j < n_tiles))
    def _():
        za = _dot(h_scr[...], w_ref[...])
        emit(za[:, :LANES])
        prev_scr[...] = za

    @pl.when(j == n_tiles)
    def _():
        emit(_dot(h_scr[...], w_ref[:, :LANES]))


def _in_proj(til, x, nw, mod_arr, w_in_b, layer):
    if not til.per_batch:
        til = _Tiling(til.b, til.t, til.d, tile=256)
    n, d, tm = til.n, til.d, til.tm
    dg = d // N_MIXERS
    nh = dg // HEAD_DIM
    nbig = 16 * dg
    tn = min(1024, dg)
    assert 4 * nh + GLA_RANK <= LANES and w_in_b.shape[2] == nbig + 4 * nh + GLA_RANK
    return pl.pallas_call(
        functools.partial(_in_proj_kernel, nh=nh, tiles_per_piece=dg // tn),
        grid=(til.tiles, nbig // tn + 1),
        in_specs=[
            pl.BlockSpec((tm, d), lambda i, j: (i, 0)),
            pl.BlockSpec((None, 1, d), lambda i, j: (layer, 0, 0)),
            til.mod_spec(1, 2),
            til.mod_spec(0, 2),
            pl.BlockSpec((None, d, tn), lambda i, j: (layer, 0, j)),
        ],
        out_specs=[
            pl.BlockSpec((tm, tn), lambda i, j: (i, jnp.maximum(j - 1, 0))),
            pl.BlockSpec((tm, LANES), lambda i, j: (i, 0)),
        ],
        out_shape=[jax.ShapeDtypeStruct((n, nbig), F32), jax.ShapeDtypeStruct((n, LANES), F32)],
        scratch_shapes=[pltpu.VMEM((tm, d), BF16), pltpu.VMEM((tm, tn), F32)],
        compiler_params=_cparams(("arbitrary", "arbitrary")),
    )(x, nw, mod_arr, mod_arr, w_in_b)


def _mlstm_kernel(*refs, bb, nh, c, n_valid, has_init, n_alias):
    q_ref, k_ref, v_ref, g_ref, zc_ref, zr_ref, br_ref, bc_ref, on_ref = refs[:9]
    n_in = 9
    if has_init:
        c0_ref, n0_ref, m0_ref = refs[9:12]
        n_in = 12
    o_ref, cs_ref, ns_ref, ms_ref = refs[n_in + n_alias:]

    @pl.when(pl.program_id(1) == 0)
    def _():
        if has_init:
            cs_ref[...] = c0_ref[...]
            ns_ref[...] = n0_ref[...]
            ms_ref[...] = m0_ref[...]
        else:
            cs_ref[...] = jnp.zeros_like(cs_ref)
            ns_ref[...] = jnp.zeros_like(ns_ref)
            ms_ref[...] = jnp.zeros_like(ms_ref)

    hd = HEAD_DIM
    scale = hd ** -0.5
    ti = _iota2((c, c), 0)
    si = _iota2((c, c), 1)
    incl = si <= ti
    incl_t = ti <= si
    valid_c = _iota2((c, 1), 0) < n_valid
    valid_r = _iota2((1, c), 1) < n_valid
    def unit(b, h, zc, zr):
        hs = slice(h * hd, (h + 1) * hd)
        q = q_ref[b, :, hs]
        k = k_ref[b, :, hs] * scale
        v = v_ref[b, :, hs]
        ig_c = zc[:, h:h + 1]
        lf_c = _log_sigmoid(zc[:, nh + h:nh + h + 1])
        ig_r = zr[h:h + 1, :]
        lf_r = _log_sigmoid(zr[nh + h:nh + h + 1, :])
        if n_valid < c:
            ig_c = jnp.where(valid_c, ig_c, NEG_BIG)
            lf_c = jnp.where(valid_c, lf_c, 0.0)
            ig_r = jnp.where(valid_r, ig_r, NEG_BIG)
            lf_r = jnp.where(valid_r, lf_r, 0.0)
        cm = cs_ref[b, h]
        nv = ns_ref[b, h:h + 1, :]
        m0 = ms_ref[b, :, h:h + 1]
        f_c, f_r = _cumsum_pair(lf_c, lf_r, incl, incl_t)
        yield
        raw = f_c - f_r + ig_r
        a = f_c + m0
        m_t = jnp.maximum(a, jnp.max(jnp.where(incl, raw, NEG_BIG), axis=1, keepdims=True))
        p = jnp.where(incl, jnp.exp(jnp.where(incl, raw - m_t, 0.0)), 0.0)
        qb = q.astype(BF16)
        kb = k.astype(BF16)
        vb = v.astype(BF16)
        yield
        s = _dot_nt(qb, kb) * p
        inter = jnp.exp(a - m_t)
        qc = _dot(qb, cm.astype(BF16))
        yield
        num = inter * qc + _dot(s.astype(BF16), vb)
        den = inter * jnp.sum(q * nv, axis=1, keepdims=True) + jnp.sum(s, axis=1, keepdims=True)
        m_end = m_t[c - 1:c, :]
        w_end = jnp.exp(f_c[c - 1:c, :] - f_c + ig_c - m_end)
        dec = jnp.exp(a[c - 1:c, :] - m_end)
        kw = w_end * k
        yield
        hh = num / jnp.maximum(jnp.abs(den), jnp.exp(-m_t))
        cs_ref[b, h] = dec * cm + _dot_tn(kw.astype(BF16), vb)
        ns_ref[b, h:h + 1, :] = dec * nv + jnp.sum(kw, axis=0, keepdims=True)
        ms_ref[b, :, h:h + 1] = m_end
        o_ref[b, :, hs] = _merge(hh, _sigmoid(g_ref[b, :, hs]), on_ref[:, hs])

    for b in range(bb):
        zc = zc_ref[b] + br_ref[...]
        zr = zr_ref[b] + bc_ref[...]
        _interleave([unit(b, h, zc, zr) for h in range(nh)])


def _gla_unit(q, k, v, lg, s_mat, tri_b, emit):
    c, hd = q.shape
    sb = min(SUB, c)
    nb = c // sb
    g = _cumsum_rows(lg, tri_b)
    yield
    o = _dot((q * jnp.exp(g)).astype(BF16), s_mat.astype(BF16))

    if nb > 1:
        qparts, kparts = [], []
        for j in range(nb - 1):
            r1 = (j + 1) * sb
            g_end = g[r1 - 1:r1, :]
            qj = q[r1:, :] * jnp.exp(g[r1:, :] - g_end)
            kj = k[j * sb:r1, :] * jnp.exp(g_end - g[j * sb:r1, :])
            qparts.append(jnp.concatenate([jnp.zeros((r1, hd), F32), qj], axis=0).astype(BF16))
            pieces = [kj]
            if j > 0:
                pieces.insert(0, jnp.zeros((j * sb, hd), F32))
            pieces.append(jnp.zeros((c - r1, hd), F32))
            kparts.append(jnp.concatenate(pieces, axis=0).astype(BF16))
        a_off = _dot_nt(jnp.concatenate(qparts, axis=1), jnp.concatenate(kparts, axis=1))
    else:
        a_off = jnp.zeros((c, c), F32)
    yield

    lane = _iota2((sb, c), 1)
    trow = _iota2((sb, 1), 0)
    strips = []
    for i in range(nb):
        r0 = i * sb
        qi, ki, gi = q[r0:r0 + sb, :], k[r0:r0 + sb, :], g[r0:r0 + sb, :]
        strip = a_off[r0:r0 + sb, :]
        for s in range(sb):
            msk = trow >= s
            w = jnp.where(msk, jnp.exp(jnp.where(msk, gi - gi[s:s + 1, :], 0.0)), 0.0)
            col = jnp.sum(qi * ki[s:s + 1, :] * w, axis=1, keepdims=True)
            strip = jnp.where(lane == r0 + s, col, strip)
        strips.append(strip)
        yield
    a = strips[0] if nb == 1 else jnp.concatenate(strips, axis=0)
    vb = v.astype(BF16)
    o = o + _dot(a.astype(BF16), vb)

    g_end = g[c - 1:c, :]
    kt = (k * jnp.exp(g_end - g)).astype(BF16)
    yield
    emit(o, _row_to_col(jnp.exp(g_end)) * s_mat + _dot_tn(kt, vb))


def _gla_kernel(*refs, bb, nh, c, n_valid, has_init, n_alias, kind, layer):
    if kind == "hgrn":
        q_ref, k_ref, v_ref, g_ref, par_ref, on_ref = refs[:6]
        rest = refs[6:]
    else:
        q_ref, k_ref, v_ref, g_ref, zc_ref, wg_ref, bg_ref, on_ref = refs[:8]
        rest = refs[8:]
    if has_init:
        s0_ref = rest[0]
        rest = rest[1:]
    o_ref, st_ref = rest[n_alias:]

    @pl.when(pl.program_id(1) == 0)
    def _():
        if has_init:
            st_ref[...] = s0_ref[...]
        else:
            st_ref[...] = jnp.zeros_like(st_ref)

    hd = HEAD_DIM
    tri_b = (_iota2((c, c), 1) <= _iota2((c, c), 0)).astype(BF16)
    valid_c = _iota2((c, 1), 0) < n_valid
    if kind == "hgrn":
        lbp = par_ref[...]
        ex = jnp.exp(lbp - jnp.max(lbp, axis=0, keepdims=True))
        lbs = ex / jnp.sum(ex, axis=0, keepdims=True)
        lb = jnp.zeros_like(lbs[0:1, :])
        for j in range(1, layer + 1):
            lb = lb + lbs[j:j + 1, :]
    def finish(b, h, hs):
        def emit(o, s_new):
            st_ref[b, h] = s_new
            o_ref[b, :, hs] = _merge(o, _silu(g_ref[b, :, hs]), on_ref[:, hs])
        return emit

    for b in range(bb):
        if kind == "gla":
            gate_in = _dot(zc_ref[b].astype(BF16), wg_ref[...].astype(BF16)) + bg_ref[...]
        units = []
        for h in range(nh):
            hs = slice(h * hd, (h + 1) * hd)
            if kind == "hgrn":
                fg = k_ref[b, :, hs]
                lbh = lb[:, hs]
                f = lbh + (1.0 - lbh) * _sigmoid(fg)
                lg = jnp.log(jnp.maximum(f, MIN_FORGET))
                k = (1.0 - lbh) * _sigmoid(-fg)
                q = _silu(q_ref[b, :, hs])
            else:
                lg = _log_sigmoid(gate_in[:, hs]) / GLA_TAU
                k = k_ref[b, :, hs]
                q = q_ref[b, :, hs] * (hd ** -0.5)
            v = v_ref[b, :, hs]
            if n_valid < c:
                lg = jnp.where(valid_c, lg, 0.0)
                k = jnp.where(valid_c, k, 0.0)
            units.append(_gla_unit(q, k, v, lg, st_ref[b, h], tri_b, finish(b, h, hs)))
        _interleave(units)


def _gdn_kernel(*refs, bb, nh, c, n_valid, has_init, n_alias):
    (q_ref, k_ref, v_ref, g_ref, zc_ref, zr_ref, br_ref, bc_ref, ar_ref, ac_ref, cw_ref, on_ref) = refs[:12]
    n_in = 12
    if has_init:
        s0_ref, cv0_ref = refs[12:14]
        n_in = 14
    o_ref, st_ref, cvo_ref, tail_ref = refs[n_in + n_alias:]
    t_id = pl.program_id(1)
    n_chunks = pl.num_programs(1)
    tail_rows = SUBLANES
    n_buf = CONV_W - 1

    @pl.when(t_id == 0)
    def _():
        tail_ref[...] = jnp.zeros_like(tail_ref)
        if has_init:
            st_ref[...] = s0_ref[...]
            for b in range(bb):
                for pc in range(3):
                    tail_ref[b, pc, tail_rows - n_buf:tail_rows, :] = cv0_ref[b, :, pc, :]
        else:
            st_ref[...] = jnp.zeros_like(st_ref)

    hd = HEAD_DIM
    scale = hd ** -0.5
    ti = _iota2((c, c), 0)
    si = _iota2((c, c), 1)
    incl = si <= ti
    incl_t = ti <= si
    strict = si < ti
    eye = (si == ti).astype(F32)
    valid_c = _iota2((c, 1), 0) < n_valid
    valid_r = _iota2((1, c), 1) < n_valid
    raw_refs = (q_ref, k_ref, v_ref)
    for b in range(bb):
        conv = []
        for pc in range(3):
            u = raw_refs[pc][b]
            ext = jnp.concatenate([tail_ref[b, pc], u], axis=0)
            acc = u * cw_ref[CONV_W - 1:CONV_W, pc, :]
            for j in range(1, CONV_W):
                shifted = pltpu.roll(ext, j, axis=0)[tail_rows:tail_rows + c, :]
                acc = acc + shifted * cw_ref[CONV_W - 1 - j:CONV_W - j, pc, :]
            conv.append(_silu(acc))
            tail_ref[b, pc] = u[c - tail_rows:c, :]

        @pl.when(t_id == n_chunks - 1)
        def _():
            for pc in range(3):
                cvo_ref[b, :, pc, :] = raw_refs[pc][b, n_valid - n_buf:n_valid, :]

        zc = zc_ref[b] + br_ref[...]
        zr = zr_ref[b] + bc_ref[...]
        units = []
        for h in range(nh):
            hs = slice(h * hd, (h + 1) * hd)
            q = conv[0][:, hs]
            k = conv[1][:, hs]
            v = conv[2][:, hs]
            q = q * lax.rsqrt(jnp.sum(q * q, axis=1, keepdims=True) + EPS) * scale
            k = k * lax.rsqrt(jnp.sum(k * k, axis=1, keepdims=True) + EPS)
            ca, cb = 2 * nh + h, 3 * nh + h
            lg_c = -jnp.exp(ar_ref[:, ca:ca + 1]) * _softplus(zc[:, ca:ca + 1])
            lg_r = -jnp.exp(ac_ref[ca:ca + 1, :]) * _softplus(zr[ca:ca + 1, :])
            beta = _sigmoid(zc[:, cb:cb + 1])
            if n_valid < c:
                lg_c = jnp.where(valid_c, lg_c, 0.0)
                lg_r = jnp.where(valid_r, lg_r, 0.0)
                beta = jnp.where(valid_c, beta, 0.0)
            s_mat = st_ref[b, h]

            g_c, g_r = _cumsum_pair(lg_c, lg_r, incl, incl_t)
            eg = jnp.exp(g_c)
            rel = jnp.where(incl, jnp.exp(jnp.where(incl, g_c - g_r, 0.0)), 0.0)
            qb = q.astype(BF16)
            kb = k.astype(BF16)
            kq_s = _dot(jnp.concatenate([kb, qb], axis=0), s_mat.astype(BF16))
            m = jnp.where(strict, beta * rel * _dot_nt(kb, kb), 0.0)
            rhs = beta * (v - eg * kq_s[:c, :])
            qk = (_dot_nt(qb, kb) * rel).astype(BF16)
            units.append(dict(hs=hs, k=k, s_mat=s_mat, g_c=g_c, eg=eg, m=m, rhs=rhs, qk=qk, qs=kq_s[c:, :]))

        b0 = min(INV_BLOCK, c)
        md = [jnp.where((ti // b0) == (si // b0), un["m"], 0.0) for un in units]
        tinv = [eye - x for x in md]
        pw = [_dot_f32(x, x) for x in md]
        n_it = int(math.log2(b0)) - 1
        for it in range(n_it):
            tinv = [t + _dot_f32(t, p) for t, p in zip(tinv, pw)]
            if it < n_it - 1:
                pw = [_dot_f32(p, p) for p in pw]
        blk = b0
        while blk < c:
            below = ((ti // (2 * blk)) == (si // (2 * blk))) & ((ti // blk) != (si // blk))
            tinv = [t - _dot_f32(_dot_f32(t, jnp.where(below, un["m"], 0.0)), t) for t, un in zip(tinv, units)]
            blk *= 2

        for un, t in zip(units, tinv):
            ub = _dot_f32(t, un["rhs"]).astype(BF16)
            o = un["eg"] * un["qs"] + _dot(un["qk"], ub)
            g_end = un["g_c"][c - 1:c, :]
            kd = (jnp.exp(g_end - un["g_c"]) * un["k"]).astype(BF16)
            h = un["hs"].start // hd
            st_ref[b, h] = jnp.exp(g_end) * un["s_mat"] + _dot_tn(kd, ub)
            o_ref[b, :, un["hs"]] = _merge(o, _silu(g_ref[b, :, un["hs"]]), on_ref[:, un["hs"]])


def _mixers(b, t, dg, z_big, zs, states, prev, prm, layer, depth):
    nh = dg // HEAD_DIM
    hd = HEAD_DIM
    n = b * t
    tp = -(-t // SUBLANES) * SUBLANES
    c = min(CHUNK, tp)
    assert tp % c == 0 and (tp == t or tp == c) and t >= CONV_W - 1 and 4 * nh + GLA_RANK <= SMALL_ROWS
    nc = tp // c
    n_valid = c - (tp - t)
    has_init = states is not None
    bb = 2 if (has_init and b % 2 == 0) else 1

    z3 = z_big.reshape(b, t, 16 * dg)
    zs3 = zs.reshape(b, t, LANES)
    if tp != t:
        z3 = jnp.pad(z3, ((0, 0), (0, tp - t), (0, 0)))
        zs3 = jnp.pad(zs3, ((0, 0), (0, tp - t), (0, 0)))
    zr4 = jnp.swapaxes(zs3[:, :, :SMALL_ROWS].reshape(b, nc, c, SMALL_ROWS), 2, 3)

    grid = (b // bb, nc)
    piece = lambda p: pl.BlockSpec((bb, c, dg), lambda i, j: (i, j, p))
    zc_spec = pl.BlockSpec((bb, c, LANES), lambda i, j: (i, j, 0))
    zr_spec = pl.BlockSpec((bb, None, SMALL_ROWS, c), lambda i, j: (i, j, 0, 0))
    full2 = lambda a: pl.BlockSpec(a.shape, lambda i, j: (0,) * a.ndim)
    mat_spec = pl.BlockSpec((None, bb, nh, hd, hd), lambda i, j: (layer, i, 0, 0, 0))
    n_spec = pl.BlockSpec((None, bb, nh, hd), lambda i, j: (layer, i, 0, 0))
    m_spec = pl.BlockSpec((None, bb, 1, nh), lambda i, j: (layer, i, 0, 0))
    cv_spec = pl.BlockSpec((None, bb, CONV_W - 1, 3, dg), lambda i, j: (layer, i, 0, 0, 0))
    mat_shape = jax.ShapeDtypeStruct((depth, b, nh, hd, hd), F32)
    n_shape = jax.ShapeDtypeStruct((depth, b, nh, hd), F32)
    m_shape = jax.ShapeDtypeStruct((depth, b, 1, nh), F32)
    cv_shape = jax.ShapeDtypeStruct((depth, b, CONV_W - 1, 3, dg), F32)
    any_spec = pl.BlockSpec(memory_space=pl.ANY)
    o_spec = pl.BlockSpec((bb, c, dg), lambda i, j: (i, j, 0))
    o_shape = jax.ShapeDtypeStruct((b, tp, dg), BF16)
    cp = _cparams(("arbitrary", "arbitrary"))
    on = prm["out_norm"][layer].reshape(N_MIXERS, 1, dg)
    br, bc, ar, ac = prm["bias_row"][layer], prm["bias_col"][layer], prm["alog_row"][layer], prm["alog_col"][layer]

    def call(body, ins, specs, init, state_specs, state_shapes, prev_arrays, scratch=()):
        ins, specs = list(ins), list(specs)
        if has_init:
            ins += init
            specs += state_specs
        n_alias = 0 if prev_arrays is None else len(prev_arrays)
        aliases = {}
        if n_alias:
            aliases = {len(ins) + k: 1 + k for k in range(n_alias)}
            ins += list(prev_arrays)
            specs += [any_spec] * n_alias
        return pl.pallas_call(
            functools.partial(body, bb=bb, nh=nh, c=c, n_valid=n_valid, has_init=has_init, n_alias=n_alias),
            grid=grid, in_specs=specs, out_specs=[o_spec] + list(state_specs),
            out_shape=[o_shape] + list(state_shapes), scratch_shapes=list(scratch),
            input_output_aliases=aliases, compiler_params=cp,
        )(*ins)

    st = states
    pv = prev
    o_a, m_c, m_n, m_m = call(
        _mlstm_kernel, [z3, z3, z3, z3, zs3, zr4, br, bc, on[0]],
        [piece(0), piece(1), piece(2), piece(3), zc_spec, zr_spec, full2(br), full2(bc), full2(on[0])],
        None if st is None else [st[0], st[1], st[2].reshape(depth, b, 1, nh)],
        [mat_spec, n_spec, m_spec], [mat_shape, n_shape, m_shape],
        None if pv is None else [pv[0], pv[1], pv[2]])

    lbp = prm["hgrn_lb"]
    o_b, s_hgrn = call(
        functools.partial(_gla_kernel, kind="hgrn", layer=layer), [z3, z3, z3, z3, lbp, on[1]],
        [piece(4), piece(5), piece(6), piece(7), full2(lbp), full2(on[1])],
        None if st is None else [st[3]], [mat_spec], [mat_shape], None if pv is None else [pv[3]])

    cw = prm["gdn_conv_w"][layer].reshape(CONV_W, 3, dg)
    o_c, s_gdn, s_conv = call(
        _gdn_kernel, [z3, z3, z3, z3, zs3, zr4, br, bc, ar, ac, cw, on[2]],
        [piece(8), piece(9), piece(10), piece(11), zc_spec, zr_spec, full2(br), full2(bc), full2(ar),
         full2(ac), full2(cw), full2(on[2])],
        None if st is None else [st[4], st[5].reshape(depth, b, CONV_W - 1, 3, dg)],
        [mat_spec, cv_spec], [mat_shape, cv_shape], None if pv is None else [pv[4], pv[5]],
        scratch=[pltpu.VMEM((bb, 3, SUBLANES, dg), F32)])

    wg, bg = prm["gla_w_pad"][layer], prm["gla_b_gate"][layer].reshape(1, dg)
    o_d, s_gla = call(
        functools.partial(_gla_kernel, kind="gla", layer=layer), [z3, z3, z3, z3, zs3, wg, bg, on[3]],
        [piece(12), piece(13), piece(14), piece(15), zc_spec, full2(wg), full2(bg), full2(on[3])],
        None if st is None else [st[6]], [mat_spec], [mat_shape], None if pv is None else [pv[6]])

    outs = [o[:, :t, :].reshape(n, dg) for o in (o_a, o_b, o_c, o_d)]
    return outs, (m_c, m_n, m_m, s_hgrn, s_gdn, s_conv, s_gla)


def _out_proj_kernel(oa_ref, ob_ref, oc_ref, od_ref, w_ref, x_ref, g_ref, y_ref):
    acc = _dot(oa_ref[...], w_ref[0])
    acc = acc + _dot(ob_ref[...], w_ref[1])
    acc = acc + _dot(oc_ref[...], w_ref[2])
    acc = acc + _dot(od_ref[...], w_ref[3])
    y_ref[...] = x_ref[...] + g_ref[...] * acc


def _out_proj(til, outs, w_out4, x, mod_arr, layer):
    n, d, tm = til.n, til.d, til.tm
    dg = d // N_MIXERS
    tn = min(1024, d)
    o_spec = pl.BlockSpec((tm, dg), lambda i, j: (i, 0))
    if til.per_batch:
        g_spec = pl.BlockSpec((None, 1, tn), lambda i, j: ((i * tm) // til.t, 0, (2 * d) // tn + j))
    else:
        g_spec = pl.BlockSpec((tm, tn), lambda i, j: (i, (2 * d) // tn + j))
    return pl.pallas_call(
        _out_proj_kernel,
        grid=(til.tiles, d // tn),
        in_specs=[o_spec, o_spec, o_spec, o_spec,
                  pl.BlockSpec((None, N_MIXERS, dg, tn), lambda i, j: (layer, 0, 0, j)),
                  pl.BlockSpec((tm, tn), lambda i, j: (i, j)),
                  g_spec],
        out_specs=pl.BlockSpec((tm, tn), lambda i, j: (i, j)),
        out_shape=jax.ShapeDtypeStruct((n, d), F32),
        compiler_params=_cparams(("arbitrary", "arbitrary")),
    )(*outs, w_out4, x, mod_arr)


def _top_desc(s, count):
    rows = float(s.shape[0])
    ri = _iota2(s.shape, 0).astype(F32)
    vals = []
    for r in range(count):
        mx = jnp.max(s, axis=0, keepdims=True)
        vals.append(mx)
        if r < count - 1:
            first = jnp.min(jnp.where(s == mx, ri, rows), axis=0, keepdims=True)
            s = jnp.where(ri == first, -jnp.inf, s)
    return vals


def _sort16_network():
    def merge(lo, hi, r):
        step = r * 2
        if step < hi - lo:
            yield from merge(lo, hi, step)
            yield from merge(lo + r, hi, step)
            yield from [(i, i + r) for i in range(lo + r, hi - r, step)]
        else:
            yield (lo, lo + r)

    def sort(lo, hi):
        if hi - lo >= 1:
            mid = lo + (hi - lo) // 2
            yield from sort(lo, mid)
            yield from sort(mid + 1, hi)
            yield from merge(lo, hi, 1)

    return list(sort(0, PEER_TOPK - 1))


def _bitonic_merge16():
    out, s = [], PEER_TOPK // 2
    while s >= 1:
        out += [(i, i + s) for i in range(PEER_TOPK) if (i & s) == 0]
        s //= 2
    return out


def _top16_sorted(s):
    assert s.shape[0] == PEER_TOPK * SUBLANES
    slabs = [s[SUBLANES * k:SUBLANES * (k + 1), :] for k in range(PEER_TOPK)]

    def exchange(net):
        for i, j in net:
            slabs[i], slabs[j] = jnp.maximum(slabs[i], slabs[j]), jnp.minimum(slabs[i], slabs[j])

    exchange(_sort16_network())
    merge_net = _bitonic_merge16()
    for shift in (4, 2, 1):
        other = [pltpu.roll(x, shift, axis=0) for x in slabs]
        for k in range(PEER_TOPK):
            slabs[k] = jnp.maximum(slabs[k], other[PEER_TOPK - 1 - k])
        exchange(merge_net)
    return [x[0:1, :] for x in slabs]


def _cand_pairs():
    return [(a, b) for a in range(PEER_TOPK) for b in range(PEER_TOPK) if (a + 1) * (b + 1) <= PEER_TOPK]


def _route_kernel(x_ref, nw_ref, sc_ref, sh_ref, wq_ref, key_ref,
                  h2t_ref, s1_ref, s2_ref, e1_ref, e2_ref, tau_ref, cand_ref, h2_scr):
    @pl.when(pl.program_id(1) == 0)
    def _():
        h2 = _rms_mod(x_ref[...], nw_ref[...], sc_ref[...], sh_ref[...])
        h2_scr[...] = h2.astype(BF16)
        h2t_ref[...] = h2.T.astype(BF16)

    half = PEER_KEYS
    q = _dot(h2_scr[...], wq_ref[...])
    s1 = _dot_nt(key_ref[0].astype(BF16), q[:, :half].astype(BF16))
    s2 = _dot_nt(key_ref[1].astype(BF16), q[:, half:].astype(BF16))
    v1 = _top16_sorted(s1)
    v2 = _top16_sorted(s2)
    pairs = _cand_pairs()
    cand_ref[...] = jnp.full(cand_ref.shape, -jnp.inf, F32)
    for r, (a, b) in enumerate(pairs):
        cand_ref[r:r + 1, :] = v1[a] + v2[b]
    best = _top_desc(cand_ref[...], PEER_TOPK)
    zsum = jnp.zeros_like(best[0])
    for r in range(PEER_TOPK):
        zsum = zsum + jnp.exp(best[r] - best[0])
    s1_ref[...] = s1
    s2_ref[...] = s2
    e1_ref[...] = jnp.exp(s1 - v1[0]) / zsum
    e2_ref[...] = jnp.exp(s2 - v2[0])
    tau_ref[...] = best[PEER_TOPK - 1]


def _route(til, x1, nw, mod_arr, w_q, sub_keys, layer):
    n, d, tm = til.n, til.d, til.tm
    qd = w_q.shape[2] // PEER_HEADS
    n_cand = -(-len(_cand_pairs()) // SUBLANES) * SUBLANES
    tok = pl.BlockSpec((None, PEER_KEYS, tm), lambda i, h: (h, 0, i))
    tok_shape = jax.ShapeDtypeStruct((PEER_HEADS, PEER_KEYS, n), F32)
    return pl.pallas_call(
        _route_kernel,
        grid=(til.tiles, PEER_HEADS),
        in_specs=[
            pl.BlockSpec((tm, d), lambda i, h: (i, 0)),
            pl.BlockSpec((None, 1, d), lambda i, h: (layer, 0, 0)),
            til.mod_spec(4, 2),
            til.mod_spec(3, 2),
            pl.BlockSpec((None, d, qd), lambda i, h: (layer, 0, h)),
            pl.BlockSpec((None, None, 2, PEER_KEYS, qd // 2), lambda i, h: (layer, h, 0, 0, 0)),
        ],
        out_specs=[
            pl.BlockSpec((d, tm), lambda i, h: (0, i)),
            tok, tok, tok, tok,
            pl.BlockSpec((None, 1, tm), lambda i, h: (h, 0, i)),
        ],
        out_shape=[jax.ShapeDtypeStruct((d, n), BF16), tok_shape, tok_shape, tok_shape, tok_shape,
                   jax.ShapeDtypeStruct((PEER_HEADS, 1, n), F32)],
        scratch_shapes=[pltpu.VMEM((n_cand, tm), F32), pltpu.VMEM((tm, d), BF16)],
        compiler_params=_cparams(("arbitrary", "arbitrary")),
    )(x1, nw, mod_arr, mod_arr, w_q, sub_keys)


def _peer_kernel(h2t_ref, u_ref, vt_ref, s1_ref, s2_ref, e1_ref, e2_ref, tau_ref, o_ref, act_scr, p_scr, *, te, n_et):
    e = pl.program_id(1)

    @pl.when(e == 0)
    def _():
        o_ref[...] = jnp.zeros_like(o_ref)
        act_scr[...] = jnp.zeros_like(act_scr)
        p_scr[...] = jnp.zeros_like(p_scr)

    o_ref[...] += _dot(vt_ref[...], p_scr[...])

    groups = te // PEER_KEYS
    tile = jnp.clip(e - 1, 0, n_et - 1)
    for ii in range(groups):
        rows = slice(ii * PEER_KEYS, (ii + 1) * PEER_KEYS)
        act = act_scr[rows, :]
        gel = 0.5 * act * (1.0 + lax.erf(act * (2.0 ** -0.5)))
        row = tile * groups + ii
        acc = jnp.zeros(act.shape, F32)
        for h in range(PEER_HEADS):
            sm = s1_ref[h, pl.ds(row, 1), :] + s2_ref[h]
            sel = jnp.where(sm >= tau_ref[h], e2_ref[h], 0.0)
            acc = acc + sel * e1_ref[h, pl.ds(row, 1), :]
        p_scr[rows, :] = (acc * gel).astype(BF16)

    act_scr[...] = _dot(u_ref[...], h2t_ref[...])


def _peer(til, h2, routing, u_tab, vt_tab, layer):
    n, d, tm = til.n, til.d, til.tm
    ne = u_tab.shape[1]
    te = EXPERT_TILE
    n_et = ne // te
    once = pl.Buffered(1)
    tok = pl.BlockSpec((PEER_HEADS, PEER_KEYS, tm), lambda i, e: (0, 0, i), pipeline_mode=once)
    return pl.pallas_call(
        functools.partial(_peer_kernel, te=te, n_et=n_et),
        grid=(til.tiles, n_et + 2),
        in_specs=[
            pl.BlockSpec((d, tm), lambda i, e: (0, i), pipeline_mode=once),
            pl.BlockSpec((None, te, d), lambda i, e: (layer, jnp.minimum(e, n_et - 1), 0)),
            pl.BlockSpec((None, d, te), lambda i, e: (layer, 0, jnp.clip(e - 2, 0, n_et - 1))),
            tok, tok, tok, tok,
            pl.BlockSpec((PEER_HEADS, 1, tm), lambda i, e: (0, 0, i), pipeline_mode=once),
        ],
        out_specs=pl.BlockSpec((d, tm), lambda i, e: (0, i), pipeline_mode=once),
        out_shape=jax.ShapeDtypeStruct((d, n), F32),
        scratch_shapes=[pltpu.VMEM((te, tm), F32), pltpu.VMEM((te, tm), BF16)],
        compiler_params=_cparams(("arbitrary", "arbitrary")),
    )(h2, u_tab, vt_tab, *routing)


def _residual_kernel(x_ref, pt_ref, g_ref, o_ref):
    o_ref[...] = x_ref[...] + g_ref[...] * pt_ref[...].T


def _residual_norm_kernel(x_ref, pt_ref, g_ref, nw_ref, o_ref):
    x = x_ref[...] + g_ref[...] * pt_ref[...].T
    o_ref[...] = x * lax.rsqrt(jnp.mean(x * x, axis=-1, keepdims=True) + EPS) * nw_ref[...]


def _residual(til, x1, p, mod_arr, final_norm):
    til = _Tiling(til.b, til.t, til.d, tile=256)
    n, d, tm = til.n, til.d, til.tm
    row = pl.BlockSpec((tm, d), lambda i: (i, 0))
    ins = [x1, p, mod_arr]
    specs = [row, pl.BlockSpec((d, tm), lambda i: (0, i)), til.mod_spec(5, 1)]
    body = _residual_kernel
    if final_norm is not None:
        ins.append(final_norm.reshape(1, d))
        specs.append(pl.BlockSpec((1, d), lambda i: (0, 0)))
        body = _residual_norm_kernel
    return pl.pallas_call(
        body, grid=(til.tiles,), in_specs=specs, out_specs=row,
        out_shape=jax.ShapeDtypeStruct((n, d), F32),
        compiler_params=_cparams(("arbitrary",)),
    )(*ins)


def _trunk(x, mod, states, prm):
    b, t, d = x.shape
    depth = mod.shape[0]
    dg = d // N_MIXERS
    til = _Tiling(b, t, d)
    xf = x.reshape(b * t, d)
    new = None
    for l in range(depth):
        mod_arr = til.mod_array(mod[l])
        z_big, zs = _in_proj(til, xf, prm["norm_mix"], mod_arr, prm["w_in_b"], l)
        outs, new = _mixers(b, t, dg, z_big, zs, states, new, prm, l, depth)
        x1 = _out_proj(til, outs, prm["w_out4"], xf, mod_arr, l)
        h2, *routing = _route(til, x1, prm["norm_ffn"], mod_arr, prm["w_q"], prm["peer_sub_keys"], l)
        p = _peer(til, h2, routing, prm["peer_u"], prm["peer_vt"], l)
        xf = _residual(til, x1, p, mod_arr, prm["final_norm"] if l == depth - 1 else None)
    m_c, m_n, m_m, s_hgrn, s_gdn, s_conv, s_gla = new
    nh = dg // HEAD_DIM
    new_states = [m_c, m_n, m_m.reshape(depth, b, nh), s_hgrn, s_gdn,
                  s_conv.reshape(depth, b, CONV_W - 1, 3 * dg), s_gla]
    return xf.reshape(b, t, d), new_states


def _prepare(mlstm_b_i, mlstm_b_f, gdn_a_log, gdn_dt_bias, gla_w_gate, d):
    depth = mlstm_b_i.shape[0]
    dg = d // N_MIXERS
    nh = dg // HEAD_DIM
    n_small = 4 * nh + GLA_RANK
    zeros = lambda k: jnp.zeros((depth, k), F32)
    bias = jnp.concatenate([mlstm_b_i, mlstm_b_f, gdn_dt_bias, zeros(LANES - 3 * nh)], axis=1)
    alog = jnp.concatenate([zeros(2 * nh), gdn_a_log, zeros(LANES - 3 * nh)], axis=1)
    gla_w_pad = jnp.concatenate(
        [jnp.zeros((depth, 4 * nh, dg), F32), gla_w_gate, jnp.zeros((depth, LANES - n_small, dg), F32)], axis=1)
    return dict(
        bias_row=bias.reshape(depth, 1, LANES), bias_col=bias[:, :SMALL_ROWS].reshape(depth, SMALL_ROWS, 1),
        alog_row=alog.reshape(depth, 1, LANES), alog_col=alog[:, :SMALL_ROWS].reshape(depth, SMALL_ROWS, 1),
        gla_w_pad=gla_w_pad)


def kernel(x_prompt, x_sample, c_prompt, c_sample, state_mlstm_C, state_mlstm_n, state_mlstm_m, state_hgrn, state_gdn, state_gdn_conv, state_gla, w_ada, b_ada, norm_mix, norm_ffn, w_in, mlstm_b_i, mlstm_b_f, hgrn_lb, gdn_conv_w, gdn_a_log, gdn_dt_bias, gla_w_gate, gla_b_gate, out_norm, w_out, peer_w_q, peer_sub_keys, peer_u, peer_v, final_norm):
    depth, d = norm_mix.shape
    dg = d // N_MIXERS
    assert d % (N_MIXERS * HEAD_DIM) == 0
    prm = _prepare(mlstm_b_i, mlstm_b_f, gdn_a_log, gdn_dt_bias, gla_w_gate, d)
    prm.update(
        norm_mix=norm_mix.reshape(depth, 1, d), norm_ffn=norm_ffn.reshape(depth, 1, d), hgrn_lb=hgrn_lb,
        gdn_conv_w=gdn_conv_w, gla_b_gate=gla_b_gate, out_norm=out_norm, final_norm=final_norm,
        peer_sub_keys=peer_sub_keys, w_in_b=w_in.astype(BF16),
        w_out4=w_out.astype(BF16).reshape(depth, N_MIXERS, dg, d),
        w_q=peer_w_q.astype(BF16), peer_u=peer_u.astype(BF16),
        peer_vt=jnp.swapaxes(peer_v, 1, 2).astype(BF16))

    bp, bs = c_prompt.shape[0], c_sample.shape[0]
    rows = -(-(bp + bs) // SUBLANES) * SUBLANES
    c_all = jnp.concatenate([c_prompt, c_sample, jnp.zeros((rows - bp - bs, d), F32)], axis=0)
    mod = _ada(c_all, w_ada, b_ada)

    y_prompt, p_states = _trunk(x_prompt, mod[:, :bp], None, prm)
    past = (state_mlstm_C, state_mlstm_n, state_mlstm_m, state_hgrn, state_gdn, state_gdn_conv, state_gla)
    y_sample, s_states = _trunk(x_sample, mod[:, bp:bp + bs], past, prm)
    return (y_prompt, y_sample, *p_states, *s_states)
```

```python
import functools
import math

import jax
import jax.numpy as jnp
from jax import lax
from jax.experimental import pallas as pl
from jax.experimental.pallas import tpu as pltpu

F32 = jnp.float32
BF16 = jnp.bfloat16

HEAD_DIM = 256
N_MIXERS = 4
CONV_W = 4
GLA_RANK = 16
GLA_TAU = 16.0
PEER_HEADS = 8
PEER_KEYS = 128
PEER_TOPK = 16
N_MOD = 6
EPS = 1e-6
NEG_BIG = -1e30
MIN_FORGET = 1e-6

LANES = 128
SUBLANES = 8
SMALL_ROWS = 32
CHUNK = 128
SUB = 8
INV_BLOCK = 16
TOKEN_TILE = 512
EXPERT_TILE = 512
ROUTE_HEADS = 2
VMEM_LIMIT = 56 * 1024 * 1024


def _cparams(sem):
    return pltpu.CompilerParams(dimension_semantics=sem, vmem_limit_bytes=VMEM_LIMIT)


def _dot(a, b):
    return jnp.dot(a, b, preferred_element_type=F32)


def _dot_nt(a, b):
    return lax.dot_general(a, b, (((1,), (1,)), ((), ())), preferred_element_type=F32)


def _dot_tn(a, b):
    return lax.dot_general(a, b, (((0,), (0,)), ((), ())), preferred_element_type=F32)


def _split2(a):
    hi = a.astype(BF16)
    return hi, (a - hi.astype(F32)).astype(BF16)


def _dot_f32(a, b):
    a_hi, a_lo = _split2(a)
    b_hi, b_lo = _split2(b)
    return _dot(a_hi, b_hi) + (_dot(a_hi, b_lo) + _dot(a_lo, b_hi))


def _sigmoid(x):
    return 1.0 / (1.0 + jnp.exp(-x))


def _silu(x):
    return x * _sigmoid(x)


def _log_sigmoid(x):
    return jnp.minimum(x, 0.0) - jnp.log1p(jnp.exp(-jnp.abs(x)))


def _softplus(x):
    return jnp.maximum(x, 0.0) + jnp.log1p(jnp.exp(-jnp.abs(x)))


def _rms_mod(x, nw, sc, sh):
    y = x * lax.rsqrt(jnp.mean(x * x, axis=-1, keepdims=True) + EPS) * nw
    return y * (1.0 + sc) + sh


def _rms_mod_rows(x_ref, nw_ref, sc_ref, sh_ref, out_ref, rows=LANES):
    tm = x_ref.shape[0]
    step = rows if tm % rows == 0 else tm
    for r in range(0, tm, step):
        sl = slice(r, r + step)
        sc = sc_ref[...] if sc_ref.shape[0] == 1 else sc_ref[sl, :]
        sh = sh_ref[...] if sh_ref.shape[0] == 1 else sh_ref[sl, :]
        out_ref[sl, :] = _rms_mod(x_ref[sl, :], nw_ref[...], sc, sh).astype(out_ref.dtype)


def _merge(h, gate, onorm):
    hn = h * lax.rsqrt(jnp.mean(h * h, axis=-1, keepdims=True) + EPS)
    return (hn * onorm * gate).astype(BF16)


def _iota2(shape, dim):
    return lax.broadcasted_iota(jnp.int32, shape, dim)


def _row_to_col(r):
    n = r.shape[1]
    eye = _iota2((n, n), 0) == _iota2((n, n), 1)
    return jnp.sum(jnp.where(eye, r, 0.0), axis=1, keepdims=True)


def _interleave(units):
    units = list(units)
    while units:
        alive = []
        for u in units:
            try:
                next(u)
                alive.append(u)
            except StopIteration:
                pass
        units = alive


def _cumsum_pair(x_c, x_r, incl, incl_t):
    f_c = jnp.sum(jnp.where(incl, x_r, 0.0), axis=1, keepdims=True)
    f_r = jnp.sum(jnp.where(incl_t, x_c, 0.0), axis=0, keepdims=True)
    return f_c, f_r


def _cumsum_rows(x, tri_b):
    hi = x.astype(BF16)
    r1 = x - hi.astype(F32)
    mid = r1.astype(BF16)
    lo = (r1 - mid.astype(F32)).astype(BF16)
    return _dot(tri_b, hi) + _dot(tri_b, mid) + _dot(tri_b, lo)


def _ada_kernel(c_ref, w_ref, b_ref, o_ref):
    cs = _silu(c_ref[...]).astype(BF16)
    o_ref[...] = _dot(cs, w_ref[...].astype(BF16)) + b_ref[...]


def _ada(c_all, w_ada, b_ada):
    depth, d, n6 = w_ada.shape
    rows = c_all.shape[0]
    tn = 512
    return pl.pallas_call(
        _ada_kernel,
        grid=(depth, n6 // tn),
        in_specs=[
            pl.BlockSpec((rows, d), lambda l, j: (0, 0)),
            pl.BlockSpec((None, d, tn), lambda l, j: (l, 0, j)),
            pl.BlockSpec((None, 1, tn), lambda l, j: (l, 0, j)),
        ],
        out_specs=pl.BlockSpec((None, rows, tn), lambda l, j: (l, 0, j)),
        out_shape=jax.ShapeDtypeStruct((depth, rows, n6), F32),
        compiler_params=_cparams(("arbitrary", "arbitrary")),
    )(c_all, w_ada, b_ada.reshape(depth, 1, n6))


class _Tiling:
    def __init__(self, b, t, d, tile=TOKEN_TILE):
        self.b, self.t, self.d = b, t, d
        self.n = b * t
        self.per_batch = t % LANES == 0
        if self.per_batch:
            self.tm = next(m for m in (tile, 256, LANES) if m <= tile and t % m == 0)
        else:
            self.tm = self.n if self.n <= tile else tile
            assert self.n % self.tm == 0 and self.tm % SUBLANES == 0
        self.tiles = self.n // self.tm

    def mod_array(self, mod_l):
        if self.per_batch:
            return mod_l.reshape(self.b, 1, mod_l.shape[-1])
        return jnp.repeat(mod_l, self.t, axis=0)

    def mod_spec(self, k, grid_rank):
        d, tm, t = self.d, self.tm, self.t
        if self.per_batch:
            if grid_rank == 1:
                return pl.BlockSpec((None, 1, d), lambda i: ((i * tm) // t, 0, k))
            return pl.BlockSpec((None, 1, d), lambda i, j: ((i * tm) // t, 0, k))
        if grid_rank == 1:
            return pl.BlockSpec((tm, d), lambda i: (i, k))
        return pl.BlockSpec((tm, d), lambda i, j: (i, k))


def _in_proj_kernel(x_ref, nw_ref, sc_ref, sh_ref, w_ref, z_ref, zs_ref, h_scr, prev_scr, *, nh, tiles_per_piece):
    j = pl.program_id(1)
    n_tiles = pl.num_programs(1) - 1
    tn = z_ref.shape[1]
    width = tn + LANES
    groups = ((4, 0, 2 * nh), (12, 2 * nh, 2 * nh), (16, 4 * nh, GLA_RANK))

    def emit(head):
        piece = (j - 1) // tiles_per_piece
        amount = jnp.where(piece < 4, 0, width - jnp.where(piece < 12, 2 * nh, 4 * nh))
        zz = jnp.concatenate([prev_scr[...], head], axis=1)
        z_ref[...] = pltpu.roll(zz, amount, axis=1)[:, :tn]
        for hi, s, n_gate in groups:
            @pl.when(j == hi * tiles_per_piece)
            def _():
                lane = _iota2(head.shape, 1)
                zs_ref[...] = jnp.where((lane >= s) & (lane < s + n_gate), head, zs_ref[...])

    @pl.when(j == 0)
    def _():
        _rms_mod_rows(x_ref, nw_ref, sc_ref, sh_ref, h_scr)
        zs_ref[...] = jnp.zeros_like(zs_ref)
        prev_scr[...] = _dot(h_scr[...], w_ref[...])

    @pl.when((j > 0) & (j < n_tiles))
    def _():
        za = _dot(h_scr[...], w_ref[...])
        emit(za[:, :LANES])
        prev_scr[...] = za

    @pl.when(j == n_tiles)
    def _():
        emit(_dot(h_scr[...], w_ref[:, :LANES]))


def _in_proj(til, x, nw, mod_arr, w_in_b, layer):
    if not til.per_batch:
        til = _Tiling(til.b, til.t, til.d, tile=256)
    n, d, tm = til.n, til.d, til.tm
    dg = d // N_MIXERS
    nh = dg // HEAD_DIM
    nbig = 16 * dg
    tn = min(1024, dg)
    assert 4 * nh + GLA_RANK <= LANES and w_in_b.shape[2] == nbig + 4 * nh + GLA_RANK
    return pl.pallas_call(
        functools.partial(_in_proj_kernel, nh=nh, tiles_per_piece=dg // tn),
        grid=(til.tiles, nbig // tn + 1),
        in_specs=[
            pl.BlockSpec((tm, d), lambda i, j: (i, 0)),
            pl.BlockSpec((None, 1, d), lambda i, j: (layer, 0, 0)),
            til.mod_spec(1, 2),
            til.mod_spec(0, 2),
            pl.BlockSpec((None, d, tn), lambda i, j: (layer, 0, j)),
        ],
        out_specs=[
            pl.BlockSpec((tm, tn), lambda i, j: (i, jnp.maximum(j - 1, 0))),
            pl.BlockSpec((tm, LANES), lambda i, j: (i, 0)),
        ],
        out_shape=[jax.ShapeDtypeStruct((n, nbig), F32), jax.ShapeDtypeStruct((n, LANES), F32)],
        scratch_shapes=[pltpu.VMEM((tm, d), BF16), pltpu.VMEM((tm, tn), F32)],
        compiler_params=_cparams(("arbitrary", "arbitrary")),
    )(x, nw, mod_arr, mod_arr, w_in_b)


def _mlstm_kernel(*refs, bb, nh, c, n_valid, has_init, n_alias):
    q_ref, k_ref, v_ref, g_ref, zc_ref, zr_ref, br_ref, bc_ref, on_ref = refs[:9]
    n_in = 9
    if has_init:
        c0_ref, n0_ref, m0_ref = refs[9:12]
        n_in = 12
    o_ref, cs_ref, ns_ref, ms_ref = refs[n_in + n_alias:]

    @pl.when(pl.program_id(1) == 0)
    def _():
        if has_init:
            cs_ref[...] = c0_ref[...]
            ns_ref[...] = n0_ref[...]
            ms_ref[...] = m0_ref[...]
        else:
            cs_ref[...] = jnp.zeros_like(cs_ref)
            ns_ref[...] = jnp.zeros_like(ns_ref)
            ms_ref[...] = jnp.zeros_like(ms_ref)

    hd = HEAD_DIM
    scale = hd ** -0.5
    ti = _iota2((c, c), 0)
    si = _iota2((c, c), 1)
    incl = si <= ti
    incl_t = ti <= si
    valid_c = _iota2((c, 1), 0) < n_valid
    valid_r = _iota2((1, c), 1) < n_valid
    def unit(b, h, zc, zr):
        hs = slice(h * hd, (h + 1) * hd)
        q = q_ref[b, :, hs]
        k = k_ref[b, :, hs] * scale
        v = v_ref[b, :, hs]
        ig_c = zc[:, h:h + 1]
        lf_c = _log_sigmoid(zc[:, nh + h:nh + h + 1])
        ig_r = zr[h:h + 1, :]
        lf_r = _log_sigmoid(zr[nh + h:nh + h + 1, :])
        if n_valid < c:
            ig_c = jnp.where(valid_c, ig_c, NEG_BIG)
            lf_c = jnp.where(valid_c, lf_c, 0.0)
            ig_r = jnp.where(valid_r, ig_r, NEG_BIG)
            lf_r = jnp.where(valid_r, lf_r, 0.0)
        cm = cs_ref[b, h]
        nv = ns_ref[b, h:h + 1, :]
        m0 = ms_ref[b, :, h:h + 1]
        f_c, f_r = _cumsum_pair(lf_c, lf_r, incl, incl_t)
        yield
        raw = f_c - f_r + ig_r
        a = f_c + m0
        m_t = jnp.maximum(a, jnp.max(jnp.where(incl, raw, NEG_BIG), axis=1, keepdims=True))
        p = jnp.where(incl, jnp.exp(jnp.where(incl, raw - m_t, 0.0)), 0.0)
        qb = q.astype(BF16)
        kb = k.astype(BF16)
        vb = v.astype(BF16)
        yield
        s = _dot_nt(qb, kb) * p
        inter = jnp.exp(a - m_t)
        qc = _dot(qb, cm.astype(BF16))
        yield
        num = inter * qc + _dot(s.astype(BF16), vb)
        den = inter * jnp.sum(q * nv, axis=1, keepdims=True) + jnp.sum(s, axis=1, keepdims=True)
        m_end = m_t[c - 1:c, :]
        w_end = jnp.exp(f_c[c - 1:c, :] - f_c + ig_c - m_end)
        dec = jnp.exp(a[c - 1:c, :] - m_end)
        kw = w_end * k
        yield
        hh = num / jnp.maximum(jnp.abs(den), jnp.exp(-m_t))
        cs_ref[b, h] = dec * cm + _dot_tn(kw.astype(BF16), vb)
        ns_ref[b, h:h + 1, :] = dec * nv + jnp.sum(kw, axis=0, keepdims=True)
        ms_ref[b, :, h:h + 1] = m_end
        o_ref[b, :, hs] = _merge(hh, _sigmoid(g_ref[b, :, hs]), on_ref[:, hs])

    for b in range(bb):
        zc = zc_ref[b] + br_ref[...]
        zr = zr_ref[b] + bc_ref[...]
        _interleave([unit(b, h, zc, zr) for h in range(nh)])


def _gla_unit(q, k, v, lg, s_mat, tri_b, emit):
    c, hd = q.shape
    sb = min(SUB, c)
    nb = c // sb
    g = _cumsum_rows(lg, tri_b)
    yield
    o = _dot((q * jnp.exp(g)).astype(BF16), s_mat.astype(BF16))

    if nb > 1:
        qparts, kparts = [], []
        for j in range(nb - 1):
            r1 = (j + 1) * sb
            g_end = g[r1 - 1:r1, :]
            qj = q[r1:, :] * jnp.exp(g[r1:, :] - g_end)
            kj = k[j * sb:r1, :] * jnp.exp(g_end - g[j * sb:r1, :])
            qparts.append(jnp.concatenate([jnp.zeros((r1, hd), F32), qj], axis=0).astype(BF16))
            pieces = [kj]
            if j > 0:
                pieces.insert(0, jnp.zeros((j * sb, hd), F32))
            pieces.append(jnp.zeros((c - r1, hd), F32))
            kparts.append(jnp.concatenate(pieces, axis=0).astype(BF16))
        a_off = _dot_nt(jnp.concatenate(qparts, axis=1), jnp.concatenate(kparts, axis=1))
    else:
        a_off = jnp.zeros((c, c), F32)
    yield

    lane = _iota2((sb, c), 1)
    trow = _iota2((sb, 1), 0)
    strips = []
    for i in range(nb):
        r0 = i * sb
        qi, ki, gi = q[r0:r0 + sb, :], k[r0:r0 + sb, :], g[r0:r0 + sb, :]
        strip = a_off[r0:r0 + sb, :]
        for s in range(sb):
            msk = trow >= s
            w = jnp.where(msk, jnp.exp(jnp.where(msk, gi - gi[s:s + 1, :], 0.0)), 0.0)
            col = jnp.sum(qi * ki[s:s + 1, :] * w, axis=1, keepdims=True)
            strip = jnp.where(lane == r0 + s, col, strip)
        strips.append(strip)
        yield
    a = strips[0] if nb == 1 else jnp.concatenate(strips, axis=0)
    vb = v.astype(BF16)
    o = o + _dot(a.astype(BF16), vb)

    g_end = g[c - 1:c, :]
    kt = (k * jnp.exp(g_end - g)).astype(BF16)
    yield
    emit(o, _row_to_col(jnp.exp(g_end)) * s_mat + _dot_tn(kt, vb))


def _gla_kernel(*refs, bb, nh, c, n_valid, has_init, n_alias, kind, layer):
    if kind == "hgrn":
        q_ref, k_ref, v_ref, g_ref, par_ref, on_ref = refs[:6]
        rest = refs[6:]
    else:
        q_ref, k_ref, v_ref, g_ref, zc_ref, wg_ref, bg_ref, on_ref = refs[:8]
        rest = refs[8:]
    if has_init:
        s0_ref = rest[0]
        rest = rest[1:]
    o_ref, st_ref = rest[n_alias:]

    @pl.when(pl.program_id(1) == 0)
    def _():
        if has_init:
            st_ref[...] = s0_ref[...]
        else:
            st_ref[...] = jnp.zeros_like(st_ref)

    hd = HEAD_DIM
    tri_b = (_iota2((c, c), 1) <= _iota2((c, c), 0)).astype(BF16)
    valid_c = _iota2((c, 1), 0) < n_valid
    if kind == "hgrn":
        lbp = par_ref[...]
        ex = jnp.exp(lbp - jnp.max(lbp, axis=0, keepdims=True))
        lbs = ex / jnp.sum(ex, axis=0, keepdims=True)
        lb = jnp.zeros_like(lbs[0:1, :])
        for j in range(1, layer + 1):
            lb = lb + lbs[j:j + 1, :]
    def finish(b, h, hs):
        def emit(o, s_new):
            st_ref[b, h] = s_new
            o_ref[b, :, hs] = _merge(o, _silu(g_ref[b, :, hs]), on_ref[:, hs])
        return emit

    for b in range(bb):
        if kind == "gla":
            gate_in = _dot(zc_ref[b].astype(BF16), wg_ref[...].astype(BF16)) + bg_ref[...]
        units = []
        for h in range(nh):
            hs = slice(h * hd, (h + 1) * hd)
            if kind == "hgrn":
                fg = k_ref[b, :, hs]
                lbh = lb[:, hs]
                f = lbh + (1.0 - lbh) * _sigmoid(fg)
                lg = jnp.log(jnp.maximum(f, MIN_FORGET))
                k = (1.0 - lbh) * _sigmoid(-fg)
                q = _silu(q_ref[b, :, hs])
            else:
                lg = _log_sigmoid(gate_in[:, hs]) / GLA_TAU
                k = k_ref[b, :, hs]
                q = q_ref[b, :, hs] * (hd ** -0.5)
            v = v_ref[b, :, hs]
            if n_valid < c:
                lg = jnp.where(valid_c, lg, 0.0)
                k = jnp.where(valid_c, k, 0.0)
            units.append(_gla_unit(q, k, v, lg, st_ref[b, h], tri_b, finish(b, h, hs)))
        _interleave(units)


def _gdn_kernel(*refs, bb, nh, c, n_valid, has_init, n_alias):
    (q_ref, k_ref, v_ref, g_ref, zc_ref, zr_ref, br_ref, bc_ref, ar_ref, ac_ref, cw_ref, on_ref) = refs[:12]
    n_in = 12
    if has_init:
        s0_ref, cv0_ref = refs[12:14]
        n_in = 14
    o_ref, st_ref, cvo_ref, tail_ref = refs[n_in + n_alias:]
    t_id = pl.program_id(1)
    n_chunks = pl.num_programs(1)
    tail_rows = SUBLANES
    n_buf = CONV_W - 1

    @pl.when(t_id == 0)
    def _():
        tail_ref[...] = jnp.zeros_like(tail_ref)
        if has_init:
            st_ref[...] = s0_ref[...]
            for b in range(bb):
                for pc in range(3):
                    tail_ref[b, pc, tail_rows - n_buf:tail_rows, :] = cv0_ref[b, :, pc, :]
        else:
            st_ref[...] = jnp.zeros_like(st_ref)

    hd = HEAD_DIM
    scale = hd ** -0.5
    ti = _iota2((c, c), 0)
    si = _iota2((c, c), 1)
    incl = si <= ti
    incl_t = ti <= si
    strict = si < ti
    eye = (si == ti).astype(F32)
    valid_c = _iota2((c, 1), 0) < n_valid
    valid_r = _iota2((1, c), 1) < n_valid
    raw_refs = (q_ref, k_ref, v_ref)
    for b in range(bb):
        conv = []
        for pc in range(3):
            u = raw_refs[pc][b]
            ext = jnp.concatenate([tail_ref[b, pc], u], axis=0)
            acc = u * cw_ref[CONV_W - 1:CONV_W, pc, :]
            for j in range(1, CONV_W):
                shifted = pltpu.roll(ext, j, axis=0)[tail_rows:tail_rows + c, :]
                acc = acc + shifted * cw_ref[CONV_W - 1 - j:CONV_W - j, pc, :]
            conv.append(_silu(acc))
            tail_ref[b, pc] = u[c - tail_rows:c, :]

        @pl.when(t_id == n_chunks - 1)
        def _():
            for pc in range(3):
                cvo_ref[b, :, pc, :] = raw_refs[pc][b, n_valid - n_buf:n_valid, :]

        zc = zc_ref[b] + br_ref[...]
        zr = zr_ref[b] + bc_ref[...]
        units = []
        for h in range(nh):
            hs = slice(h * hd, (h + 1) * hd)
            q = conv[0][:, hs]
            k = conv[1][:, hs]
            v = conv[2][:, hs]
            q = q * lax.rsqrt(jnp.sum(q * q, axis=1, keepdims=True) + EPS) * scale
            k = k * lax.rsqrt(jnp.sum(k * k, axis=1, keepdims=True) + EPS)
            ca, cb = 2 * nh + h, 3 * nh + h
            lg_c = -jnp.exp(ar_ref[:, ca:ca + 1]) * _softplus(zc[:, ca:ca + 1])
            lg_r = -jnp.exp(ac_ref[ca:ca + 1, :]) * _softplus(zr[ca:ca + 1, :])
            beta = _sigmoid(zc[:, cb:cb + 1])
            if n_valid < c:
                lg_c = jnp.where(valid_c, lg_c, 0.0)
                lg_r = jnp.where(valid_r, lg_r, 0.0)
                beta = jnp.where(valid_c, beta, 0.0)
            s_mat = st_ref[b, h]

            g_c, g_r = _cumsum_pair(lg_c, lg_r, incl, incl_t)
            eg = jnp.exp(g_c)
            rel = jnp.where(incl, jnp.exp(jnp.where(incl, g_c - g_r, 0.0)), 0.0)
            qb = q.astype(BF16)
            kb = k.astype(BF16)
            kq_s = _dot(jnp.concatenate([kb, qb], axis=0), s_mat.astype(BF16))
            m = jnp.where(strict, beta * rel * _dot_nt(kb, kb), 0.0)
            rhs = beta * (v - eg * kq_s[:c, :])
            qk = (_dot_nt(qb, kb) * rel).astype(BF16)
            units.append(dict(hs=hs, k=k, s_mat=s_mat, g_c=g_c, eg=eg, m=m, rhs=rhs, qk=qk, qs=kq_s[c:, :]))

        b0 = min(INV_BLOCK, c)
        md = [jnp.where((ti // b0) == (si // b0), un["m"], 0.0) for un in units]
        tinv = [eye - x for x in md]
        pw = [_dot_f32(x, x) for x in md]
        n_it = int(math.log2(b0)) - 1
        for it in range(n_it):
            tinv = [t + _dot_f32(t, p) for t, p in zip(tinv, pw)]
            if it < n_it - 1:
                pw = [_dot_f32(p, p) for p in pw]
        blk = b0
        while blk < c:
            below = ((ti // (2 * blk)) == (si // (2 * blk))) & ((ti // blk) != (si // blk))
            tinv = [t - _dot_f32(_dot_f32(t, jnp.where(below, un["m"], 0.0)), t) for t, un in zip(tinv, units)]
            blk *= 2

        for un, t in zip(units, tinv):
            ub = _dot_f32(t, un["rhs"]).astype(BF16)
            o = un["eg"] * un["qs"] + _dot(un["qk"], ub)
            g_end = un["g_c"][c - 1:c, :]
            kd = (jnp.exp(g_end - un["g_c"]) * un["k"]).astype(BF16)
            h = un["hs"].start // hd
            st_ref[b, h] = jnp.exp(g_end) * un["s_mat"] + _dot_tn(kd, ub)
            o_ref[b, :, un["hs"]] = _merge(o, _silu(g_ref[b, :, un["hs"]]), on_ref[:, un["hs"]])


def _mixers(b, t, dg, z_big, zs, states, prev, prm, layer, depth):
    nh = dg // HEAD_DIM
    hd = HEAD_DIM
    n = b * t
    tp = -(-t // SUBLANES) * SUBLANES
    c = min(CHUNK, tp)
    assert tp % c == 0 and (tp == t or tp == c) and t >= CONV_W - 1 and 4 * nh + GLA_RANK <= SMALL_ROWS
    nc = tp // c
    n_valid = c - (tp - t)
    has_init = states is not None
    bb = 4 if (has_init and b % 4 == 0) else 1

    z3 = z_big.reshape(b, t, 16 * dg)
    zs3 = zs.reshape(b, t, LANES)
    if tp != t:
        z3 = jnp.pad(z3, ((0, 0), (0, tp - t), (0, 0)))
        zs3 = jnp.pad(zs3, ((0, 0), (0, tp - t), (0, 0)))
    zr4 = jnp.swapaxes(zs3[:, :, :SMALL_ROWS].reshape(b, nc, c, SMALL_ROWS), 2, 3)

    grid = (b // bb, nc)
    piece = lambda p: pl.BlockSpec((bb, c, dg), lambda i, j: (i, j, p))
    zc_spec = pl.BlockSpec((bb, c, LANES), lambda i, j: (i, j, 0))
    zr_spec = pl.BlockSpec((bb, None, SMALL_ROWS, c), lambda i, j: (i, j, 0, 0))
    full2 = lambda a: pl.BlockSpec(a.shape, lambda i, j: (0,) * a.ndim)
    mat_spec = pl.BlockSpec((None, bb, nh, hd, hd), lambda i, j: (layer, i, 0, 0, 0))
    n_spec = pl.BlockSpec((None, bb, nh, hd), lambda i, j: (layer, i, 0, 0))
    m_spec = pl.BlockSpec((None, bb, 1, nh), lambda i, j: (layer, i, 0, 0))
    cv_spec = pl.BlockSpec((None, bb, CONV_W - 1, 3, dg), lambda i, j: (layer, i, 0, 0, 0))
    mat_shape = jax.ShapeDtypeStruct((depth, b, nh, hd, hd), F32)
    n_shape = jax.ShapeDtypeStruct((depth, b, nh, hd), F32)
    m_shape = jax.ShapeDtypeStruct((depth, b, 1, nh), F32)
    cv_shape = jax.ShapeDtypeStruct((depth, b, CONV_W - 1, 3, dg), F32)
    any_spec = pl.BlockSpec(memory_space=pl.ANY)
    o_spec = pl.BlockSpec((bb, c, dg), lambda i, j: (i, j, 0))
    o_shape = jax.ShapeDtypeStruct((b, tp, dg), BF16)
    cp = _cparams(("arbitrary", "arbitrary"))
    on = prm["out_norm"][layer].reshape(N_MIXERS, 1, dg)
    br, bc, ar, ac = prm["bias_row"][layer], prm["bias_col"][layer], prm["alog_row"][layer], prm["alog_col"][layer]

    def call(body, ins, specs, init, state_specs, state_shapes, prev_arrays, scratch=()):
        ins, specs = list(ins), list(specs)
        if has_init:
            ins += init
            specs += state_specs
        n_alias = 0 if prev_arrays is None else len(prev_arrays)
        aliases = {}
        if n_alias:
            aliases = {len(ins) + k: 1 + k for k in range(n_alias)}
            ins += list(prev_arrays)
            specs += [any_spec] * n_alias
        return pl.pallas_call(
            functools.partial(body, bb=bb, nh=nh, c=c, n_valid=n_valid, has_init=has_init, n_alias=n_alias),
            grid=grid, in_specs=specs, out_specs=[o_spec] + list(state_specs),
            out_shape=[o_shape] + list(state_shapes), scratch_shapes=list(scratch),
            input_output_aliases=aliases, compiler_params=cp,
        )(*ins)

    st = states
    pv = prev
    o_a, m_c, m_n, m_m = call(
        _mlstm_kernel, [z3, z3, z3, z3, zs3, zr4, br, bc, on[0]],
        [piece(0), piece(1), piece(2), piece(3), zc_spec, zr_spec, full2(br), full2(bc), full2(on[0])],
        None if st is None else [st[0], st[1], st[2].reshape(depth, b, 1, nh)],
        [mat_spec, n_spec, m_spec], [mat_shape, n_shape, m_shape],
        None if pv is None else [pv[0], pv[1], pv[2]])

    lbp = prm["hgrn_lb"]
    o_b, s_hgrn = call(
        functools.partial(_gla_kernel, kind="hgrn", layer=layer), [z3, z3, z3, z3, lbp, on[1]],
        [piece(4), piece(5), piece(6), piece(7), full2(lbp), full2(on[1])],
        None if st is None else [st[3]], [mat_spec], [mat_shape], None if pv is None else [pv[3]])

    cw = prm["gdn_conv_w"][layer].reshape(CONV_W, 3, dg)
    o_c, s_gdn, s_conv = call(
        _gdn_kernel, [z3, z3, z3, z3, zs3, zr4, br, bc, ar, ac, cw, on[2]],
        [piece(8), piece(9), piece(10), piece(11), zc_spec, zr_spec, full2(br), full2(bc), full2(ar),
         full2(ac), full2(cw), full2(on[2])],
        None if st is None else [st[4], st[5].reshape(depth, b, CONV_W - 1, 3, dg)],
        [mat_spec, cv_spec], [mat_shape, cv_shape], None if pv is None else [pv[4], pv[5]],
        scratch=[pltpu.VMEM((bb, 3, SUBLANES, dg), F32)])

    wg, bg = prm["gla_w_pad"][layer], prm["gla_b_gate"][layer].reshape(1, dg)
    o_d, s_gla = call(
        functools.partial(_gla_kernel, kind="gla", layer=layer), [z3, z3, z3, z3, zs3, wg, bg, on[3]],
        [piece(12), piece(13), piece(14), piece(15), zc_spec, full2(wg), full2(bg), full2(on[3])],
        None if st is None else [st[6]], [mat_spec], [mat_shape], None if pv is None else [pv[6]])

    outs = [o[:, :t, :].reshape(n, dg) for o in (o_a, o_b, o_c, o_d)]
    return outs, (m_c, m_n, m_m, s_hgrn, s_gdn, s_conv, s_gla)


def _out_proj_kernel(oa_ref, ob_ref, oc_ref, od_ref, w_ref, x_ref, g_ref, y_ref):
    acc = _dot(oa_ref[...], w_ref[0])
    acc = acc + _dot(ob_ref[...], w_ref[1])
    acc = acc + _dot(oc_ref[...], w_ref[2])
    acc = acc + _dot(od_ref[...], w_ref[3])
    y_ref[...] = x_ref[...] + g_ref[...] * acc


def _out_proj(til, outs, w_out4, x, mod_arr, layer):
    n, d, tm = til.n, til.d, til.tm
    dg = d // N_MIXERS
    tn = min(1024, d)
    o_spec = pl.BlockSpec((tm, dg), lambda i, j: (i, 0))
    if til.per_batch:
        g_spec = pl.BlockSpec((None, 1, tn), lambda i, j: ((i * tm) // til.t, 0, (2 * d) // tn + j))
    else:
        g_spec = pl.BlockSpec((tm, tn), lambda i, j: (i, (2 * d) // tn + j))
    return pl.pallas_call(
        _out_proj_kernel,
        grid=(til.tiles, d // tn),
        in_specs=[o_spec, o_spec, o_spec, o_spec,
                  pl.BlockSpec((None, N_MIXERS, dg, tn), lambda i, j: (layer, 0, 0, j)),
                  pl.BlockSpec((tm, tn), lambda i, j: (i, j)),
                  g_spec],
        out_specs=pl.BlockSpec((tm, tn), lambda i, j: (i, j)),
        out_shape=jax.ShapeDtypeStruct((n, d), F32),
        compiler_params=_cparams(("arbitrary", "arbitrary")),
    )(*outs, w_out4, x, mod_arr)


def _top_desc(s, count):
    rows = float(s.shape[0])
    ri = _iota2(s.shape, 0).astype(F32)
    vals = []
    for r in range(count):
        mx = jnp.max(s, axis=0, keepdims=True)
        vals.append(mx)
        if r < count - 1:
            first = jnp.min(jnp.where(s == mx, ri, rows), axis=0, keepdims=True)
            s = jnp.where(ri == first, -jnp.inf, s)
    return vals


def _sort16_network():
    def merge(lo, hi, r):
        step = r * 2
        if step < hi - lo:
            yield from merge(lo, hi, step)
            yield from merge(lo + r, hi, step)
            yield from [(i, i + r) for i in range(lo + r, hi - r, step)]
        else:
            yield (lo, lo + r)

    def sort(lo, hi):
        if hi - lo >= 1:
            mid = lo + (hi - lo) // 2
            yield from sort(lo, mid)
            yield from sort(mid + 1, hi)
            yield from merge(lo, hi, 1)

    return list(sort(0, PEER_TOPK - 1))


def _bitonic_merge16():
    out, s = [], PEER_TOPK // 2
    while s >= 1:
        out += [(i, i + s) for i in range(PEER_TOPK) if (i & s) == 0]
        s //= 2
    return out


def _top16_sorted(s):
    assert s.shape[0] == PEER_TOPK * SUBLANES
    slabs = [s[SUBLANES * k:SUBLANES * (k + 1), :] for k in range(PEER_TOPK)]

    def exchange(net):
        for i, j in net:
            slabs[i], slabs[j] = jnp.maximum(slabs[i], slabs[j]), jnp.minimum(slabs[i], slabs[j])

    exchange(_sort16_network())
    merge_net = _bitonic_merge16()
    for shift in (4, 2, 1):
        other = [pltpu.roll(x, shift, axis=0) for x in slabs]
        for k in range(PEER_TOPK):
            slabs[k] = jnp.maximum(slabs[k], other[PEER_TOPK - 1 - k])
        exchange(merge_net)
    return [x[0:1, :] for x in slabs]


def _cand_pairs():
    return [(a, b) for a in range(PEER_TOPK) for b in range(PEER_TOPK) if (a + 1) * (b + 1) <= PEER_TOPK]


def _route_kernel(x_ref, nw_ref, sc_ref, sh_ref, wq_ref, key_ref,
                  h2t_ref, s1_ref, s2_ref, e1_ref, e2_ref, tau_ref, cand_ref, h2_scr):
    @pl.when(pl.program_id(1) == 0)
    def _():
        h2 = _rms_mod(x_ref[...], nw_ref[...], sc_ref[...], sh_ref[...])
        h2_scr[...] = h2.astype(BF16)
        h2t_ref[...] = h2.T.astype(BF16)

    half = PEER_KEYS
    heads = s1_ref.shape[0]
    q = _dot(h2_scr[...], wq_ref[...])
    pairs = _cand_pairs()
    for hh in range(heads):
        q1 = q[:, (2 * hh) * half:(2 * hh + 1) * half]
        q2 = q[:, (2 * hh + 1) * half:(2 * hh + 2) * half]
        s1 = _dot_nt(key_ref[hh, 0].astype(BF16), q1.astype(BF16))
        s2 = _dot_nt(key_ref[hh, 1].astype(BF16), q2.astype(BF16))
        v1 = _top16_sorted(s1)
        v2 = _top16_sorted(s2)
        cand_ref[hh] = jnp.full(cand_ref.shape[1:], -jnp.inf, F32)
        for r, (a, b) in enumerate(pairs):
            cand_ref[hh, r:r + 1, :] = v1[a] + v2[b]
        best = _top_desc(cand_ref[hh], PEER_TOPK)
        zsum = jnp.zeros_like(best[0])
        for r in range(PEER_TOPK):
            zsum = zsum + jnp.exp(best[r] - best[0])
        s1_ref[hh] = s1
        s2_ref[hh] = s2
        e1_ref[hh] = jnp.exp(s1 - v1[0]) / zsum
        e2_ref[hh] = jnp.exp(s2 - v2[0])
        tau_ref[hh] = best[PEER_TOPK - 1]


def _route(til, x1, nw, mod_arr, w_q, sub_keys, layer):
    n, d, tm = til.n, til.d, til.tm
    qd = w_q.shape[2] // PEER_HEADS
    n_cand = -(-len(_cand_pairs()) // SUBLANES) * SUBLANES
    hb = ROUTE_HEADS
    tok = pl.BlockSpec((hb, PEER_KEYS, tm), lambda i, h: (h, 0, i))
    tok_shape = jax.ShapeDtypeStruct((PEER_HEADS, PEER_KEYS, n), F32)
    return pl.pallas_call(
        _route_kernel,
        grid=(til.tiles, PEER_HEADS // hb),
        in_specs=[
            pl.BlockSpec((tm, d), lambda i, h: (i, 0)),
            pl.BlockSpec((None, 1, d), lambda i, h: (layer, 0, 0)),
            til.mod_spec(4, 2),
            til.mod_spec(3, 2),
            pl.BlockSpec((None, d, hb * qd), lambda i, h: (layer, 0, h)),
            pl.BlockSpec((None, hb, 2, PEER_KEYS, qd // 2), lambda i, h: (layer, h, 0, 0, 0)),
        ],
        out_specs=[
            pl.BlockSpec((d, tm), lambda i, h: (0, i)),
            tok, tok, tok, tok,
            pl.BlockSpec((hb, 1, tm), lambda i, h: (h, 0, i)),
        ],
        out_shape=[jax.ShapeDtypeStruct((d, n), BF16), tok_shape, tok_shape, tok_shape, tok_shape,
                   jax.ShapeDtypeStruct((PEER_HEADS, 1, n), F32)],
        scratch_shapes=[pltpu.VMEM((hb, n_cand, tm), F32), pltpu.VMEM((tm, d), BF16)],
        compiler_params=_cparams(("arbitrary", "arbitrary")),
    )(x1, nw, mod_arr, mod_arr, w_q, sub_keys)


def _peer_kernel(h2t_ref, u_ref, vt_ref, s1_ref, s2_ref, e1_ref, e2_ref, tau_ref, o_ref, act_scr, p_scr, *, te, n_et):
    e = pl.program_id(1)

    @pl.when(e == 0)
    def _():
        o_ref[...] = jnp.zeros_like(o_ref)
        act_scr[...] = jnp.zeros_like(act_scr)
        p_scr[...] = jnp.zeros_like(p_scr)

    o_ref[...] += _dot(vt_ref[...], p_scr[...])

    groups = te // PEER_KEYS
    tile = jnp.clip(e - 1, 0, n_et - 1)
    for ii in range(groups):
        rows = slice(ii * PEER_KEYS, (ii + 1) * PEER_KEYS)
        act = act_scr[rows, :]
        gel = 0.5 * act * (1.0 + lax.erf(act * (2.0 ** -0.5)))
        row = tile * groups + ii
        acc = jnp.zeros(act.shape, F32)
        for h in range(PEER_HEADS):
            sm = s1_ref[h, pl.ds(row, 1), :] + s2_ref[h]
            sel = jnp.where(sm >= tau_ref[h], e2_ref[h], 0.0)
            acc = acc + sel * e1_ref[h, pl.ds(row, 1), :]
        p_scr[rows, :] = (acc * gel).astype(BF16)

    act_scr[...] = _dot(u_ref[...], h2t_ref[...])


def _peer(til, h2, routing, u_tab, vt_tab, layer):
    n, d, tm = til.n, til.d, til.tm
    ne = u_tab.shape[1]
    te = EXPERT_TILE
    n_et = ne // te
    once = pl.Buffered(1)
    tok = pl.BlockSpec((PEER_HEADS, PEER_KEYS, tm), lambda i, e: (0, 0, i), pipeline_mode=once)
    return pl.pallas_call(
        functools.partial(_peer_kernel, te=te, n_et=n_et),
        grid=(til.tiles, n_et + 2),
        in_specs=[
            pl.BlockSpec((d, tm), lambda i, e: (0, i), pipeline_mode=once),
            pl.BlockSpec((None, te, d), lambda i, e: (layer, jnp.minimum(e, n_et - 1), 0)),
            pl.BlockSpec((None, d, te), lambda i, e: (layer, 0, jnp.clip(e - 2, 0, n_et - 1))),
            tok, tok, tok, tok,
            pl.BlockSpec((PEER_HEADS, 1, tm), lambda i, e: (0, 0, i), pipeline_mode=once),
        ],
        out_specs=pl.BlockSpec((d, tm), lambda i, e: (0, i), pipeline_mode=once),
        out_shape=jax.ShapeDtypeStruct((d, n), F32),
        scratch_shapes=[pltpu.VMEM((te, tm), F32), pltpu.VMEM((te, tm), BF16)],
        compiler_params=_cparams(("arbitrary", "arbitrary")),
    )(h2, u_tab, vt_tab, *routing)


def _residual_kernel(x_ref, pt_ref, g_ref, o_ref):
    o_ref[...] = x_ref[...] + g_ref[...] * pt_ref[...].T


def _residual_norm_kernel(x_ref, pt_ref, g_ref, nw_ref, o_ref):
    x = x_ref[...] + g_ref[...] * pt_ref[...].T
    o_ref[...] = x * lax.rsqrt(jnp.mean(x * x, axis=-1, keepdims=True) + EPS) * nw_ref[...]


def _residual(til, x1, p, mod_arr, final_norm):
    til = _Tiling(til.b, til.t, til.d, tile=256)
    n, d, tm = til.n, til.d, til.tm
    row = pl.BlockSpec((tm, d), lambda i: (i, 0))
    ins = [x1, p, mod_arr]
    specs = [row, pl.BlockSpec((d, tm), lambda i: (0, i)), til.mod_spec(5, 1)]
    body = _residual_kernel
    if final_norm is not None:
        ins.append(final_norm.reshape(1, d))
        specs.append(pl.BlockSpec((1, d), lambda i: (0, 0)))
        body = _residual_norm_kernel
    return pl.pallas_call(
        body, grid=(til.tiles,), in_specs=specs, out_specs=row,
        out_shape=jax.ShapeDtypeStruct((n, d), F32),
        compiler_params=_cparams(("arbitrary",)),
    )(*ins)


def _trunk(x, mod, states, prm):
    b, t, d = x.shape
    depth = mod.shape[0]
    dg = d // N_MIXERS
    til = _Tiling(b, t, d)
    xf = x.reshape(b * t, d)
    new = None
    for l in range(depth):
        mod_arr = til.mod_array(mod[l])
        z_big, zs = _in_proj(til, xf, prm["norm_mix"], mod_arr, prm["w_in_b"], l)
        outs, new = _mixers(b, t, dg, z_big, zs, states, new, prm, l, depth)
        x1 = _out_proj(til, outs, prm["w_out4"], xf, mod_arr, l)
        h2, *routing = _route(til, x1, prm["norm_ffn"], mod_arr, prm["w_q"], prm["peer_sub_keys"], l)
        p = _peer(til, h2, routing, prm["peer_u"], prm["peer_vt"], l)
        xf = _residual(til, x1, p, mod_arr, prm["final_norm"] if l == depth - 1 else None)
    m_c, m_n, m_m, s_hgrn, s_gdn, s_conv, s_gla = new
    nh = dg // HEAD_DIM
    new_states = [m_c, m_n, m_m.reshape(depth, b, nh), s_hgrn, s_gdn,
                  s_conv.reshape(depth, b, CONV_W - 1, 3 * dg), s_gla]
    return xf.reshape(b, t, d), new_states


def _prepare(mlstm_b_i, mlstm_b_f, gdn_a_log, gdn_dt_bias, gla_w_gate, d):
    depth = mlstm_b_i.shape[0]
    dg = d // N_MIXERS
    nh = dg // HEAD_DIM
    n_small = 4 * nh + GLA_RANK
    zeros = lambda k: jnp.zeros((depth, k), F32)
    bias = jnp.concatenate([mlstm_b_i, mlstm_b_f, gdn_dt_bias, zeros(LANES - 3 * nh)], axis=1)
    alog = jnp.concatenate([zeros(2 * nh), gdn_a_log, zeros(LANES - 3 * nh)], axis=1)
    gla_w_pad = jnp.concatenate(
        [jnp.zeros((depth, 4 * nh, dg), F32), gla_w_gate, jnp.zeros((depth, LANES - n_small, dg), F32)], axis=1)
    return dict(
        bias_row=bias.reshape(depth, 1, LANES), bias_col=bias[:, :SMALL_ROWS].reshape(depth, SMALL_ROWS, 1),
        alog_row=alog.reshape(depth, 1, LANES), alog_col=alog[:, :SMALL_ROWS].reshape(depth, SMALL_ROWS, 1),
        gla_w_pad=gla_w_pad)


def kernel(x_prompt, x_sample, c_prompt, c_sample, state_mlstm_C, state_mlstm_n, state_mlstm_m, state_hgrn, state_gdn, state_gdn_conv, state_gla, w_ada, b_ada, norm_mix, norm_ffn, w_in, mlstm_b_i, mlstm_b_f, hgrn_lb, gdn_conv_w, gdn_a_log, gdn_dt_bias, gla_w_gate, gla_b_gate, out_norm, w_out, peer_w_q, peer_sub_keys, peer_u, peer_v, final_norm):
    depth, d = norm_mix.shape
    dg = d // N_MIXERS
    assert d % (N_MIXERS * HEAD_DIM) == 0
    prm = _prepare(mlstm_b_i, mlstm_b_f, gdn_a_log, gdn_dt_bias, gla_w_gate, d)
    prm.update(
        norm_mix=norm_mix.reshape(depth, 1, d), norm_ffn=norm_ffn.reshape(depth, 1, d), hgrn_lb=hgrn_lb,
        gdn_conv_w=gdn_conv_w, gla_b_gate=gla_b_gate, out_norm=out_norm, final_norm=final_norm,
        peer_sub_keys=peer_sub_keys, w_in_b=w_in.astype(BF16),
        w_out4=w_out.astype(BF16).reshape(depth, N_MIXERS, dg, d),
        w_q=peer_w_q.astype(BF16), peer_u=peer_u.astype(BF16),
        peer_vt=jnp.swapaxes(peer_v, 1, 2).astype(BF16))

    bp, bs = c_prompt.shape[0], c_sample.shape[0]
    rows = -(-(bp + bs) // SUBLANES) * SUBLANES
    c_all = jnp.concatenate([c_prompt, c_sample, jnp.zeros((rows - bp - bs, d), F32)], axis=0)
    mod = _ada(c_all, w_ada, b_ada)

    y_prompt, p_states = _trunk(x_prompt, mod[:, :bp], None, prm)
    past = (state_mlstm_C, state_mlstm_n, state_mlstm_m, state_hgrn, state_gdn, state_gdn_conv, state_gla)
    y_sample, s_states = _trunk(x_sample, mod[:, bp:bp + bs], past, prm)
    return (y_prompt, y_sample, *p_states, *s_states)
```

```python
import functools
import math

import jax
import jax.numpy as jnp
from jax import lax
from jax.experimental import pallas as pl
from jax.experimental.pallas import tpu as pltpu

F32 = jnp.float32
BF16 = jnp.bfloat16

HEAD_DIM = 256
N_MIXERS = 4
CONV_W = 4
GLA_RANK = 16
GLA_TAU = 16.0
PEER_HEADS = 8
PEER_KEYS = 128
PEER_TOPK = 16
N_MOD = 6
EPS = 1e-6
NEG_BIG = -1e30
MIN_FORGET = 1e-6

LANES = 128
SUBLANES = 8
SMALL_ROWS = 32
CHUNK = 128
SUB = 8
INV_BLOCK = 16
TOKEN_TILE = 512
EXPERT_TILE = 512
ROUTE_HEADS = 2
VMEM_LIMIT = 56 * 1024 * 1024
PEER_VMEM_LIMIT = 60 * 1024 * 1024


def _cparams(sem):
    return pltpu.CompilerParams(dimension_semantics=sem, vmem_limit_bytes=VMEM_LIMIT)


def _dot(a, b):
    return jnp.dot(a, b, preferred_element_type=F32)


def _dot_nt(a, b):
    return lax.dot_general(a, b, (((1,), (1,)), ((), ())), preferred_element_type=F32)


def _dot_tn(a, b):
    return lax.dot_general(a, b, (((0,), (0,)), ((), ())), preferred_element_type=F32)


def _split2(a):
    hi = a.astype(BF16)
    return hi, (a - hi.astype(F32)).astype(BF16)


def _dot_f32(a, b):
    a_hi, a_lo = _split2(a)
    b_hi, b_lo = _split2(b)
    return _dot(a_hi, b_hi) + (_dot(a_hi, b_lo) + _dot(a_lo, b_hi))


def _sigmoid(x):
    return 1.0 / (1.0 + jnp.exp(-x))


def _silu(x):
    return x * _sigmoid(x)


def _log_sigmoid(x):
    return jnp.minimum(x, 0.0) - jnp.log1p(jnp.exp(-jnp.abs(x)))


def _softplus(x):
    return jnp.maximum(x, 0.0) + jnp.log1p(jnp.exp(-jnp.abs(x)))


def _rms_mod(x, nw, sc, sh):
    y = x * lax.rsqrt(jnp.mean(x * x, axis=-1, keepdims=True) + EPS) * nw
    return y * (1.0 + sc) + sh


def _rms_mod_rows(x_ref, nw_ref, sc_ref, sh_ref, out_ref, rows=LANES):
    tm = x_ref.shape[0]
    step = rows if tm % rows == 0 else tm
    for r in range(0, tm, step):
        sl = slice(r, r + step)
        sc = sc_ref[...] if sc_ref.shape[0] == 1 else sc_ref[sl, :]
        sh = sh_ref[...] if sh_ref.shape[0] == 1 else sh_ref[sl, :]
        out_ref[sl, :] = _rms_mod(x_ref[sl, :], nw_ref[...], sc, sh).astype(out_ref.dtype)


def _merge(h, gate, onorm):
    hn = h * lax.rsqrt(jnp.mean(h * h, axis=-1, keepdims=True) + EPS)
    return (hn * onorm * gate).astype(BF16)


def _iota2(shape, dim):
    return lax.broadcasted_iota(jnp.int32, shape, dim)


def _row_to_col(r):
    n = r.shape[1]
    eye = _iota2((n, n), 0) == _iota2((n, n), 1)
    return jnp.sum(jnp.where(eye, r, 0.0), axis=1, keepdims=True)


def _interleave(units):
    units = list(units)
    while units:
        alive = []
        for u in units:
            try:
                next(u)
                alive.append(u)
            except StopIteration:
                pass
        units = alive


def _cumsum_pair(x_c, x_r, incl, incl_t):
    f_c = jnp.sum(jnp.where(incl, x_r, 0.0), axis=1, keepdims=True)
    f_r = jnp.sum(jnp.where(incl_t, x_c, 0.0), axis=0, keepdims=True)
    return f_c, f_r


def _cumsum_rows(x, tri_b):
    hi = x.astype(BF16)
    r1 = x - hi.astype(F32)
    mid = r1.astype(BF16)
    lo = (r1 - mid.astype(F32)).astype(BF16)
    return _dot(tri_b, hi) + _dot(tri_b, mid) + _dot(tri_b, lo)


def _ada_kernel(c_ref, w_ref, b_ref, o_ref):
    cs = _silu(c_ref[...]).astype(BF16)
    o_ref[...] = _dot(cs, w_ref[...].astype(BF16)) + b_ref[...]


def _ada(c_all, w_ada, b_ada):
    depth, d, n6 = w_ada.shape
    rows = c_all.shape[0]
    tn = 512
    return pl.pallas_call(
        _ada_kernel,
        grid=(depth, n6 // tn),
        in_specs=[
            pl.BlockSpec((rows, d), lambda l, j: (0, 0)),
            pl.BlockSpec((None, d, tn), lambda l, j: (l, 0, j)),
            pl.BlockSpec((None, 1, tn), lambda l, j: (l, 0, j)),
        ],
        out_specs=pl.BlockSpec((None, rows, tn), lambda l, j: (l, 0, j)),
        out_shape=jax.ShapeDtypeStruct((depth, rows, n6), F32),
        compiler_params=_cparams(("arbitrary", "arbitrary")),
    )(c_all, w_ada, b_ada.reshape(depth, 1, n6))


class _Tiling:
    def __init__(self, b, t, d, tile=TOKEN_TILE):
        self.b, self.t, self.d = b, t, d
        self.n = b * t
        self.per_batch = t % LANES == 0
        if self.per_batch:
            self.tm = next(m for m in (tile, 256, LANES) if m <= tile and t % m == 0)
        else:
            self.tm = self.n if self.n <= tile else tile
            assert self.n % self.tm == 0 and self.tm % SUBLANES == 0
        self.tiles = self.n // self.tm

    def mod_array(self, mod_l):
        if self.per_batch:
            return mod_l.reshape(self.b, 1, mod_l.shape[-1])
        return jnp.repeat(mod_l, self.t, axis=0)

    def mod_spec(self, k, grid_rank):
        d, tm, t = self.d, self.tm, self.t
        if self.per_batch:
            if grid_rank == 1:
                return pl.BlockSpec((None, 1, d), lambda i: ((i * tm) // t, 0, k))
            return pl.BlockSpec((None, 1, d), lambda i, j: ((i * tm) // t, 0, k))
        if grid_rank == 1:
            return pl.BlockSpec((tm, d), lambda i: (i, k))
        return pl.BlockSpec((tm, d), lambda i, j: (i, k))


def _in_proj_kernel(x_ref, nw_ref, sc_ref, sh_ref, w_ref, z_ref, zs_ref, h_scr, prev_scr, *, nh, tiles_per_piece):
    j = pl.program_id(1)
    n_tiles = pl.num_programs(1) - 1
    tn = z_ref.shape[1]
    width = tn + LANES
    groups = ((4, 0, 2 * nh), (12, 2 * nh, 2 * nh), (16, 4 * nh, GLA_RANK))

    def emit(head):
        piece = (j - 1) // tiles_per_piece
        amount = jnp.where(piece < 4, 0, width - jnp.where(piece < 12, 2 * nh, 4 * nh))
        zz = jnp.concatenate([prev_scr[...], head], axis=1)
        z_ref[...] = pltpu.roll(zz, amount, axis=1)[:, :tn]
        for hi, s, n_gate in groups:
            @pl.when(j == hi * tiles_per_piece)
            def _():
                lane = _iota2(head.shape, 1)
                zs_ref[...] = jnp.where((lane >= s) & (lane < s + n_gate), head, zs_ref[...])

    @pl.when(j == 0)
    def _():
        _rms_mod_rows(x_ref, nw_ref, sc_ref, sh_ref, h_scr)
        zs_ref[...] = jnp.zeros_like(zs_ref)
        prev_scr[...] = _dot_nt(h_scr[...], w_ref[...])

    @pl.when((j > 0) & (j < n_tiles))
    def _():
        za = _dot_nt(h_scr[...], w_ref[...])
        emit(za[:, :LANES])
        prev_scr[...] = za

    @pl.when(j == n_tiles)
    def _():
        emit(_dot_nt(h_scr[...], w_ref[:LANES, :]))


def _in_proj(til, x, nw, mod_arr, w_in_b, layer):
    if not til.per_batch:
        til = _Tiling(til.b, til.t, til.d, tile=256)
    n, d, tm = til.n, til.d, til.tm
    dg = d // N_MIXERS
    nh = dg // HEAD_DIM
    nbig = 16 * dg
    tn = min(1024, dg)
    assert 4 * nh + GLA_RANK <= LANES and w_in_b.shape[1] == nbig + 4 * nh + GLA_RANK
    return pl.pallas_call(
        functools.partial(_in_proj_kernel, nh=nh, tiles_per_piece=dg // tn),
        grid=(til.tiles, nbig // tn + 1),
        in_specs=[
            pl.BlockSpec((tm, d), lambda i, j: (i, 0)),
            pl.BlockSpec((None, 1, d), lambda i, j: (layer, 0, 0)),
            til.mod_spec(1, 2),
            til.mod_spec(0, 2),
            pl.BlockSpec((None, tn, d), lambda i, j: (layer, j, 0)),
        ],
        out_specs=[
            pl.BlockSpec((tm, tn), lambda i, j: (i, jnp.maximum(j - 1, 0))),
            pl.BlockSpec((tm, LANES), lambda i, j: (i, 0)),
        ],
        out_shape=[jax.ShapeDtypeStruct((n, nbig), F32), jax.ShapeDtypeStruct((n, LANES), F32)],
        scratch_shapes=[pltpu.VMEM((tm, d), BF16), pltpu.VMEM((tm, tn), F32)],
        compiler_params=_cparams(("arbitrary", "arbitrary")),
    )(x, nw, mod_arr, mod_arr, w_in_b)


def _mlstm_kernel(*refs, bb, nh, c, n_valid, has_init, n_alias):
    q_ref, k_ref, v_ref, g_ref, zc_ref, zr_ref, br_ref, bc_ref, on_ref = refs[:9]
    n_in = 9
    if has_init:
        c0_ref, n0_ref, m0_ref = refs[9:12]
        n_in = 12
    o_ref, cs_ref, ns_ref, ms_ref = refs[n_in + n_alias:]

    @pl.when(pl.program_id(1) == 0)
    def _():
        if has_init:
            cs_ref[...] = c0_ref[...]
            ns_ref[...] = n0_ref[...]
            ms_ref[...] = m0_ref[...]
        else:
            cs_ref[...] = jnp.zeros_like(cs_ref)
            ns_ref[...] = jnp.zeros_like(ns_ref)
            ms_ref[...] = jnp.zeros_like(ms_ref)

    hd = HEAD_DIM
    scale = hd ** -0.5
    ti = _iota2((c, c), 0)
    si = _iota2((c, c), 1)
    incl = si <= ti
    incl_t = ti <= si
    valid_c = _iota2((c, 1), 0) < n_valid
    valid_r = _iota2((1, c), 1) < n_valid
    def unit(b, h, zc, zr):
        hs = slice(h * hd, (h + 1) * hd)
        q = q_ref[b, :, hs]
        k = k_ref[b, :, hs] * scale
        v = v_ref[b, :, hs]
        ig_c = zc[:, h:h + 1]
        lf_c = _log_sigmoid(zc[:, nh + h:nh + h + 1])
        ig_r = zr[h:h + 1, :]
        lf_r = _log_sigmoid(zr[nh + h:nh + h + 1, :])
        if n_valid < c:
            ig_c = jnp.where(valid_c, ig_c, NEG_BIG)
            lf_c = jnp.where(valid_c, lf_c, 0.0)
            ig_r = jnp.where(valid_r, ig_r, NEG_BIG)
            lf_r = jnp.where(valid_r, lf_r, 0.0)
        cm = cs_ref[b, h]
        nv = ns_ref[b, h:h + 1, :]
        m0 = ms_ref[b, :, h:h + 1]
        f_c, f_r = _cumsum_pair(lf_c, lf_r, incl, incl_t)
        yield
        raw = f_c - f_r + ig_r
        a = f_c + m0
        m_t = jnp.maximum(a, jnp.max(jnp.where(incl, raw, NEG_BIG), axis=1, keepdims=True))
        p = jnp.where(incl, jnp.exp(jnp.where(incl, raw - m_t, 0.0)), 0.0)
        qb = q.astype(BF16)
        kb = k.astype(BF16)
        vb = v.astype(BF16)
        yield
        s = _dot_nt(qb, kb) * p
        inter = jnp.exp(a - m_t)
        qc = _dot(qb, cm.astype(BF16))
        yield
        num = inter * qc + _dot(s.astype(BF16), vb)
        den = inter * jnp.sum(q * nv, axis=1, keepdims=True) + jnp.sum(s, axis=1, keepdims=True)
        m_end = m_t[c - 1:c, :]
        w_end = jnp.exp(f_c[c - 1:c, :] - f_c + ig_c - m_end)
        dec = jnp.exp(a[c - 1:c, :] - m_end)
        kw = w_end * k
        yield
        hh = num / jnp.maximum(jnp.abs(den), jnp.exp(-m_t))
        cs_ref[b, h] = dec * cm + _dot_tn(kw.astype(BF16), vb)
        ns_ref[b, h:h + 1, :] = dec * nv + jnp.sum(kw, axis=0, keepdims=True)
        ms_ref[b, :, h:h + 1] = m_end
        o_ref[b, :, hs] = _merge(hh, _sigmoid(g_ref[b, :, hs]), on_ref[:, hs])

    for b in range(bb):
        zc = zc_ref[b] + br_ref[...]
        zr = zr_ref[b] + bc_ref[...]
        _interleave([unit(b, h, zc, zr) for h in range(nh)])


def _gla_unit(q, k, v, lg, s_mat, tri_b, emit):
    c, hd = q.shape
    sb = min(SUB, c)
    nb = c // sb
    g = _cumsum_rows(lg, tri_b)
    yield
    o = _dot((q * jnp.exp(g)).astype(BF16), s_mat.astype(BF16))

    if nb > 1:
        qparts, kparts = [], []
        for j in range(nb - 1):
            r1 = (j + 1) * sb
            g_end = g[r1 - 1:r1, :]
            qj = q[r1:, :] * jnp.exp(g[r1:, :] - g_end)
            kj = k[j * sb:r1, :] * jnp.exp(g_end - g[j * sb:r1, :])
            qparts.append(jnp.concatenate([jnp.zeros((r1, hd), F32), qj], axis=0).astype(BF16))
            pieces = [kj]
            if j > 0:
                pieces.insert(0, jnp.zeros((j * sb, hd), F32))
            pieces.append(jnp.zeros((c - r1, hd), F32))
            kparts.append(jnp.concatenate(pieces, axis=0).astype(BF16))
        a_off = _dot_nt(jnp.concatenate(qparts, axis=1), jnp.concatenate(kparts, axis=1))
    else:
        a_off = jnp.zeros((c, c), F32)
    yield

    lane = _iota2((sb, c), 1)
    trow = _iota2((sb, 1), 0)
    strips = []
    for i in range(nb):
        r0 = i * sb
        qi, ki, gi = q[r0:r0 + sb, :], k[r0:r0 + sb, :], g[r0:r0 + sb, :]
        strip = a_off[r0:r0 + sb, :]
        for s in range(sb):
            msk = trow >= s
            w = jnp.where(msk, jnp.exp(jnp.where(msk, gi - gi[s:s + 1, :], 0.0)), 0.0)
            col = jnp.sum(qi * ki[s:s + 1, :] * w, axis=1, keepdims=True)
            strip = jnp.where(lane == r0 + s, col, strip)
        strips.append(strip)
        yield
    a = strips[0] if nb == 1 else jnp.concatenate(strips, axis=0)
    vb = v.astype(BF16)
    o = o + _dot(a.astype(BF16), vb)

    g_end = g[c - 1:c, :]
    kt = (k * jnp.exp(g_end - g)).astype(BF16)
    yield
    emit(o, _row_to_col(jnp.exp(g_end)) * s_mat + _dot_tn(kt, vb))


def _gla_kernel(*refs, bb, nh, c, n_valid, has_init, n_alias, kind, layer):
    if kind == "hgrn":
        q_ref, k_ref, v_ref, g_ref, par_ref, on_ref = refs[:6]
        rest = refs[6:]
    else:
        q_ref, k_ref, v_ref, g_ref, zc_ref, wg_ref, bg_ref, on_ref = refs[:8]
        rest = refs[8:]
    if has_init:
        s0_ref = rest[0]
        rest = rest[1:]
    o_ref, st_ref = rest[n_alias:]

    @pl.when(pl.program_id(1) == 0)
    def _():
        if has_init:
            st_ref[...] = s0_ref[...]
        else:
            st_ref[...] = jnp.zeros_like(st_ref)

    hd = HEAD_DIM
    tri_b = (_iota2((c, c), 1) <= _iota2((c, c), 0)).astype(BF16)
    valid_c = _iota2((c, 1), 0) < n_valid
    if kind == "hgrn":
        lbp = par_ref[...]
        ex = jnp.exp(lbp - jnp.max(lbp, axis=0, keepdims=True))
        lbs = ex / jnp.sum(ex, axis=0, keepdims=True)
        lb = jnp.zeros_like(lbs[0:1, :])
        for j in range(1, layer + 1):
            lb = lb + lbs[j:j + 1, :]
    def finish(b, h, hs):
        def emit(o, s_new):
            st_ref[b, h] = s_new
            o_ref[b, :, hs] = _merge(o, _silu(g_ref[b, :, hs]), on_ref[:, hs])
        return emit

    for b in range(bb):
        if kind == "gla":
            gate_in = _dot(zc_ref[b].astype(BF16), wg_ref[...].astype(BF16)) + bg_ref[...]
        units = []
        for h in range(nh):
            hs = slice(h * hd, (h + 1) * hd)
            if kind == "hgrn":
                fg = k_ref[b, :, hs]
                lbh = lb[:, hs]
                f = lbh + (1.0 - lbh) * _sigmoid(fg)
                lg = jnp.log(jnp.maximum(f, MIN_FORGET))
                k = (1.0 - lbh) * _sigmoid(-fg)
                q = _silu(q_ref[b, :, hs])
            else:
                lg = _log_sigmoid(gate_in[:, hs]) / GLA_TAU
                k = k_ref[b, :, hs]
                q = q_ref[b, :, hs] * (hd ** -0.5)
            v = v_ref[b, :, hs]
            if n_valid < c:
                lg = jnp.where(valid_c, lg, 0.0)
                k = jnp.where(valid_c, k, 0.0)
            units.append(_gla_unit(q, k, v, lg, st_ref[b, h], tri_b, finish(b, h, hs)))
        _interleave(units)


def _gdn_kernel(*refs, bb, nh, c, n_valid, has_init, n_alias):
    (q_ref, k_ref, v_ref, g_ref, zc_ref, zr_ref, br_ref, bc_ref, ar_ref, ac_ref, cw_ref, on_ref) = refs[:12]
    n_in = 12
    if has_init:
        s0_ref, cv0_ref = refs[12:14]
        n_in = 14
    o_ref, st_ref, cvo_ref, tail_ref = refs[n_in + n_alias:]
    t_id = pl.program_id(1)
    n_chunks = pl.num_programs(1)
    tail_rows = SUBLANES
    n_buf = CONV_W - 1

    @pl.when(t_id == 0)
    def _():
        tail_ref[...] = jnp.zeros_like(tail_ref)
        if has_init:
            st_ref[...] = s0_ref[...]
            for b in range(bb):
                for pc in range(3):
                    tail_ref[b, pc, tail_rows - n_buf:tail_rows, :] = cv0_ref[b, :, pc, :]
        else:
            st_ref[...] = jnp.zeros_like(st_ref)

    hd = HEAD_DIM
    scale = hd ** -0.5
    ti = _iota2((c, c), 0)
    si = _iota2((c, c), 1)
    incl = si <= ti
    incl_t = ti <= si
    strict = si < ti
    eye = (si == ti).astype(F32)
    valid_c = _iota2((c, 1), 0) < n_valid
    valid_r = _iota2((1, c), 1) < n_valid
    raw_refs = (q_ref, k_ref, v_ref)
    for b in range(bb):
        conv = []
        for pc in range(3):
            u = raw_refs[pc][b]
            ext = jnp.concatenate([tail_ref[b, pc], u], axis=0)
            acc = u * cw_ref[CONV_W - 1:CONV_W, pc, :]
            for j in range(1, CONV_W):
                shifted = pltpu.roll(ext, j, axis=0)[tail_rows:tail_rows + c, :]
                acc = acc + shifted * cw_ref[CONV_W - 1 - j:CONV_W - j, pc, :]
            conv.append(_silu(acc))
            tail_ref[b, pc] = u[c - tail_rows:c, :]

        @pl.when(t_id == n_chunks - 1)
        def _():
            for pc in range(3):
                cvo_ref[b, :, pc, :] = raw_refs[pc][b, n_valid - n_buf:n_valid, :]

        zc = zc_ref[b] + br_ref[...]
        zr = zr_ref[b] + bc_ref[...]
        units = []
        for h in range(nh):
            hs = slice(h * hd, (h + 1) * hd)
            q = conv[0][:, hs]
            k = conv[1][:, hs]
            v = conv[2][:, hs]
            q = q * lax.rsqrt(jnp.sum(q * q, axis=1, keepdims=True) + EPS) * scale
            k = k * lax.rsqrt(jnp.sum(k * k, axis=1, keepdims=True) + EPS)
            ca, cb = 2 * nh + h, 3 * nh + h
            lg_c = -jnp.exp(ar_ref[:, ca:ca + 1]) * _softplus(zc[:, ca:ca + 1])
            lg_r = -jnp.exp(ac_ref[ca:ca + 1, :]) * _softplus(zr[ca:ca + 1, :])
            beta = _sigmoid(zc[:, cb:cb + 1])
            if n_valid < c:
                lg_c = jnp.where(valid_c, lg_c, 0.0)
                lg_r = jnp.where(valid_r, lg_r, 0.0)
                beta = jnp.where(valid_c, beta, 0.0)
            s_mat = st_ref[b, h]

            g_c, g_r = _cumsum_pair(lg_c, lg_r, incl, incl_t)
            eg = jnp.exp(g_c)
            rel = jnp.where(incl, jnp.exp(jnp.where(incl, g_c - g_r, 0.0)), 0.0)
            qb = q.astype(BF16)
            kb = k.astype(BF16)
            kq_s = _dot(jnp.concatenate([kb, qb], axis=0), s_mat.astype(BF16))
            m = jnp.where(strict, beta * rel * _dot_nt(kb, kb), 0.0)
            rhs = beta * (v - eg * kq_s[:c, :])
            qk = (_dot_nt(qb, kb) * rel).astype(BF16)
            units.append(dict(hs=hs, k=k, s_mat=s_mat, g_c=g_c, eg=eg, m=m, rhs=rhs, qk=qk, qs=kq_s[c:, :]))

        b0 = min(INV_BLOCK, c)
        md = [jnp.where((ti // b0) == (si // b0), un["m"], 0.0) for un in units]
        tinv = [eye - x for x in md]
        pw = [_dot_f32(x, x) for x in md]
        n_it = int(math.log2(b0)) - 1
        for it in range(n_it):
            tinv = [t + _dot_f32(t, p) for t, p in zip(tinv, pw)]
            if it < n_it - 1:
                pw = [_dot_f32(p, p) for p in pw]
        blk = b0
        while blk < c:
            below = ((ti // (2 * blk)) == (si // (2 * blk))) & ((ti // blk) != (si // blk))
            tinv = [t - _dot_f32(_dot_f32(t, jnp.where(below, un["m"], 0.0)), t) for t, un in zip(tinv, units)]
            blk *= 2

        for un, t in zip(units, tinv):
            ub = _dot_f32(t, un["rhs"]).astype(BF16)
            o = un["eg"] * un["qs"] + _dot(un["qk"], ub)
            g_end = un["g_c"][c - 1:c, :]
            kd = (jnp.exp(g_end - un["g_c"]) * un["k"]).astype(BF16)
            h = un["hs"].start // hd
            st_ref[b, h] = jnp.exp(g_end) * un["s_mat"] + _dot_tn(kd, ub)
            o_ref[b, :, un["hs"]] = _merge(o, _silu(g_ref[b, :, un["hs"]]), on_ref[:, un["hs"]])


def _mixers(b, t, dg, z_big, zs, states, prev, prm, layer, depth):
    nh = dg // HEAD_DIM
    hd = HEAD_DIM
    n = b * t
    tp = -(-t // SUBLANES) * SUBLANES
    c = min(CHUNK, tp)
    assert tp % c == 0 and (tp == t or tp == c) and t >= CONV_W - 1 and 4 * nh + GLA_RANK <= SMALL_ROWS
    nc = tp // c
    n_valid = c - (tp - t)
    has_init = states is not None
    bb = 4 if (has_init and b % 4 == 0) else 1

    z3 = z_big.reshape(b, t, 16 * dg)
    zs3 = zs.reshape(b, t, LANES)
    if tp != t:
        z3 = jnp.pad(z3, ((0, 0), (0, tp - t), (0, 0)))
        zs3 = jnp.pad(zs3, ((0, 0), (0, tp - t), (0, 0)))
    zr4 = jnp.swapaxes(zs3[:, :, :SMALL_ROWS].reshape(b, nc, c, SMALL_ROWS), 2, 3)

    grid = (b // bb, nc)
    piece = lambda p: pl.BlockSpec((bb, c, dg), lambda i, j: (i, j, p))
    zc_spec = pl.BlockSpec((bb, c, LANES), lambda i, j: (i, j, 0))
    zr_spec = pl.BlockSpec((bb, None, SMALL_ROWS, c), lambda i, j: (i, j, 0, 0))
    full2 = lambda a: pl.BlockSpec(a.shape, lambda i, j: (0,) * a.ndim)
    mat_spec = pl.BlockSpec((None, bb, nh, hd, hd), lambda i, j: (layer, i, 0, 0, 0))
    n_spec = pl.BlockSpec((None, bb, nh, hd), lambda i, j: (layer, i, 0, 0))
    m_spec = pl.BlockSpec((None, bb, 1, nh), lambda i, j: (layer, i, 0, 0))
    cv_spec = pl.BlockSpec((None, bb, CONV_W - 1, 3, dg), lambda i, j: (layer, i, 0, 0, 0))
    mat_shape = jax.ShapeDtypeStruct((depth, b, nh, hd, hd), F32)
    n_shape = jax.ShapeDtypeStruct((depth, b, nh, hd), F32)
    m_shape = jax.ShapeDtypeStruct((depth, b, 1, nh), F32)
    cv_shape = jax.ShapeDtypeStruct((depth, b, CONV_W - 1, 3, dg), F32)
    any_spec = pl.BlockSpec(memory_space=pl.ANY)
    o_spec = pl.BlockSpec((bb, c, dg), lambda i, j: (i, j, 0))
    o_shape = jax.ShapeDtypeStruct((b, tp, dg), BF16)
    cp = _cparams(("arbitrary", "arbitrary"))
    on = prm["out_norm"][layer].reshape(N_MIXERS, 1, dg)
    br, bc, ar, ac = prm["bias_row"][layer], prm["bias_col"][layer], prm["alog_row"][layer], prm["alog_col"][layer]

    def call(body, ins, specs, init, state_specs, state_shapes, prev_arrays, scratch=()):
        ins, specs = list(ins), list(specs)
        if has_init:
            ins += init
            specs += state_specs
        n_alias = 0 if prev_arrays is None else len(prev_arrays)
        aliases = {}
        if n_alias:
            aliases = {len(ins) + k: 1 + k for k in range(n_alias)}
            ins += list(prev_arrays)
            specs += [any_spec] * n_alias
        return pl.pallas_call(
            functools.partial(body, bb=bb, nh=nh, c=c, n_valid=n_valid, has_init=has_init, n_alias=n_alias),
            grid=grid, in_specs=specs, out_specs=[o_spec] + list(state_specs),
            out_shape=[o_shape] + list(state_shapes), scratch_shapes=list(scratch),
            input_output_aliases=aliases, compiler_params=cp,
        )(*ins)

    st = states
    pv = prev
    o_a, m_c, m_n, m_m = call(
        _mlstm_kernel, [z3, z3, z3, z3, zs3, zr4, br, bc, on[0]],
        [piece(0), piece(1), piece(2), piece(3), zc_spec, zr_spec, full2(br), full2(bc), full2(on[0])],
        None if st is None else [st[0], st[1], st[2].reshape(depth, b, 1, nh)],
        [mat_spec, n_spec, m_spec], [mat_shape, n_shape, m_shape],
        None if pv is None else [pv[0], pv[1], pv[2]])

    lbp = prm["hgrn_lb"]
    o_b, s_hgrn = call(
        functools.partial(_gla_kernel, kind="hgrn", layer=layer), [z3, z3, z3, z3, lbp, on[1]],
        [piece(4), piece(5), piece(6), piece(7), full2(lbp), full2(on[1])],
        None if st is None else [st[3]], [mat_spec], [mat_shape], None if pv is None else [pv[3]])

    cw = prm["gdn_conv_w"][layer].reshape(CONV_W, 3, dg)
    o_c, s_gdn, s_conv = call(
        _gdn_kernel, [z3, z3, z3, z3, zs3, zr4, br, bc, ar, ac, cw, on[2]],
        [piece(8), piece(9), piece(10), piece(11), zc_spec, zr_spec, full2(br), full2(bc), full2(ar),
         full2(ac), full2(cw), full2(on[2])],
        None if st is None else [st[4], st[5].reshape(depth, b, CONV_W - 1, 3, dg)],
        [mat_spec, cv_spec], [mat_shape, cv_shape], None if pv is None else [pv[4], pv[5]],
        scratch=[pltpu.VMEM((bb, 3, SUBLANES, dg), F32)])

    wg, bg = prm["gla_w_pad"][layer], prm["gla_b_gate"][layer].reshape(1, dg)
    o_d, s_gla = call(
        functools.partial(_gla_kernel, kind="gla", layer=layer), [z3, z3, z3, z3, zs3, wg, bg, on[3]],
        [piece(12), piece(13), piece(14), piece(15), zc_spec, full2(wg), full2(bg), full2(on[3])],
        None if st is None else [st[6]], [mat_spec], [mat_shape], None if pv is None else [pv[6]])

    outs = [o[:, :t, :].reshape(n, dg) for o in (o_a, o_b, o_c, o_d)]
    return outs, (m_c, m_n, m_m, s_hgrn, s_gdn, s_conv, s_gla)


def _out_proj_kernel(oa_ref, ob_ref, oc_ref, od_ref, w_ref, x_ref, g_ref, y_ref):
    acc = _dot(oa_ref[...], w_ref[0])
    acc = acc + _dot(ob_ref[...], w_ref[1])
    acc = acc + _dot(oc_ref[...], w_ref[2])
    acc = acc + _dot(od_ref[...], w_ref[3])
    y_ref[...] = x_ref[...] + g_ref[...] * acc


def _out_proj(til, outs, w_out4, x, mod_arr, layer):
    n, d, tm = til.n, til.d, til.tm
    dg = d // N_MIXERS
    tn = min(1024, d)
    o_spec = pl.BlockSpec((tm, dg), lambda i, j: (i, 0))
    if til.per_batch:
        g_spec = pl.BlockSpec((None, 1, tn), lambda i, j: ((i * tm) // til.t, 0, (2 * d) // tn + j))
    else:
        g_spec = pl.BlockSpec((tm, tn), lambda i, j: (i, (2 * d) // tn + j))
    return pl.pallas_call(
        _out_proj_kernel,
        grid=(til.tiles, d // tn),
        in_specs=[o_spec, o_spec, o_spec, o_spec,
                  pl.BlockSpec((None, N_MIXERS, dg, tn), lambda i, j: (layer, 0, 0, j)),
                  pl.BlockSpec((tm, tn), lambda i, j: (i, j)),
                  g_spec],
        out_specs=pl.BlockSpec((tm, tn), lambda i, j: (i, j)),
        out_shape=jax.ShapeDtypeStruct((n, d), F32),
        compiler_params=_cparams(("arbitrary", "arbitrary")),
    )(*outs, w_out4, x, mod_arr)


def _top_desc(s, count):
    rows = float(s.shape[0])
    ri = _iota2(s.shape, 0).astype(F32)
    vals = []
    for r in range(count):
        mx = jnp.max(s, axis=0, keepdims=True)
        vals.append(mx)
        if r < count - 1:
            first = jnp.min(jnp.where(s == mx, ri, rows), axis=0, keepdims=True)
            s = jnp.where(ri == first, -jnp.inf, s)
    return vals


def _sort16_network():
    def merge(lo, hi, r):
        step = r * 2
        if step < hi - lo:
            yield from merge(lo, hi, step)
            yield from merge(lo + r, hi, step)
            yield from [(i, i + r) for i in range(lo + r, hi - r, step)]
        else:
            yield (lo, lo + r)

    def sort(lo, hi):
        if hi - lo >= 1:
            mid = lo + (hi - lo) // 2
            yield from sort(lo, mid)
            yield from sort(mid + 1, hi)
            yield from merge(lo, hi, 1)

    return list(sort(0, PEER_TOPK - 1))


def _bitonic_merge16():
    out, s = [], PEER_TOPK // 2
    while s >= 1:
        out += [(i, i + s) for i in range(PEER_TOPK) if (i & s) == 0]
        s //= 2
    return out


def _top16_sorted(s):
    assert s.shape[0] == PEER_TOPK * SUBLANES
    slabs = [s[SUBLANES * k:SUBLANES * (k + 1), :] for k in range(PEER_TOPK)]

    def exchange(net):
        for i, j in net:
            slabs[i], slabs[j] = jnp.maximum(slabs[i], slabs[j]), jnp.minimum(slabs[i], slabs[j])

    exchange(_sort16_network())
    merge_net = _bitonic_merge16()
    for shift in (4, 2, 1):
        other = [pltpu.roll(x, shift, axis=0) for x in slabs]
        for k in range(PEER_TOPK):
            slabs[k] = jnp.maximum(slabs[k], other[PEER_TOPK - 1 - k])
        exchange(merge_net)
    return [x[0:1, :] for x in slabs]


def _cand_pairs():
    return [(a, b) for a in range(PEER_TOPK) for b in range(PEER_TOPK) if (a + 1) * (b + 1) <= PEER_TOPK]


def _route_kernel(x_ref, nw_ref, sc_ref, sh_ref, wq_ref, key_ref,
                  h2t_ref, s1_ref, s2_ref, e1_ref, e2_ref, tau_ref, cand_ref, h2_scr):
    @pl.when(pl.program_id(1) == 0)
    def _():
        h2 = _rms_mod(x_ref[...], nw_ref[...], sc_ref[...], sh_ref[...])
        h2_scr[...] = h2.astype(BF16)
        h2t_ref[...] = h2.T.astype(BF16)

    half = PEER_KEYS
    heads = s1_ref.shape[0]
    q = _dot(h2_scr[...], wq_ref[...])
    pairs = _cand_pairs()
    for hh in range(heads):
        q1 = q[:, (2 * hh) * half:(2 * hh + 1) * half]
        q2 = q[:, (2 * hh + 1) * half:(2 * hh + 2) * half]
        s1 = _dot_nt(key_ref[hh, 0].astype(BF16), q1.astype(BF16))
        s2 = _dot_nt(key_ref[hh, 1].astype(BF16), q2.astype(BF16))
        v1 = _top16_sorted(s1)
        v2 = _top16_sorted(s2)
        cand_ref[hh] = jnp.full(cand_ref.shape[1:], -jnp.inf, F32)
        for r, (a, b) in enumerate(pairs):
            cand_ref[hh, r:r + 1, :] = v1[a] + v2[b]
        best = _top_desc(cand_ref[hh], PEER_TOPK)
        zsum = jnp.zeros_like(best[0])
        for r in range(PEER_TOPK):
            zsum = zsum + jnp.exp(best[r] - best[0])
        s1_ref[hh] = s1
        s2_ref[hh] = s2
        e1_ref[hh] = jnp.exp(s1 - v1[0]) / zsum
        e2_ref[hh] = jnp.exp(s2 - v2[0])
        tau_ref[hh] = best[PEER_TOPK - 1]


def _route(til, x1, nw, mod_arr, w_q, sub_keys, layer):
    n, d, tm = til.n, til.d, til.tm
    qd = w_q.shape[2] // PEER_HEADS
    n_cand = -(-len(_cand_pairs()) // SUBLANES) * SUBLANES
    hb = ROUTE_HEADS
    tok = pl.BlockSpec((hb, PEER_KEYS, tm), lambda i, h: (h, 0, i))
    tok_shape = jax.ShapeDtypeStruct((PEER_HEADS, PEER_KEYS, n), F32)
    return pl.pallas_call(
        _route_kernel,
        grid=(til.tiles, PEER_HEADS // hb),
        in_specs=[
            pl.BlockSpec((tm, d), lambda i, h: (i, 0)),
            pl.BlockSpec((None, 1, d), lambda i, h: (layer, 0, 0)),
            til.mod_spec(4, 2),
            til.mod_spec(3, 2),
            pl.BlockSpec((None, d, hb * qd), lambda i, h: (layer, 0, h)),
            pl.BlockSpec((None, hb, 2, PEER_KEYS, qd // 2), lambda i, h: (layer, h, 0, 0, 0)),
        ],
        out_specs=[
            pl.BlockSpec((d, tm), lambda i, h: (0, i)),
            tok, tok, tok, tok,
            pl.BlockSpec((hb, 1, tm), lambda i, h: (h, 0, i)),
        ],
        out_shape=[jax.ShapeDtypeStruct((d, n), BF16), tok_shape, tok_shape, tok_shape, tok_shape,
                   jax.ShapeDtypeStruct((PEER_HEADS, 1, n), F32)],
        scratch_shapes=[pltpu.VMEM((hb, n_cand, tm), F32), pltpu.VMEM((tm, d), BF16)],
        compiler_params=_cparams(("arbitrary", "arbitrary")),
    )(x1, nw, mod_arr, mod_arr, w_q, sub_keys)


def _peer_kernel(h2t_ref, u_ref, vt_ref, s1_ref, s2_ref, e1_ref, e2_ref, tau_ref, o_ref, act_scr, p_scr, *, te, n_et):
    e = pl.program_id(1)

    @pl.when(e == 0)
    def _():
        o_ref[...] = jnp.zeros_like(o_ref)
        act_scr[...] = jnp.zeros_like(act_scr)
        p_scr[...] = jnp.zeros_like(p_scr)

    o_ref[...] += _dot(vt_ref[...], p_scr[...])

    groups = te // PEER_KEYS
    tile = jnp.clip(e - 1, 0, n_et - 1)
    for ii in range(groups):
        rows = slice(ii * PEER_KEYS, (ii + 1) * PEER_KEYS)
        act = act_scr[rows, :]
        gel = 0.5 * act * (1.0 + lax.erf(act * (2.0 ** -0.5)))
        row = tile * groups + ii
        acc = jnp.zeros(act.shape, F32)
        for h in range(PEER_HEADS):
            sm = s1_ref[h, pl.ds(row, 1), :] + s2_ref[h]
            sel = jnp.where(sm >= tau_ref[h], e2_ref[h], 0.0)
            acc = acc + sel * e1_ref[h, pl.ds(row, 1), :]
        p_scr[rows, :] = (acc * gel).astype(BF16)

    act_scr[...] = _dot(u_ref[...].astype(BF16), h2t_ref[...])


def _peer(til, h2, routing, u_tab, vt_tab, layer):
    n, d, tm = til.n, til.d, til.tm
    ne = u_tab.shape[1]
    te = EXPERT_TILE
    n_et = ne // te
    once = pl.Buffered(1)
    tok = pl.BlockSpec((PEER_HEADS, PEER_KEYS, tm), lambda i, e: (0, 0, i), pipeline_mode=once)
    return pl.pallas_call(
        functools.partial(_peer_kernel, te=te, n_et=n_et),
        grid=(til.tiles, n_et + 2),
        in_specs=[
            pl.BlockSpec((d, tm), lambda i, e: (0, i), pipeline_mode=once),
            pl.BlockSpec((None, te, d), lambda i, e: (layer, jnp.minimum(e, n_et - 1), 0)),
            pl.BlockSpec((None, d, te), lambda i, e: (layer, 0, jnp.clip(e - 2, 0, n_et - 1))),
            tok, tok, tok, tok,
            pl.BlockSpec((PEER_HEADS, 1, tm), lambda i, e: (0, 0, i), pipeline_mode=once),
        ],
        out_specs=pl.BlockSpec((d, tm), lambda i, e: (0, i), pipeline_mode=once),
        out_shape=jax.ShapeDtypeStruct((d, n), F32),
        scratch_shapes=[pltpu.VMEM((te, tm), F32), pltpu.VMEM((te, tm), BF16)],
        compiler_params=pltpu.CompilerParams(dimension_semantics=("arbitrary", "arbitrary"),
                                             vmem_limit_bytes=PEER_VMEM_LIMIT),
    )(h2, u_tab, vt_tab, *routing)


def _residual_kernel(x_ref, pt_ref, g_ref, o_ref):
    o_ref[...] = x_ref[...] + g_ref[...] * pt_ref[...].T


def _residual_norm_kernel(x_ref, pt_ref, g_ref, nw_ref, o_ref):
    x = x_ref[...] + g_ref[...] * pt_ref[...].T
    o_ref[...] = x * lax.rsqrt(jnp.mean(x * x, axis=-1, keepdims=True) + EPS) * nw_ref[...]


def _residual(til, x1, p, mod_arr, final_norm):
    til = _Tiling(til.b, til.t, til.d, tile=256)
    n, d, tm = til.n, til.d, til.tm
    row = pl.BlockSpec((tm, d), lambda i: (i, 0))
    ins = [x1, p, mod_arr]
    specs = [row, pl.BlockSpec((d, tm), lambda i: (0, i)), til.mod_spec(5, 1)]
    body = _residual_kernel
    if final_norm is not None:
        ins.append(final_norm.reshape(1, d))
        specs.append(pl.BlockSpec((1, d), lambda i: (0, 0)))
        body = _residual_norm_kernel
    return pl.pallas_call(
        body, grid=(til.tiles,), in_specs=specs, out_specs=row,
        out_shape=jax.ShapeDtypeStruct((n, d), F32),
        compiler_params=_cparams(("arbitrary",)),
    )(*ins)


def _trunk(x, mod, states, prm):
    b, t, d = x.shape
    depth = mod.shape[0]
    dg = d // N_MIXERS
    til = _Tiling(b, t, d)
    xf = x.reshape(b * t, d)
    new = None
    for l in range(depth):
        mod_arr = til.mod_array(mod[l])
        z_big, zs = _in_proj(til, xf, prm["norm_mix"], mod_arr, prm["w_in_b"], l)
        outs, new = _mixers(b, t, dg, z_big, zs, states, new, prm, l, depth)
        x1 = _out_proj(til, outs, prm["w_out4"], xf, mod_arr, l)
        h2, *routing = _route(til, x1, prm["norm_ffn"], mod_arr, prm["w_q"], prm["peer_sub_keys"], l)
        p = _peer(til, h2, routing, prm["peer_u"], prm["peer_vt"], l)
        xf = _residual(til, x1, p, mod_arr, prm["final_norm"] if l == depth - 1 else None)
    m_c, m_n, m_m, s_hgrn, s_gdn, s_conv, s_gla = new
    nh = dg // HEAD_DIM
    new_states = [m_c, m_n, m_m.reshape(depth, b, nh), s_hgrn, s_gdn,
                  s_conv.reshape(depth, b, CONV_W - 1, 3 * dg), s_gla]
    return xf.reshape(b, t, d), new_states


def _prepare(mlstm_b_i, mlstm_b_f, gdn_a_log, gdn_dt_bias, gla_w_gate, d):
    depth = mlstm_b_i.shape[0]
    dg = d // N_MIXERS
    nh = dg // HEAD_DIM
    n_small = 4 * nh + GLA_RANK
    zeros = lambda k: jnp.zeros((depth, k), F32)
    bias = jnp.concatenate([mlstm_b_i, mlstm_b_f, gdn_dt_bias, zeros(LANES - 3 * nh)], axis=1)
    alog = jnp.concatenate([zeros(2 * nh), gdn_a_log, zeros(LANES - 3 * nh)], axis=1)
    gla_w_pad = jnp.concatenate(
        [jnp.zeros((depth, 4 * nh, dg), F32), gla_w_gate, jnp.zeros((depth, LANES - n_small, dg), F32)], axis=1)
    return dict(
        bias_row=bias.reshape(depth, 1, LANES), bias_col=bias[:, :SMALL_ROWS].reshape(depth, SMALL_ROWS, 1),
        alog_row=alog.reshape(depth, 1, LANES), alog_col=alog[:, :SMALL_ROWS].reshape(depth, SMALL_ROWS, 1),
        gla_w_pad=gla_w_pad)


def kernel(x_prompt, x_sample, c_prompt, c_sample, state_mlstm_C, state_mlstm_n, state_mlstm_m, state_hgrn, state_gdn, state_gdn_conv, state_gla, w_ada, b_ada, norm_mix, norm_ffn, w_in, mlstm_b_i, mlstm_b_f, hgrn_lb, gdn_conv_w, gdn_a_log, gdn_dt_bias, gla_w_gate, gla_b_gate, out_norm, w_out, peer_w_q, peer_sub_keys, peer_u, peer_v, final_norm):
    depth, d = norm_mix.shape
    dg = d // N_MIXERS
    assert d % (N_MIXERS * HEAD_DIM) == 0
    prm = _prepare(mlstm_b_i, mlstm_b_f, gdn_a_log, gdn_dt_bias, gla_w_gate, d)
    prm.update(
        norm_mix=norm_mix.reshape(depth, 1, d), norm_ffn=norm_ffn.reshape(depth, 1, d), hgrn_lb=hgrn_lb,
        gdn_conv_w=gdn_conv_w, gla_b_gate=gla_b_gate, out_norm=out_norm, final_norm=final_norm,
        peer_sub_keys=peer_sub_keys, w_in_b=jnp.swapaxes(w_in, 1, 2).astype(BF16),
        w_out4=w_out.astype(BF16).reshape(depth, N_MIXERS, dg, d),
        w_q=peer_w_q.astype(BF16), peer_u=peer_u,
        peer_vt=jnp.swapaxes(peer_v, 1, 2).astype(BF16))

    bp, bs = c_prompt.shape[0], c_sample.shape[0]
    rows = -(-(bp + bs) // SUBLANES) * SUBLANES
    c_all = jnp.concatenate([c_prompt, c_sample, jnp.zeros((rows - bp - bs, d), F32)], axis=0)
    mod = _ada(c_all, w_ada, b_ada)

    y_prompt, p_states = _trunk(x_prompt, mod[:, :bp], None, prm)
    past = (state_mlstm_C, state_mlstm_n, state_mlstm_m, state_hgrn, state_gdn, state_gdn_conv, state_gla)
    y_sample, s_states = _trunk(x_sample, mod[:, bp:bp + bs], past, prm)
    return (y_prompt, y_sample, *p_states, *s_states)
```

```python
import functools
import math

import jax
import jax.numpy as jnp
from jax import lax
from jax.experimental import pallas as pl
from jax.experimental.pallas import tpu as pltpu

F32 = jnp.float32
BF16 = jnp.bfloat16

HEAD_DIM = 256
N_MIXERS = 4
CONV_W = 4
GLA_RANK = 16
GLA_TAU = 16.0
PEER_HEADS = 8
PEER_KEYS = 128
PEER_TOPK = 16
N_MOD = 6
EPS = 1e-6
NEG_BIG = -1e30
MIN_FORGET = 1e-6

LANES = 128
SUBLANES = 8
SMALL_ROWS = 32
CHUNK = 128
SUB = 8
INV_BLOCK = 16
TOKEN_TILE = 512
EXPERT_TILE = 512
ROUTE_HEADS = 2
VMEM_LIMIT = 56 * 1024 * 1024
PEER_VMEM_LIMIT = 60 * 1024 * 1024


def _cparams(sem):
    return pltpu.CompilerParams(dimension_semantics=sem, vmem_limit_bytes=VMEM_LIMIT)


def _dot(a, b):
    return jnp.dot(a, b, preferred_element_type=F32)


def _dot_nt(a, b):
    return lax.dot_general(a, b, (((1,), (1,)), ((), ())), preferred_element_type=F32)


def _dot_tn(a, b):
    return lax.dot_general(a, b, (((0,), (0,)), ((), ())), preferred_element_type=F32)


def _split2(a):
    hi = a.astype(BF16)
    return hi, (a - hi.astype(F32)).astype(BF16)


def _dot_f32(a, b):
    a_hi, a_lo = _split2(a)
    b_hi, b_lo = _split2(b)
    return _dot(a_hi, b_hi) + (_dot(a_hi, b_lo) + _dot(a_lo, b_hi))


def _sigmoid(x):
    return 1.0 / (1.0 + jnp.exp(-x))


def _silu(x):
    return x * _sigmoid(x)


def _log_sigmoid(x):
    return jnp.minimum(x, 0.0) - jnp.log1p(jnp.exp(-jnp.abs(x)))


def _softplus(x):
    return jnp.maximum(x, 0.0) + jnp.log1p(jnp.exp(-jnp.abs(x)))


def _rms_mod(x, nw, sc, sh):
    y = x * lax.rsqrt(jnp.mean(x * x, axis=-1, keepdims=True) + EPS) * nw
    return y * (1.0 + sc) + sh


def _rms_mod_rows(x_ref, nw_ref, sc_ref, sh_ref, out_ref, rows=LANES):
    tm = x_ref.shape[0]
    step = rows if tm % rows == 0 else tm
    for r in range(0, tm, step):
        sl = slice(r, r + step)
        sc = sc_ref[...] if sc_ref.shape[0] == 1 else sc_ref[sl, :]
        sh = sh_ref[...] if sh_ref.shape[0] == 1 else sh_ref[sl, :]
        out_ref[sl, :] = _rms_mod(x_ref[sl, :], nw_ref[...], sc, sh).astype(out_ref.dtype)


def _merge(h, gate, onorm):
    hn = h * lax.rsqrt(jnp.mean(h * h, axis=-1, keepdims=True) + EPS)
    return (hn * onorm * gate).astype(BF16)


def _iota2(shape, dim):
    return lax.broadcasted_iota(jnp.int32, shape, dim)


def _row_to_col(r):
    n = r.shape[1]
    eye = _iota2((n, n), 0) == _iota2((n, n), 1)
    return jnp.sum(jnp.where(eye, r, 0.0), axis=1, keepdims=True)


def _interleave(units):
    units = list(units)
    while units:
        alive = []
        for u in units:
            try:
                next(u)
                alive.append(u)
            except StopIteration:
                pass
        units = alive


def _cumsum_pair(x_c, x_r, incl, incl_t):
    f_c = jnp.sum(jnp.where(incl, x_r, 0.0), axis=1, keepdims=True)
    f_r = jnp.sum(jnp.where(incl_t, x_c, 0.0), axis=0, keepdims=True)
    return f_c, f_r


def _cumsum_rows(x, tri_b):
    hi = x.astype(BF16)
    r1 = x - hi.astype(F32)
    mid = r1.astype(BF16)
    lo = (r1 - mid.astype(F32)).astype(BF16)
    return _dot(tri_b, hi) + _dot(tri_b, mid) + _dot(tri_b, lo)


def _ada_kernel(c_ref, w_ref, b_ref, o_ref):
    cs = _silu(c_ref[...]).astype(BF16)
    o_ref[...] = _dot(cs, w_ref[...].astype(BF16)) + b_ref[...]


def _ada(c_all, w_ada, b_ada):
    depth, d, n6 = w_ada.shape
    rows = c_all.shape[0]
    tn = 512
    return pl.pallas_call(
        _ada_kernel,
        grid=(depth, n6 // tn),
        in_specs=[
            pl.BlockSpec((rows, d), lambda l, j: (0, 0)),
            pl.BlockSpec((None, d, tn), lambda l, j: (l, 0, j)),
            pl.BlockSpec((None, 1, tn), lambda l, j: (l, 0, j)),
        ],
        out_specs=pl.BlockSpec((None, rows, tn), lambda l, j: (l, 0, j)),
        out_shape=jax.ShapeDtypeStruct((depth, rows, n6), F32),
        compiler_params=_cparams(("arbitrary", "arbitrary")),
    )(c_all, w_ada, b_ada.reshape(depth, 1, n6))


class _Tiling:
    def __init__(self, b, t, d, tile=TOKEN_TILE):
        self.b, self.t, self.d = b, t, d
        self.n = b * t
        self.per_batch = t % LANES == 0
        if self.per_batch:
            self.tm = next(m for m in (tile, 256, LANES) if m <= tile and t % m == 0)
        else:
            self.tm = self.n if self.n <= tile else tile
            assert self.n % self.tm == 0 and self.tm % SUBLANES == 0
        self.tiles = self.n // self.tm

    def mod_array(self, mod_l):
        if self.per_batch:
            return mod_l.reshape(self.b, 1, mod_l.shape[-1])
        return jnp.repeat(mod_l, self.t, axis=0)

    def mod_spec(self, k, grid_rank):
        d, tm, t = self.d, self.tm, self.t
        if self.per_batch:
            if grid_rank == 1:
                return pl.BlockSpec((None, 1, d), lambda i: ((i * tm) // t, 0, k))
            return pl.BlockSpec((None, 1, d), lambda i, j: ((i * tm) // t, 0, k))
        if grid_rank == 1:
            return pl.BlockSpec((tm, d), lambda i: (i, k))
        return pl.BlockSpec((tm, d), lambda i, j: (i, k))


def _in_proj_kernel(x_ref, nw_ref, sc_ref, sh_ref, w_ref, z_ref, zs_ref, h_scr, prev_scr, *, nh, tiles_per_piece):
    j = pl.program_id(1)
    n_tiles = pl.num_programs(1) - 1
    tn = z_ref.shape[1]
    width = tn + LANES
    groups = ((4, 0, 2 * nh), (12, 2 * nh, 2 * nh), (16, 4 * nh, GLA_RANK))

    def emit(head):
        piece = (j - 1) // tiles_per_piece
        amount = jnp.where(piece < 4, 0, width - jnp.where(piece < 12, 2 * nh, 4 * nh))
        zz = jnp.concatenate([prev_scr[...], head], axis=1)
        z_ref[...] = pltpu.roll(zz, amount, axis=1)[:, :tn]
        for hi, s, n_gate in groups:
            @pl.when(j == hi * tiles_per_piece)
            def _():
                lane = _iota2(head.shape, 1)
                zs_ref[...] = jnp.where((lane >= s) & (lane < s + n_gate), head, zs_ref[...])

    @pl.when(j == 0)
    def _():
        _rms_mod_rows(x_ref, nw_ref, sc_ref, sh_ref, h_scr)
        zs_ref[...] = jnp.zeros_like(zs_ref)
        prev_scr[...] = _dot_nt(h_scr[...], w_ref[...])

    @pl.when((j > 0) & (j < n_tiles))
    def _():
        za = _dot_nt(h_scr[...], w_ref[...])
        emit(za[:, :LANES])
        prev_scr[...] = za

    @pl.when(j == n_tiles)
    def _():
        emit(_dot_nt(h_scr[...], w_ref[:LANES, :]))


def _in_proj(til, x, nw, mod_arr, w_in_b, layer):
    if not til.per_batch:
        til = _Tiling(til.b, til.t, til.d, tile=256)
    n, d, tm = til.n, til.d, til.tm
    dg = d // N_MIXERS
    nh = dg // HEAD_DIM
    nbig = 16 * dg
    tn = min(1024, dg)
    assert 4 * nh + GLA_RANK <= LANES and w_in_b.shape[1] == nbig + 4 * nh + GLA_RANK
    return pl.pallas_call(
        functools.partial(_in_proj_kernel, nh=nh, tiles_per_piece=dg // tn),
        grid=(til.tiles, nbig // tn + 1),
        in_specs=[
            pl.BlockSpec((tm, d), lambda i, j: (i, 0)),
            pl.BlockSpec((None, 1, d), lambda i, j: (layer, 0, 0)),
            til.mod_spec(1, 2),
            til.mod_spec(0, 2),
            pl.BlockSpec((None, tn, d), lambda i, j: (layer, j, 0)),
        ],
        out_specs=[
            pl.BlockSpec((tm, tn), lambda i, j: (i, jnp.maximum(j - 1, 0))),
            pl.BlockSpec((tm, LANES), lambda i, j: (i, 0)),
        ],
        out_shape=[jax.ShapeDtypeStruct((n, nbig), F32), jax.ShapeDtypeStruct((n, LANES), F32)],
        scratch_shapes=[pltpu.VMEM((tm, d), BF16), pltpu.VMEM((tm, tn), F32)],
        compiler_params=_cparams(("arbitrary", "arbitrary")),
    )(x, nw, mod_arr, mod_arr, w_in_b)


def _mlstm_kernel(*refs, bb, nh, c, n_valid, has_init, n_alias):
    q_ref, k_ref, v_ref, g_ref, zc_ref, zr_ref, br_ref, bc_ref, on_ref = refs[:9]
    n_in = 9
    if has_init:
        c0_ref, n0_ref, m0_ref = refs[9:12]
        n_in = 12
    o_ref, cs_ref, ns_ref, ms_ref = refs[n_in + n_alias:]

    @pl.when(pl.program_id(1) == 0)
    def _():
        if has_init:
            cs_ref[...] = c0_ref[...]
            ns_ref[...] = n0_ref[...]
            ms_ref[...] = m0_ref[...]
        else:
            cs_ref[...] = jnp.zeros_like(cs_ref)
            ns_ref[...] = jnp.zeros_like(ns_ref)
            ms_ref[...] = jnp.zeros_like(ms_ref)

    hd = HEAD_DIM
    scale = hd ** -0.5
    ti = _iota2((c, c), 0)
    si = _iota2((c, c), 1)
    incl = si <= ti
    incl_t = ti <= si
    valid_c = _iota2((c, 1), 0) < n_valid
    valid_r = _iota2((1, c), 1) < n_valid
    def unit(b, h, zc, zr):
        hs = slice(h * hd, (h + 1) * hd)
        q = q_ref[b, :, hs]
        k = k_ref[b, :, hs] * scale
        v = v_ref[b, :, hs]
        ig_c = zc[:, h:h + 1]
        lf_c = _log_sigmoid(zc[:, nh + h:nh + h + 1])
        ig_r = zr[h:h + 1, :]
        lf_r = _log_sigmoid(zr[nh + h:nh + h + 1, :])
        if n_valid < c:
            ig_c = jnp.where(valid_c, ig_c, NEG_BIG)
            lf_c = jnp.where(valid_c, lf_c, 0.0)
            ig_r = jnp.where(valid_r, ig_r, NEG_BIG)
            lf_r = jnp.where(valid_r, lf_r, 0.0)
        cm = cs_ref[b, h]
        nv = ns_ref[b, h:h + 1, :]
        m0 = ms_ref[b, :, h:h + 1]
        f_c, f_r = _cumsum_pair(lf_c, lf_r, incl, incl_t)
        yield
        raw = f_c - f_r + ig_r
        a = f_c + m0
        m_t = jnp.maximum(a, jnp.max(jnp.where(incl, raw, NEG_BIG), axis=1, keepdims=True))
        p = jnp.where(incl, jnp.exp(jnp.where(incl, raw - m_t, 0.0)), 0.0)
        qb = q.astype(BF16)
        kb = k.astype(BF16)
        vb = v.astype(BF16)
        yield
        s = _dot_nt(qb, kb) * p
        inter = jnp.exp(a - m_t)
        qc = _dot(qb, cm.astype(BF16))
        yield
        num = inter * qc + _dot(s.astype(BF16), vb)
        den = inter * jnp.sum(q * nv, axis=1, keepdims=True) + jnp.sum(s, axis=1, keepdims=True)
        m_end = m_t[c - 1:c, :]
        w_end = jnp.exp(f_c[c - 1:c, :] - f_c + ig_c - m_end)
        dec = jnp.exp(a[c - 1:c, :] - m_end)
        kw = w_end * k
        yield
        hh = num / jnp.maximum(jnp.abs(den), jnp.exp(-m_t))
        cs_ref[b, h] = dec * cm + _dot_tn(kw.astype(BF16), vb)
        ns_ref[b, h:h + 1, :] = dec * nv + jnp.sum(kw, axis=0, keepdims=True)
        ms_ref[b, :, h:h + 1] = m_end
        o_ref[b, :, hs] = _merge(hh, _sigmoid(g_ref[b, :, hs]), on_ref[:, hs])

    for b in range(bb):
        zc = zc_ref[b] + br_ref[...]
        zr = zr_ref[b] + bc_ref[...]
        _interleave([unit(b, h, zc, zr) for h in range(nh)])


def _gla_unit(q, k, v, lg, s_mat, tri_b, emit):
    c, hd = q.shape
    sb = min(SUB, c)
    nb = c // sb
    g = _cumsum_rows(lg, tri_b)
    yield
    o = _dot((q * jnp.exp(g)).astype(BF16), s_mat.astype(BF16))

    if nb > 1:
        qparts, kparts = [], []
        for j in range(nb - 1):
            r1 = (j + 1) * sb
            g_end = g[r1 - 1:r1, :]
            qj = q[r1:, :] * jnp.exp(g[r1:, :] - g_end)
            kj = k[j * sb:r1, :] * jnp.exp(g_end - g[j * sb:r1, :])
            qparts.append(jnp.concatenate([jnp.zeros((r1, hd), F32), qj], axis=0).astype(BF16))
            pieces = [kj]
            if j > 0:
                pieces.insert(0, jnp.zeros((j * sb, hd), F32))
            pieces.append(jnp.zeros((c - r1, hd), F32))
            kparts.append(jnp.concatenate(pieces, axis=0).astype(BF16))
        a_off = _dot_nt(jnp.concatenate(qparts, axis=1), jnp.concatenate(kparts, axis=1))
    else:
        a_off = jnp.zeros((c, c), F32)
    yield

    lane = _iota2((sb, c), 1)
    trow = _iota2((sb, 1), 0)
    strips = []
    for i in range(nb):
        r0 = i * sb
        qi, ki, gi = q[r0:r0 + sb, :], k[r0:r0 + sb, :], g[r0:r0 + sb, :]
        strip = a_off[r0:r0 + sb, :]
        for s in range(sb):
            msk = trow >= s
            w = jnp.where(msk, jnp.exp(jnp.where(msk, gi - gi[s:s + 1, :], 0.0)), 0.0)
            col = jnp.sum(qi * ki[s:s + 1, :] * w, axis=1, keepdims=True)
            strip = jnp.where(lane == r0 + s, col, strip)
        strips.append(strip)
        yield
    a = strips[0] if nb == 1 else jnp.concatenate(strips, axis=0)
    vb = v.astype(BF16)
    o = o + _dot(a.astype(BF16), vb)

    g_end = g[c - 1:c, :]
    kt = (k * jnp.exp(g_end - g)).astype(BF16)
    yield
    emit(o, _row_to_col(jnp.exp(g_end)) * s_mat + _dot_tn(kt, vb))


def _gla_kernel(*refs, bb, nh, c, n_valid, has_init, n_alias, kind, layer):
    if kind == "hgrn":
        q_ref, k_ref, v_ref, g_ref, par_ref, on_ref = refs[:6]
        rest = refs[6:]
    else:
        q_ref, k_ref, v_ref, g_ref, zc_ref, wg_ref, bg_ref, on_ref = refs[:8]
        rest = refs[8:]
    if has_init:
        s0_ref = rest[0]
        rest = rest[1:]
    o_ref, st_ref = rest[n_alias:]

    @pl.when(pl.program_id(1) == 0)
    def _():
        if has_init:
            st_ref[...] = s0_ref[...]
        else:
            st_ref[...] = jnp.zeros_like(st_ref)

    hd = HEAD_DIM
    tri_b = (_iota2((c, c), 1) <= _iota2((c, c), 0)).astype(BF16)
    valid_c = _iota2((c, 1), 0) < n_valid
    if kind == "hgrn":
        lbp = par_ref[...]
        ex = jnp.exp(lbp - jnp.max(lbp, axis=0, keepdims=True))
        lbs = ex / jnp.sum(ex, axis=0, keepdims=True)
        lb = jnp.zeros_like(lbs[0:1, :])
        for j in range(1, layer + 1):
            lb = lb + lbs[j:j + 1, :]
    def finish(b, h, hs):
        def emit(o, s_new):
            st_ref[b, h] = s_new
            o_ref[b, :, hs] = _merge(o, _silu(g_ref[b, :, hs]), on_ref[:, hs])
        return emit

    for b in range(bb):
        if kind == "gla":
            gate_in = _dot(zc_ref[b].astype(BF16), wg_ref[...].astype(BF16)) + bg_ref[...]
        units = []
        for h in range(nh):
            hs = slice(h * hd, (h + 1) * hd)
            if kind == "hgrn":
                fg = k_ref[b, :, hs]
                lbh = lb[:, hs]
                f = lbh + (1.0 - lbh) * _sigmoid(fg)
                lg = jnp.log(jnp.maximum(f, MIN_FORGET))
                k = (1.0 - lbh) * _sigmoid(-fg)
                q = _silu(q_ref[b, :, hs])
            else:
                lg = _log_sigmoid(gate_in[:, hs]) / GLA_TAU
                k = k_ref[b, :, hs]
                q = q_ref[b, :, hs] * (hd ** -0.5)
            v = v_ref[b, :, hs]
            if n_valid < c:
                lg = jnp.where(valid_c, lg, 0.0)
                k = jnp.where(valid_c, k, 0.0)
            units.append(_gla_unit(q, k, v, lg, st_ref[b, h], tri_b, finish(b, h, hs)))
        _interleave(units)


def _gdn_kernel(*refs, bb, nh, c, n_valid, has_init, n_alias):
    (q_ref, k_ref, v_ref, g_ref, zc_ref, zr_ref, br_ref, bc_ref, ar_ref, ac_ref, cw_ref, on_ref) = refs[:12]
    n_in = 12
    if has_init:
        s0_ref, cv0_ref = refs[12:14]
        n_in = 14
    o_ref, st_ref, cvo_ref, tail_ref = refs[n_in + n_alias:]
    t_id = pl.program_id(1)
    n_chunks = pl.num_programs(1)
    tail_rows = SUBLANES
    n_buf = CONV_W - 1

    @pl.when(t_id == 0)
    def _():
        tail_ref[...] = jnp.zeros_like(tail_ref)
        if has_init:
            st_ref[...] = s0_ref[...]
            for b in range(bb):
                for pc in range(3):
                    tail_ref[b, pc, tail_rows - n_buf:tail_rows, :] = cv0_ref[b, :, pc, :]
        else:
            st_ref[...] = jnp.zeros_like(st_ref)

    hd = HEAD_DIM
    scale = hd ** -0.5
    ti = _iota2((c, c), 0)
    si = _iota2((c, c), 1)
    incl = si <= ti
    incl_t = ti <= si
    strict = si < ti
    eye = (si == ti).astype(F32)
    valid_c = _iota2((c, 1), 0) < n_valid
    valid_r = _iota2((1, c), 1) < n_valid
    raw_refs = (q_ref, k_ref, v_ref)
    for b in range(bb):
        conv = []
        for pc in range(3):
            u = raw_refs[pc][b]
            ext = jnp.concatenate([tail_ref[b, pc], u], axis=0)
            acc = u * cw_ref[CONV_W - 1:CONV_W, pc, :]
            for j in range(1, CONV_W):
                shifted = pltpu.roll(ext, j, axis=0)[tail_rows:tail_rows + c, :]
                acc = acc + shifted * cw_ref[CONV_W - 1 - j:CONV_W - j, pc, :]
            conv.append(_silu(acc))
            tail_ref[b, pc] = u[c - tail_rows:c, :]

        @pl.when(t_id == n_chunks - 1)
        def _():
            for pc in range(3):
                cvo_ref[b, :, pc, :] = raw_refs[pc][b, n_valid - n_buf:n_valid, :]

        zc = zc_ref[b] + br_ref[...]
        zr = zr_ref[b] + bc_ref[...]
        units = []
        for h in range(nh):
            hs = slice(h * hd, (h + 1) * hd)
            q = conv[0][:, hs]
            k = conv[1][:, hs]
            v = conv[2][:, hs]
            q = q * lax.rsqrt(jnp.sum(q * q, axis=1, keepdims=True) + EPS) * scale
            k = k * lax.rsqrt(jnp.sum(k * k, axis=1, keepdims=True) + EPS)
            ca, cb = 2 * nh + h, 3 * nh + h
            lg_c = -jnp.exp(ar_ref[:, ca:ca + 1]) * _softplus(zc[:, ca:ca + 1])
            lg_r = -jnp.exp(ac_ref[ca:ca + 1, :]) * _softplus(zr[ca:ca + 1, :])
            beta = _sigmoid(zc[:, cb:cb + 1])
            if n_valid < c:
                lg_c = jnp.where(valid_c, lg_c, 0.0)
                lg_r = jnp.where(valid_r, lg_r, 0.0)
                beta = jnp.where(valid_c, beta, 0.0)
            s_mat = st_ref[b, h]

            g_c, g_r = _cumsum_pair(lg_c, lg_r, incl, incl_t)
            eg = jnp.exp(g_c)
            rel = jnp.where(incl, jnp.exp(jnp.where(incl, g_c - g_r, 0.0)), 0.0)
            qb = q.astype(BF16)
            kb = k.astype(BF16)
            kq_s = _dot(jnp.concatenate([kb, qb], axis=0), s_mat.astype(BF16))
            m = jnp.where(strict, beta * rel * _dot_nt(kb, kb), 0.0)
            rhs = beta * (v - eg * kq_s[:c, :])
            qk = (_dot_nt(qb, kb) * rel).astype(BF16)
            units.append(dict(hs=hs, k=k, s_mat=s_mat, g_c=g_c, eg=eg, m=m, rhs=rhs, qk=qk, qs=kq_s[c:, :]))

        b0 = min(INV_BLOCK, c)
        md = [jnp.where((ti // b0) == (si // b0), un["m"], 0.0) for un in units]
        tinv = [eye - x for x in md]
        pw = [_dot_f32(x, x) for x in md]
        n_it = int(math.log2(b0)) - 1
        for it in range(n_it):
            tinv = [t + _dot_f32(t, p) for t, p in zip(tinv, pw)]
            if it < n_it - 1:
                pw = [_dot_f32(p, p) for p in pw]
        blk = b0
        while blk < c:
            below = ((ti // (2 * blk)) == (si // (2 * blk))) & ((ti // blk) != (si // blk))
            tinv = [t - _dot_f32(_dot_f32(t, jnp.where(below, un["m"], 0.0)), t) for t, un in zip(tinv, units)]
            blk *= 2

        for un, t in zip(units, tinv):
            ub = _dot_f32(t, un["rhs"]).astype(BF16)
            o = un["eg"] * un["qs"] + _dot(un["qk"], ub)
            g_end = un["g_c"][c - 1:c, :]
            kd = (jnp.exp(g_end - un["g_c"]) * un["k"]).astype(BF16)
            h = un["hs"].start // hd
            st_ref[b, h] = jnp.exp(g_end) * un["s_mat"] + _dot_tn(kd, ub)
            o_ref[b, :, un["hs"]] = _merge(o, _silu(g_ref[b, :, un["hs"]]), on_ref[:, un["hs"]])


def _mixers(b, t, dg, z_big, zs, states, prev, prm, layer, depth):
    nh = dg // HEAD_DIM
    hd = HEAD_DIM
    n = b * t
    tp = -(-t // SUBLANES) * SUBLANES
    c = min(CHUNK, tp)
    assert tp % c == 0 and (tp == t or tp == c) and t >= CONV_W - 1 and 4 * nh + GLA_RANK <= SMALL_ROWS
    nc = tp // c
    n_valid = c - (tp - t)
    has_init = states is not None
    bb = 4 if (has_init and b % 4 == 0) else 1

    z3 = z_big.reshape(b, t, 16 * dg)
    zs3 = zs.reshape(b, t, LANES)
    if tp != t:
        z3 = jnp.pad(z3, ((0, 0), (0, tp - t), (0, 0)))
        zs3 = jnp.pad(zs3, ((0, 0), (0, tp - t), (0, 0)))
    zr4 = jnp.swapaxes(zs3[:, :, :SMALL_ROWS].reshape(b, nc, c, SMALL_ROWS), 2, 3)

    grid = (b // bb, nc)
    piece = lambda p: pl.BlockSpec((bb, c, dg), lambda i, j: (i, j, p))
    zc_spec = pl.BlockSpec((bb, c, LANES), lambda i, j: (i, j, 0))
    zr_spec = pl.BlockSpec((bb, None, SMALL_ROWS, c), lambda i, j: (i, j, 0, 0))
    full2 = lambda a: pl.BlockSpec(a.shape, lambda i, j: (0,) * a.ndim)
    mat_spec = pl.BlockSpec((None, bb, nh, hd, hd), lambda i, j: (layer, i, 0, 0, 0))
    n_spec = pl.BlockSpec((None, bb, nh, hd), lambda i, j: (layer, i, 0, 0))
    m_spec = pl.BlockSpec((None, bb, 1, nh), lambda i, j: (layer, i, 0, 0))
    cv_spec = pl.BlockSpec((None, bb, CONV_W - 1, 3, dg), lambda i, j: (layer, i, 0, 0, 0))
    mat_shape = jax.ShapeDtypeStruct((depth, b, nh, hd, hd), F32)
    n_shape = jax.ShapeDtypeStruct((depth, b, nh, hd), F32)
    m_shape = jax.ShapeDtypeStruct((depth, b, 1, nh), F32)
    cv_shape = jax.ShapeDtypeStruct((depth, b, CONV_W - 1, 3, dg), F32)
    any_spec = pl.BlockSpec(memory_space=pl.ANY)
    o_spec = pl.BlockSpec((bb, c, dg), lambda i, j: (i, j, 0))
    o_shape = jax.ShapeDtypeStruct((b, tp, dg), BF16)
    cp = _cparams(("arbitrary", "arbitrary"))
    on = prm["out_norm"][layer].reshape(N_MIXERS, 1, dg)
    br, bc, ar, ac = prm["bias_row"][layer], prm["bias_col"][layer], prm["alog_row"][layer], prm["alog_col"][layer]

    def call(body, ins, specs, init, state_specs, state_shapes, prev_arrays, scratch=()):
        ins, specs = list(ins), list(specs)
        if has_init:
            ins += init
            specs += state_specs
        n_alias = 0 if prev_arrays is None else len(prev_arrays)
        aliases = {}
        if n_alias:
            aliases = {len(ins) + k: 1 + k for k in range(n_alias)}
            ins += list(prev_arrays)
            specs += [any_spec] * n_alias
        return pl.pallas_call(
            functools.partial(body, bb=bb, nh=nh, c=c, n_valid=n_valid, has_init=has_init, n_alias=n_alias),
            grid=grid, in_specs=specs, out_specs=[o_spec] + list(state_specs),
            out_shape=[o_shape] + list(state_shapes), scratch_shapes=list(scratch),
            input_output_aliases=aliases, compiler_params=cp,
        )(*ins)

    st = states
    pv = prev
    o_a, m_c, m_n, m_m = call(
        _mlstm_kernel, [z3, z3, z3, z3, zs3, zr4, br, bc, on[0]],
        [piece(0), piece(1), piece(2), piece(3), zc_spec, zr_spec, full2(br), full2(bc), full2(on[0])],
        None if st is None else [st[0], st[1], st[2].reshape(depth, b, 1, nh)],
        [mat_spec, n_spec, m_spec], [mat_shape, n_shape, m_shape],
        None if pv is None else [pv[0], pv[1], pv[2]])

    lbp = prm["hgrn_lb"]
    o_b, s_hgrn = call(
        functools.partial(_gla_kernel, kind="hgrn", layer=layer), [z3, z3, z3, z3, lbp, on[1]],
        [piece(4), piece(5), piece(6), piece(7), full2(lbp), full2(on[1])],
        None if st is None else [st[3]], [mat_spec], [mat_shape], None if pv is None else [pv[3]])

    cw = prm["gdn_conv_w"][layer].reshape(CONV_W, 3, dg)
    o_c, s_gdn, s_conv = call(
        _gdn_kernel, [z3, z3, z3, z3, zs3, zr4, br, bc, ar, ac, cw, on[2]],
        [piece(8), piece(9), piece(10), piece(11), zc_spec, zr_spec, full2(br), full2(bc), full2(ar),
         full2(ac), full2(cw), full2(on[2])],
        None if st is None else [st[4], st[5].reshape(depth, b, CONV_W - 1, 3, dg)],
        [mat_spec, cv_spec], [mat_shape, cv_shape], None if pv is None else [pv[4], pv[5]],
        scratch=[pltpu.VMEM((bb, 3, SUBLANES, dg), F32)])

    wg, bg = prm["gla_w_pad"][layer], prm["gla_b_gate"][layer].reshape(1, dg)
    o_d, s_gla = call(
        functools.partial(_gla_kernel, kind="gla", layer=layer), [z3, z3, z3, z3, zs3, wg, bg, on[3]],
        [piece(12), piece(13), piece(14), piece(15), zc_spec, full2(wg), full2(bg), full2(on[3])],
        None if st is None else [st[6]], [mat_spec], [mat_shape], None if pv is None else [pv[6]])

    outs = [o[:, :t, :].reshape(n, dg) for o in (o_a, o_b, o_c, o_d)]
    return outs, (m_c, m_n, m_m, s_hgrn, s_gdn, s_conv, s_gla)


def _out_proj_kernel(oa_ref, ob_ref, oc_ref, od_ref, w_ref, x_ref, g_ref, y_ref):
    acc = _dot(oa_ref[...], w_ref[0])
    acc = acc + _dot(ob_ref[...], w_ref[1])
    acc = acc + _dot(oc_ref[...], w_ref[2])
    acc = acc + _dot(od_ref[...], w_ref[3])
    y_ref[...] = x_ref[...] + g_ref[...] * acc


def _out_proj(til, outs, w_out4, x, mod_arr, layer):
    n, d, tm = til.n, til.d, til.tm
    dg = d // N_MIXERS
    tn = min(1024, d)
    o_spec = pl.BlockSpec((tm, dg), lambda i, j: (i, 0))
    if til.per_batch:
        g_spec = pl.BlockSpec((None, 1, tn), lambda i, j: ((i * tm) // til.t, 0, (2 * d) // tn + j))
    else:
        g_spec = pl.BlockSpec((tm, tn), lambda i, j: (i, (2 * d) // tn + j))
    return pl.pallas_call(
        _out_proj_kernel,
        grid=(til.tiles, d // tn),
        in_specs=[o_spec, o_spec, o_spec, o_spec,
                  pl.BlockSpec((None, N_MIXERS, dg, tn), lambda i, j: (layer, 0, 0, j)),
                  pl.BlockSpec((tm, tn), lambda i, j: (i, j)),
                  g_spec],
        out_specs=pl.BlockSpec((tm, tn), lambda i, j: (i, j)),
        out_shape=jax.ShapeDtypeStruct((n, d), F32),
        compiler_params=_cparams(("arbitrary", "arbitrary")),
    )(*outs, w_out4, x, mod_arr)


def _top_desc(s, count):
    rows = float(s.shape[0])
    ri = _iota2(s.shape, 0).astype(F32)
    vals = []
    for r in range(count):
        mx = jnp.max(s, axis=0, keepdims=True)
        vals.append(mx)
        if r < count - 1:
            first = jnp.min(jnp.where(s == mx, ri, rows), axis=0, keepdims=True)
            s = jnp.where(ri == first, -jnp.inf, s)
    return vals


def _sort16_network():
    def merge(lo, hi, r):
        step = r * 2
        if step < hi - lo:
            yield from merge(lo, hi, step)
            yield from merge(lo + r, hi, step)
            yield from [(i, i + r) for i in range(lo + r, hi - r, step)]
        else:
            yield (lo, lo + r)

    def sort(lo, hi):
        if hi - lo >= 1:
            mid = lo + (hi - lo) // 2
            yield from sort(lo, mid)
            yield from sort(mid + 1, hi)
            yield from merge(lo, hi, 1)

    return list(sort(0, PEER_TOPK - 1))


def _bitonic_merge16():
    out, s = [], PEER_TOPK // 2
    while s >= 1:
        out += [(i, i + s) for i in range(PEER_TOPK) if (i & s) == 0]
        s //= 2
    return out


def _top16_sorted(s):
    assert s.shape[0] == PEER_TOPK * SUBLANES
    slabs = [s[SUBLANES * k:SUBLANES * (k + 1), :] for k in range(PEER_TOPK)]

    def exchange(net):
        for i, j in net:
            slabs[i], slabs[j] = jnp.maximum(slabs[i], slabs[j]), jnp.minimum(slabs[i], slabs[j])

    exchange(_sort16_network())
    merge_net = _bitonic_merge16()
    for shift in (4, 2, 1):
        other = [pltpu.roll(x, shift, axis=0) for x in slabs]
        for k in range(PEER_TOPK):
            slabs[k] = jnp.maximum(slabs[k], other[PEER_TOPK - 1 - k])
        exchange(merge_net)
    return [x[0:1, :] for x in slabs]


def _cand_pairs():
    return [(a, b) for a in range(PEER_TOPK) for b in range(PEER_TOPK) if (a + 1) * (b + 1) <= PEER_TOPK]


def _route_kernel(x_ref, nw_ref, sc_ref, sh_ref, wq_ref, key_ref,
                  h2t_ref, s1_ref, s2_ref, e1_ref, e2_ref, tau_ref, cand_ref, h2_scr):
    @pl.when(pl.program_id(1) == 0)
    def _():
        h2 = _rms_mod(x_ref[...], nw_ref[...], sc_ref[...], sh_ref[...])
        h2_scr[...] = h2.astype(BF16)
        h2t_ref[...] = h2.T.astype(BF16)

    half = PEER_KEYS
    heads = s1_ref.shape[0]
    q = _dot(h2_scr[...], wq_ref[...])
    pairs = _cand_pairs()
    for hh in range(heads):
        q1 = q[:, (2 * hh) * half:(2 * hh + 1) * half]
        q2 = q[:, (2 * hh + 1) * half:(2 * hh + 2) * half]
        s1 = _dot_nt(key_ref[hh, 0].astype(BF16), q1.astype(BF16))
        s2 = _dot_nt(key_ref[hh, 1].astype(BF16), q2.astype(BF16))
        v1 = _top16_sorted(s1)
        v2 = _top16_sorted(s2)
        cand_ref[hh] = jnp.full(cand_ref.shape[1:], -jnp.inf, F32)
        for r, (a, b) in enumerate(pairs):
            cand_ref[hh, r:r + 1, :] = v1[a] + v2[b]
        best = _top_desc(cand_ref[hh], PEER_TOPK)
        zsum = jnp.zeros_like(best[0])
        for r in range(PEER_TOPK):
            zsum = zsum + jnp.exp(best[r] - best[0])
        s1_ref[hh] = s1
        s2_ref[hh] = s2
        e1_ref[hh] = jnp.exp(s1 - v1[0]) / zsum
        e2_ref[hh] = jnp.exp(s2 - v2[0])
        tau_ref[hh] = best[PEER_TOPK - 1]


def _route(til, x1, nw, mod_arr, w_q, sub_keys, layer):
    n, d, tm = til.n, til.d, til.tm
    qd = w_q.shape[2] // PEER_HEADS
    n_cand = -(-len(_cand_pairs()) // SUBLANES) * SUBLANES
    hb = ROUTE_HEADS
    tok = pl.BlockSpec((hb, PEER_KEYS, tm), lambda i, h: (h, 0, i))
    tok_shape = jax.ShapeDtypeStruct((PEER_HEADS, PEER_KEYS, n), F32)
    return pl.pallas_call(
        _route_kernel,
        grid=(til.tiles, PEER_HEADS // hb),
        in_specs=[
            pl.BlockSpec((tm, d), lambda i, h: (i, 0)),
            pl.BlockSpec((None, 1, d), lambda i, h: (layer, 0, 0)),
            til.mod_spec(4, 2),
            til.mod_spec(3, 2),
            pl.BlockSpec((None, d, hb * qd), lambda i, h: (layer, 0, h)),
            pl.BlockSpec((None, hb, 2, PEER_KEYS, qd // 2), lambda i, h: (layer, h, 0, 0, 0)),
        ],
        out_specs=[
            pl.BlockSpec((d, tm), lambda i, h: (0, i)),
            tok, tok, tok, tok,
            pl.BlockSpec((hb, 1, tm), lambda i, h: (h, 0, i)),
        ],
        out_shape=[jax.ShapeDtypeStruct((d, n), BF16), tok_shape, tok_shape, tok_shape, tok_shape,
                   jax.ShapeDtypeStruct((PEER_HEADS, 1, n), F32)],
        scratch_shapes=[pltpu.VMEM((hb, n_cand, tm), F32), pltpu.VMEM((tm, d), BF16)],
        compiler_params=_cparams(("arbitrary", "arbitrary")),
    )(x1, nw, mod_arr, mod_arr, w_q, sub_keys)


def _peer_kernel(h2t_ref, u_ref, vt_ref, s1_ref, s2_ref, e1_ref, e2_ref, tau_ref, o_ref, act_scr, p_scr, *, te, n_et):
    e = pl.program_id(1)

    @pl.when(e == 0)
    def _():
        o_ref[...] = jnp.zeros_like(o_ref)
        act_scr[...] = jnp.zeros_like(act_scr)
        p_scr[...] = jnp.zeros_like(p_scr)

    o_ref[...] += _dot(vt_ref[...], p_scr[...])

    groups = te // PEER_KEYS
    tile = jnp.clip(e - 1, 0, n_et - 1)
    for ii in range(groups):
        rows = slice(ii * PEER_KEYS, (ii + 1) * PEER_KEYS)
        act = act_scr[rows, :]
        gel = 0.5 * act * (1.0 + lax.erf(act * (2.0 ** -0.5)))
        row = tile * groups + ii
        acc = jnp.zeros(act.shape, F32)
        for h in range(PEER_HEADS):
            sm = s1_ref[h, pl.ds(row, 1), :] + s2_ref[h]
            sel = jnp.where(sm >= tau_ref[h], e2_ref[h], 0.0)
            acc = acc + sel * e1_ref[h, pl.ds(row, 1), :]
        p_scr[rows, :] = (acc * gel).astype(BF16)

    act_scr[...] = _dot(u_ref[...].astype(BF16), h2t_ref[...])


def _peer(til, h2, routing, u_tab, vt_tab, layer):
    n, d, tm = til.n, til.d, til.tm
    ne = u_tab.shape[1]
    te = EXPERT_TILE
    n_et = ne // te
    once = pl.Buffered(1)
    tok = pl.BlockSpec((PEER_HEADS, PEER_KEYS, tm), lambda i, e: (0, 0, i))
    return pl.pallas_call(
        functools.partial(_peer_kernel, te=te, n_et=n_et),
        grid=(til.tiles, n_et + 2),
        in_specs=[
            pl.BlockSpec((d, tm), lambda i, e: (0, i), pipeline_mode=once),
            pl.BlockSpec((None, te, d), lambda i, e: (layer, jnp.minimum(e, n_et - 1), 0)),
            pl.BlockSpec((None, d, te), lambda i, e: (layer, 0, jnp.clip(e - 2, 0, n_et - 1))),
            tok, tok, tok, tok,
            pl.BlockSpec((PEER_HEADS, 1, tm), lambda i, e: (0, 0, i)),
        ],
        out_specs=pl.BlockSpec((d, tm), lambda i, e: (0, i), pipeline_mode=once),
        out_shape=jax.ShapeDtypeStruct((d, n), F32),
        scratch_shapes=[pltpu.VMEM((te, tm), F32), pltpu.VMEM((te, tm), BF16)],
        compiler_params=pltpu.CompilerParams(dimension_semantics=("arbitrary", "arbitrary"),
                                             vmem_limit_bytes=PEER_VMEM_LIMIT),
    )(h2, u_tab, vt_tab, *routing)


def _residual_kernel(x_ref, pt_ref, g_ref, o_ref):
    o_ref[...] = x_ref[...] + g_ref[...] * pt_ref[...].T


def _residual_norm_kernel(x_ref, pt_ref, g_ref, nw_ref, o_ref):
    x = x_ref[...] + g_ref[...] * pt_ref[...].T
    o_ref[...] = x * lax.rsqrt(jnp.mean(x * x, axis=-1, keepdims=True) + EPS) * nw_ref[...]


def _residual(til, x1, p, mod_arr, final_norm):
    til = _Tiling(til.b, til.t, til.d, tile=256)
    n, d, tm = til.n, til.d, til.tm
    row = pl.BlockSpec((tm, d), lambda i: (i, 0))
    ins = [x1, p, mod_arr]
    specs = [row, pl.BlockSpec((d, tm), lambda i: (0, i)), til.mod_spec(5, 1)]
    body = _residual_kernel
    if final_norm is not None:
        ins.append(final_norm.reshape(1, d))
        specs.append(pl.BlockSpec((1, d), lambda i: (0, 0)))
        body = _residual_norm_kernel
    return pl.pallas_call(
        body, grid=(til.tiles,), in_specs=specs, out_specs=row,
        out_shape=jax.ShapeDtypeStruct((n, d), F32),
        compiler_params=_cparams(("arbitrary",)),
    )(*ins)


def _trunk(x, mod, states, prm):
    b, t, d = x.shape
    depth = mod.shape[0]
    dg = d // N_MIXERS
    til = _Tiling(b, t, d)
    xf = x.reshape(b * t, d)
    new = None
    for l in range(depth):
        mod_arr = til.mod_array(mod[l])
        z_big, zs = _in_proj(til, xf, prm["norm_mix"], mod_arr, prm["w_in_b"], l)
        outs, new = _mixers(b, t, dg, z_big, zs, states, new, prm, l, depth)
        x1 = _out_proj(til, outs, prm["w_out4"], xf, mod_arr, l)
        h2, *routing = _route(til, x1, prm["norm_ffn"], mod_arr, prm["w_q"], prm["peer_sub_keys"], l)
        p = _peer(til, h2, routing, prm["peer_u"], prm["peer_vt"], l)
        xf = _residual(til, x1, p, mod_arr, prm["final_norm"] if l == depth - 1 else None)
    m_c, m_n, m_m, s_hgrn, s_gdn, s_conv, s_gla = new
    nh = dg // HEAD_DIM
    new_states = [m_c, m_n, m_m.reshape(depth, b, nh), s_hgrn, s_gdn,
                  s_conv.reshape(depth, b, CONV_W - 1, 3 * dg), s_gla]
    return xf.reshape(b, t, d), new_states


def _prepare(mlstm_b_i, mlstm_b_f, gdn_a_log, gdn_dt_bias, gla_w_gate, d):
    depth = mlstm_b_i.shape[0]
    dg = d // N_MIXERS
    nh = dg // HEAD_DIM
    n_small = 4 * nh + GLA_RANK
    zeros = lambda k: jnp.zeros((depth, k), F32)
    bias = jnp.concatenate([mlstm_b_i, mlstm_b_f, gdn_dt_bias, zeros(LANES - 3 * nh)], axis=1)
    alog = jnp.concatenate([zeros(2 * nh), gdn_a_log, zeros(LANES - 3 * nh)], axis=1)
    gla_w_pad = jnp.concatenate(
        [jnp.zeros((depth, 4 * nh, dg), F32), gla_w_gate, jnp.zeros((depth, LANES - n_small, dg), F32)], axis=1)
    return dict(
        bias_row=bias.reshape(depth, 1, LANES), bias_col=bias[:, :SMALL_ROWS].reshape(depth, SMALL_ROWS, 1),
        alog_row=alog.reshape(depth, 1, LANES), alog_col=alog[:, :SMALL_ROWS].reshape(depth, SMALL_ROWS, 1),
        gla_w_pad=gla_w_pad)


def kernel(x_prompt, x_sample, c_prompt, c_sample, state_mlstm_C, state_mlstm_n, state_mlstm_m, state_hgrn, state_gdn, state_gdn_conv, state_gla, w_ada, b_ada, norm_mix, norm_ffn, w_in, mlstm_b_i, mlstm_b_f, hgrn_lb, gdn_conv_w, gdn_a_log, gdn_dt_bias, gla_w_gate, gla_b_gate, out_norm, w_out, peer_w_q, peer_sub_keys, peer_u, peer_v, final_norm):
    depth, d = norm_mix.shape
    dg = d // N_MIXERS
    assert d % (N_MIXERS * HEAD_DIM) == 0
    prm = _prepare(mlstm_b_i, mlstm_b_f, gdn_a_log, gdn_dt_bias, gla_w_gate, d)
    prm.update(
        norm_mix=norm_mix.reshape(depth, 1, d), norm_ffn=norm_ffn.reshape(depth, 1, d), hgrn_lb=hgrn_lb,
        gdn_conv_w=gdn_conv_w, gla_b_gate=gla_b_gate, out_norm=out_norm, final_norm=final_norm,
        peer_sub_keys=peer_sub_keys, w_in_b=jnp.swapaxes(w_in, 1, 2).astype(BF16),
        w_out4=w_out.astype(BF16).reshape(depth, N_MIXERS, dg, d),
        w_q=peer_w_q.astype(BF16), peer_u=peer_u,
        peer_vt=jnp.swapaxes(peer_v, 1, 2).astype(BF16))

    bp, bs = c_prompt.shape[0], c_sample.shape[0]
    rows = -(-(bp + bs) // SUBLANES) * SUBLANES
    c_all = jnp.concatenate([c_prompt, c_sample, jnp.zeros((rows - bp - bs, d), F32)], axis=0)
    mod = _ada(c_all, w_ada, b_ada)

    y_prompt, p_states = _trunk(x_prompt, mod[:, :bp], None, prm)
    past = (state_mlstm_C, state_mlstm_n, state_mlstm_m, state_hgrn, state_gdn, state_gdn_conv, state_gla)
    y_sample, s_states = _trunk(x_sample, mod[:, bp:bp + bs], past, prm)
    return (y_prompt, y_sample, *p_states, *s_states)
```

```python
import functools
import math

import jax
import jax.numpy as jnp
from jax import lax
from jax.experimental import pallas as pl
from jax.experimental.pallas import tpu as pltpu

F32 = jnp.float32
BF16 = jnp.bfloat16

HEAD_DIM = 256
N_MIXERS = 4
CONV_W = 4
GLA_RANK = 16
GLA_TAU = 16.0
PEER_HEADS = 8
PEER_KEYS = 128
PEER_TOPK = 16
N_MOD = 6
EPS = 1e-6
NEG_BIG = -1e30
MIN_FORGET = 1e-6

LANES = 128
SUBLANES = 8
SMALL_ROWS = 32
CHUNK = 128
SUB = 8
INV_BLOCK = 16
TOKEN_TILE = 512
EXPERT_TILE = 512
ROUTE_HEADS = 2
VMEM_LIMIT = 56 * 1024 * 1024
PEER_VMEM_LIMIT = 60 * 1024 * 1024


def _cparams(sem):
    return pltpu.CompilerParams(dimension_semantics=sem, vmem_limit_bytes=VMEM_LIMIT)


def _dot(a, b):
    return jnp.dot(a, b, preferred_element_type=F32)


def _dot_nt(a, b):
    return lax.dot_general(a, b, (((1,), (1,)), ((), ())), preferred_element_type=F32)


def _dot_tn(a, b):
    return lax.dot_general(a, b, (((0,), (0,)), ((), ())), preferred_element_type=F32)


def _split2(a):
    hi = a.astype(BF16)
    return hi, (a - hi.astype(F32)).astype(BF16)


def _dot_f32(a, b):
    a_hi, a_lo = _split2(a)
    b_hi, b_lo = _split2(b)
    return _dot(a_hi, b_hi) + (_dot(a_hi, b_lo) + _dot(a_lo, b_hi))


def _sigmoid(x):
    return 1.0 / (1.0 + jnp.exp(-x))


def _silu(x):
    return x * _sigmoid(x)


def _log_sigmoid(x):
    return jnp.minimum(x, 0.0) - jnp.log1p(jnp.exp(-jnp.abs(x)))


def _softplus(x):
    return jnp.maximum(x, 0.0) + jnp.log1p(jnp.exp(-jnp.abs(x)))


def _rms_mod(x, nw, sc, sh):
    y = x * lax.rsqrt(jnp.mean(x * x, axis=-1, keepdims=True) + EPS) * nw
    return y * (1.0 + sc) + sh


def _rms_mod_rows(x_ref, nw_ref, sc_ref, sh_ref, out_ref, rows=LANES):
    tm = x_ref.shape[0]
    step = rows if tm % rows == 0 else tm
    for r in range(0, tm, step):
        sl = slice(r, r + step)
        sc = sc_ref[...] if sc_ref.shape[0] == 1 else sc_ref[sl, :]
        sh = sh_ref[...] if sh_ref.shape[0] == 1 else sh_ref[sl, :]
        out_ref[sl, :] = _rms_mod(x_ref[sl, :], nw_ref[...], sc, sh).astype(out_ref.dtype)


def _merge(h, gate, onorm):
    hn = h * lax.rsqrt(jnp.mean(h * h, axis=-1, keepdims=True) + EPS)
    return (hn * onorm * gate).astype(BF16)


def _iota2(shape, dim):
    return lax.broadcasted_iota(jnp.int32, shape, dim)


def _row_to_col(r):
    n = r.shape[1]
    eye = _iota2((n, n), 0) == _iota2((n, n), 1)
    return jnp.sum(jnp.where(eye, r, 0.0), axis=1, keepdims=True)


def _interleave(units):
    units = list(units)
    while units:
        alive = []
        for u in units:
            try:
                next(u)
                alive.append(u)
            except StopIteration:
                pass
        units = alive


def _cumsum_pair(x_c, x_r, incl, incl_t):
    f_c = jnp.sum(jnp.where(incl, x_r, 0.0), axis=1, keepdims=True)
    f_r = jnp.sum(jnp.where(incl_t, x_c, 0.0), axis=0, keepdims=True)
    return f_c, f_r


def _cumsum_rows(x, tri_b):
    hi = x.astype(BF16)
    r1 = x - hi.astype(F32)
    mid = r1.astype(BF16)
    lo = (r1 - mid.astype(F32)).astype(BF16)
    return _dot(tri_b, hi) + _dot(tri_b, mid) + _dot(tri_b, lo)


def _ada_kernel(c_ref, w_ref, b_ref, o_ref):
    cs = _silu(c_ref[...]).astype(BF16)
    o_ref[...] = _dot(cs, w_ref[...].astype(BF16)) + b_ref[...]


def _ada(c_all, w_ada, b_ada):
    depth, d, n6 = w_ada.shape
    rows = c_all.shape[0]
    tn = 512
    return pl.pallas_call(
        _ada_kernel,
        grid=(depth, n6 // tn),
        in_specs=[
            pl.BlockSpec((rows, d), lambda l, j: (0, 0)),
            pl.BlockSpec((None, d, tn), lambda l, j: (l, 0, j)),
            pl.BlockSpec((None, 1, tn), lambda l, j: (l, 0, j)),
        ],
        out_specs=pl.BlockSpec((None, rows, tn), lambda l, j: (l, 0, j)),
        out_shape=jax.ShapeDtypeStruct((depth, rows, n6), F32),
        compiler_params=_cparams(("arbitrary", "arbitrary")),
    )(c_all, w_ada, b_ada.reshape(depth, 1, n6))


class _Tiling:
    def __init__(self, b, t, d, tile=TOKEN_TILE):
        self.b, self.t, self.d = b, t, d
        self.n = b * t
        self.per_batch = t % LANES == 0
        if self.per_batch:
            self.tm = next(m for m in (tile, 256, LANES) if m <= tile and t % m == 0)
        else:
            self.tm = self.n if self.n <= tile else tile
            assert self.n % self.tm == 0 and self.tm % SUBLANES == 0
        self.tiles = self.n // self.tm

    def mod_array(self, mod_l):
        if self.per_batch:
            return mod_l.reshape(self.b, 1, mod_l.shape[-1])
        return jnp.repeat(mod_l, self.t, axis=0)

    def mod_spec(self, k, grid_rank):
        d, tm, t = self.d, self.tm, self.t
        if self.per_batch:
            if grid_rank == 1:
                return pl.BlockSpec((None, 1, d), lambda i: ((i * tm) // t, 0, k))
            return pl.BlockSpec((None, 1, d), lambda i, j: ((i * tm) // t, 0, k))
        if grid_rank == 1:
            return pl.BlockSpec((tm, d), lambda i: (i, k))
        return pl.BlockSpec((tm, d), lambda i, j: (i, k))


def _in_proj_kernel(x_ref, nw_ref, sc_ref, sh_ref, w_ref, z_ref, zs_ref, h_scr, prev_scr, *, nh, tiles_per_piece):
    j = pl.program_id(1)
    n_tiles = pl.num_programs(1) - 1
    tn = z_ref.shape[1]
    width = tn + LANES
    groups = ((4, 0, 2 * nh), (12, 2 * nh, 2 * nh), (16, 4 * nh, GLA_RANK))

    def emit(head):
        piece = (j - 1) // tiles_per_piece
        amount = jnp.where(piece < 4, 0, width - jnp.where(piece < 12, 2 * nh, 4 * nh))
        zz = jnp.concatenate([prev_scr[...], head], axis=1)
        z_ref[...] = pltpu.roll(zz, amount, axis=1)[:, :tn]
        for hi, s, n_gate in groups:
            @pl.when(j == hi * tiles_per_piece)
            def _():
                lane = _iota2(head.shape, 1)
                zs_ref[...] = jnp.where((lane >= s) & (lane < s + n_gate), head, zs_ref[...])

    @pl.when(j == 0)
    def _():
        _rms_mod_rows(x_ref, nw_ref, sc_ref, sh_ref, h_scr)
        zs_ref[...] = jnp.zeros_like(zs_ref)
        prev_scr[...] = _dot_nt(h_scr[...], w_ref[...])

    @pl.when((j > 0) & (j < n_tiles))
    def _():
        za = _dot_nt(h_scr[...], w_ref[...])
        emit(za[:, :LANES])
        prev_scr[...] = za

    @pl.when(j == n_tiles)
    def _():
        emit(_dot_nt(h_scr[...], w_ref[:LANES, :]))


def _in_proj(til, x, nw, mod_arr, w_in_b, layer):
    if not til.per_batch:
        til = _Tiling(til.b, til.t, til.d, tile=256)
    n, d, tm = til.n, til.d, til.tm
    dg = d // N_MIXERS
    nh = dg // HEAD_DIM
    nbig = 16 * dg
    tn = min(1024, dg)
    assert 4 * nh + GLA_RANK <= LANES and w_in_b.shape[1] == nbig + 4 * nh + GLA_RANK
    return pl.pallas_call(
        functools.partial(_in_proj_kernel, nh=nh, tiles_per_piece=dg // tn),
        grid=(til.tiles, nbig // tn + 1),
        in_specs=[
            pl.BlockSpec((tm, d), lambda i, j: (i, 0)),
            pl.BlockSpec((None, 1, d), lambda i, j: (layer, 0, 0)),
            til.mod_spec(1, 2),
            til.mod_spec(0, 2),
            pl.BlockSpec((None, tn, d), lambda i, j: (layer, j, 0)),
        ],
        out_specs=[
            pl.BlockSpec((tm, tn), lambda i, j: (i, jnp.maximum(j - 1, 0))),
            pl.BlockSpec((tm, LANES), lambda i, j: (i, 0)),
        ],
        out_shape=[jax.ShapeDtypeStruct((n, nbig), F32), jax.ShapeDtypeStruct((n, LANES), F32)],
        scratch_shapes=[pltpu.VMEM((tm, d), BF16), pltpu.VMEM((tm, tn), F32)],
        compiler_params=_cparams(("arbitrary", "arbitrary")),
    )(x, nw, mod_arr, mod_arr, w_in_b)


def _mlstm_kernel(*refs, bb, nh, c, n_valid, has_init, n_alias):
    q_ref, k_ref, v_ref, g_ref, zc_ref, zr_ref, br_ref, bc_ref, on_ref = refs[:9]
    n_in = 9
    if has_init:
        c0_ref, n0_ref, m0_ref = refs[9:12]
        n_in = 12
    o_ref, cs_ref, ns_ref, ms_ref = refs[n_in + n_alias:]

    @pl.when(pl.program_id(1) == 0)
    def _():
        if has_init:
            cs_ref[...] = c0_ref[...]
            ns_ref[...] = n0_ref[...]
            ms_ref[...] = m0_ref[...]
        else:
            cs_ref[...] = jnp.zeros_like(cs_ref)
            ns_ref[...] = jnp.zeros_like(ns_ref)
            ms_ref[...] = jnp.zeros_like(ms_ref)

    hd = HEAD_DIM
    scale = hd ** -0.5
    ti = _iota2((c, c), 0)
    si = _iota2((c, c), 1)
    incl = si <= ti
    incl_t = ti <= si
    valid_c = _iota2((c, 1), 0) < n_valid
    valid_r = _iota2((1, c), 1) < n_valid
    def unit(b, h, zc, zr):
        hs = slice(h * hd, (h + 1) * hd)
        q = q_ref[b, :, hs]
        k = k_ref[b, :, hs] * scale
        v = v_ref[b, :, hs]
        ig_c = zc[:, h:h + 1]
        lf_c = _log_sigmoid(zc[:, nh + h:nh + h + 1])
        ig_r = zr[h:h + 1, :]
        lf_r = _log_sigmoid(zr[nh + h:nh + h + 1, :])
        if n_valid < c:
            ig_c = jnp.where(valid_c, ig_c, NEG_BIG)
            lf_c = jnp.where(valid_c, lf_c, 0.0)
            ig_r = jnp.where(valid_r, ig_r, NEG_BIG)
            lf_r = jnp.where(valid_r, lf_r, 0.0)
        cm = cs_ref[b, h]
        nv = ns_ref[b, h:h + 1, :]
        m0 = ms_ref[b, :, h:h + 1]
        f_c, f_r = _cumsum_pair(lf_c, lf_r, incl, incl_t)
        yield
        raw = f_c - f_r + ig_r
        a = f_c + m0
        m_t = jnp.maximum(a, jnp.max(jnp.where(incl, raw, NEG_BIG), axis=1, keepdims=True))
        p = jnp.where(incl, jnp.exp(jnp.where(incl, raw - m_t, 0.0)), 0.0)
        qb = q.astype(BF16)
        kb = k.astype(BF16)
        vb = v.astype(BF16)
        yield
        s = _dot_nt(qb, kb) * p
        inter = jnp.exp(a - m_t)
        qc = _dot(qb, cm.astype(BF16))
        yield
        num = inter * qc + _dot(s.astype(BF16), vb)
        den = inter * jnp.sum(q * nv, axis=1, keepdims=True) + jnp.sum(s, axis=1, keepdims=True)
        m_end = m_t[c - 1:c, :]
        w_end = jnp.exp(f_c[c - 1:c, :] - f_c + ig_c - m_end)
        dec = jnp.exp(a[c - 1:c, :] - m_end)
        kw = w_end * k
        yield
        hh = num / jnp.maximum(jnp.abs(den), jnp.exp(-m_t))
        cs_ref[b, h] = dec * cm + _dot_tn(kw.astype(BF16), vb)
        ns_ref[b, h:h + 1, :] = dec * nv + jnp.sum(kw, axis=0, keepdims=True)
        ms_ref[b, :, h:h + 1] = m_end
        o_ref[b, :, hs] = _merge(hh, _sigmoid(g_ref[b, :, hs]), on_ref[:, hs])

    for b in range(bb):
        zc = zc_ref[b] + br_ref[...]
        zr = zr_ref[b] + bc_ref[...]
        _interleave([unit(b, h, zc, zr) for h in range(nh)])


def _gla_unit(q, k, v, lg, s_mat, tri_b, emit):
    c, hd = q.shape
    sb = min(SUB, c)
    nb = c // sb
    g = _cumsum_rows(lg, tri_b)
    yield
    o = _dot((q * jnp.exp(g)).astype(BF16), s_mat.astype(BF16))

    if nb > 1:
        qparts, kparts = [], []
        for j in range(nb - 1):
            r1 = (j + 1) * sb
            g_end = g[r1 - 1:r1, :]
            qj = q[r1:, :] * jnp.exp(g[r1:, :] - g_end)
            kj = k[j * sb:r1, :] * jnp.exp(g_end - g[j * sb:r1, :])
            qparts.append(jnp.concatenate([jnp.zeros((r1, hd), F32), qj], axis=0).astype(BF16))
            pieces = [kj]
            if j > 0:
                pieces.insert(0, jnp.zeros((j * sb, hd), F32))
            pieces.append(jnp.zeros((c - r1, hd), F32))
            kparts.append(jnp.concatenate(pieces, axis=0).astype(BF16))
        a_off = _dot_nt(jnp.concatenate(qparts, axis=1), jnp.concatenate(kparts, axis=1))
    else:
        a_off = jnp.zeros((c, c), F32)
    yield

    lane = _iota2((sb, c), 1)
    trow = _iota2((sb, 1), 0)
    strips = []
    for i in range(nb):
        r0 = i * sb
        qi, ki, gi = q[r0:r0 + sb, :], k[r0:r0 + sb, :], g[r0:r0 + sb, :]
        strip = a_off[r0:r0 + sb, :]
        for s in range(sb):
            msk = trow >= s
            w = jnp.where(msk, jnp.exp(jnp.where(msk, gi - gi[s:s + 1, :], 0.0)), 0.0)
            col = jnp.sum(qi * ki[s:s + 1, :] * w, axis=1, keepdims=True)
            strip = jnp.where(lane == r0 + s, col, strip)
        strips.append(strip)
        yield
    a = strips[0] if nb == 1 else jnp.concatenate(strips, axis=0)
    vb = v.astype(BF16)
    o = o + _dot(a.astype(BF16), vb)

    g_end = g[c - 1:c, :]
    kt = (k * jnp.exp(g_end - g)).astype(BF16)
    yield
    emit(o, _row_to_col(jnp.exp(g_end)) * s_mat + _dot_tn(kt, vb))


def _gla_kernel(*refs, bb, nh, c, n_valid, has_init, n_alias, kind, layer):
    if kind == "hgrn":
        q_ref, k_ref, v_ref, g_ref, par_ref, on_ref = refs[:6]
        rest = refs[6:]
    else:
        q_ref, k_ref, v_ref, g_ref, zc_ref, wg_ref, bg_ref, on_ref = refs[:8]
        rest = refs[8:]
    if has_init:
        s0_ref = rest[0]
        rest = rest[1:]
    o_ref, st_ref = rest[n_alias:]

    @pl.when(pl.program_id(1) == 0)
    def _():
        if has_init:
            st_ref[...] = s0_ref[...]
        else:
            st_ref[...] = jnp.zeros_like(st_ref)

    hd = HEAD_DIM
    tri_b = (_iota2((c, c), 1) <= _iota2((c, c), 0)).astype(BF16)
    valid_c = _iota2((c, 1), 0) < n_valid
    if kind == "hgrn":
        lbp = par_ref[...]
        ex = jnp.exp(lbp - jnp.max(lbp, axis=0, keepdims=True))
        lbs = ex / jnp.sum(ex, axis=0, keepdims=True)
        lb = jnp.zeros_like(lbs[0:1, :])
        for j in range(1, layer + 1):
            lb = lb + lbs[j:j + 1, :]
    def finish(b, h, hs):
        def emit(o, s_new):
            st_ref[b, h] = s_new
            o_ref[b, :, hs] = _merge(o, _silu(g_ref[b, :, hs]), on_ref[:, hs])
        return emit

    for b in range(bb):
        if kind == "gla":
            gate_in = _dot(zc_ref[b].astype(BF16), wg_ref[...].astype(BF16)) + bg_ref[...]
        units = []
        for h in range(nh):
            hs = slice(h * hd, (h + 1) * hd)
            if kind == "hgrn":
                fg = k_ref[b, :, hs]
                lbh = lb[:, hs]
                f = lbh + (1.0 - lbh) * _sigmoid(fg)
                lg = jnp.log(jnp.maximum(f, MIN_FORGET))
                k = (1.0 - lbh) * _sigmoid(-fg)
                q = _silu(q_ref[b, :, hs])
            else:
                lg = _log_sigmoid(gate_in[:, hs]) / GLA_TAU
                k = k_ref[b, :, hs]
                q = q_ref[b, :, hs] * (hd ** -0.5)
            v = v_ref[b, :, hs]
            if n_valid < c:
                lg = jnp.where(valid_c, lg, 0.0)
                k = jnp.where(valid_c, k, 0.0)
            units.append(_gla_unit(q, k, v, lg, st_ref[b, h], tri_b, finish(b, h, hs)))
        _interleave(units)


def _gdn_kernel(*refs, bb, nh, c, n_valid, has_init, n_alias):
    (q_ref, k_ref, v_ref, g_ref, zc_ref, zr_ref, br_ref, bc_ref, ar_ref, ac_ref, cw_ref, on_ref) = refs[:12]
    n_in = 12
    if has_init:
        s0_ref, cv0_ref = refs[12:14]
        n_in = 14
    o_ref, st_ref, cvo_ref, tail_ref = refs[n_in + n_alias:]
    t_id = pl.program_id(1)
    n_chunks = pl.num_programs(1)
    tail_rows = SUBLANES
    n_buf = CONV_W - 1

    @pl.when(t_id == 0)
    def _():
        tail_ref[...] = jnp.zeros_like(tail_ref)
        if has_init:
            st_ref[...] = s0_ref[...]
            for b in range(bb):
                for pc in range(3):
                    tail_ref[b, pc, tail_rows - n_buf:tail_rows, :] = cv0_ref[b, :, pc, :]
        else:
            st_ref[...] = jnp.zeros_like(st_ref)

    hd = HEAD_DIM
    scale = hd ** -0.5
    ti = _iota2((c, c), 0)
    si = _iota2((c, c), 1)
    incl = si <= ti
    incl_t = ti <= si
    strict = si < ti
    eye = (si == ti).astype(F32)
    valid_c = _iota2((c, 1), 0) < n_valid
    valid_r = _iota2((1, c), 1) < n_valid
    raw_refs = (q_ref, k_ref, v_ref)
    for b in range(bb):
        conv = []
        for pc in range(3):
            u = raw_refs[pc][b]
            ext = jnp.concatenate([tail_ref[b, pc], u], axis=0)
            acc = u * cw_ref[CONV_W - 1:CONV_W, pc, :]
            for j in range(1, CONV_W):
                shifted = pltpu.roll(ext, j, axis=0)[tail_rows:tail_rows + c, :]
                acc = acc + shifted * cw_ref[CONV_W - 1 - j:CONV_W - j, pc, :]
            conv.append(_silu(acc))
            tail_ref[b, pc] = u[c - tail_rows:c, :]

        @pl.when(t_id == n_chunks - 1)
        def _():
            for pc in range(3):
                cvo_ref[b, :, pc, :] = raw_refs[pc][b, n_valid - n_buf:n_valid, :]

        zc = zc_ref[b] + br_ref[...]
        zr = zr_ref[b] + bc_ref[...]
        units = []
        for h in range(nh):
            hs = slice(h * hd, (h + 1) * hd)
            q = conv[0][:, hs]
            k = conv[1][:, hs]
            v = conv[2][:, hs]
            q = q * lax.rsqrt(jnp.sum(q * q, axis=1, keepdims=True) + EPS) * scale
            k = k * lax.rsqrt(jnp.sum(k * k, axis=1, keepdims=True) + EPS)
            ca, cb = 2 * nh + h, 3 * nh + h
            lg_c = -jnp.exp(ar_ref[:, ca:ca + 1]) * _softplus(zc[:, ca:ca + 1])
            lg_r = -jnp.exp(ac_ref[ca:ca + 1, :]) * _softplus(zr[ca:ca + 1, :])
            beta = _sigmoid(zc[:, cb:cb + 1])
            if n_valid < c:
                lg_c = jnp.where(valid_c, lg_c, 0.0)
                lg_r = jnp.where(valid_r, lg_r, 0.0)
                beta = jnp.where(valid_c, beta, 0.0)
            s_mat = st_ref[b, h]

            g_c, g_r = _cumsum_pair(lg_c, lg_r, incl, incl_t)
            eg = jnp.exp(g_c)
            rel = jnp.where(incl, jnp.exp(jnp.where(incl, g_c - g_r, 0.0)), 0.0)
            qb = q.astype(BF16)
            kb = k.astype(BF16)
            kq_s = _dot(jnp.concatenate([kb, qb], axis=0), s_mat.astype(BF16))
            m = jnp.where(strict, beta * rel * _dot_nt(kb, kb), 0.0)
            rhs = beta * (v - eg * kq_s[:c, :])
            qk = (_dot_nt(qb, kb) * rel).astype(BF16)
            units.append(dict(hs=hs, k=k, s_mat=s_mat, g_c=g_c, eg=eg, m=m, rhs=rhs, qk=qk, qs=kq_s[c:, :]))

        b0 = min(INV_BLOCK, c)
        md = [jnp.where((ti // b0) == (si // b0), un["m"], 0.0) for un in units]
        tinv = [eye - x for x in md]
        pw = [_dot_f32(x, x) for x in md]
        n_it = int(math.log2(b0)) - 1
        for it in range(n_it):
            tinv = [t + _dot_f32(t, p) for t, p in zip(tinv, pw)]
            if it < n_it - 1:
                pw = [_dot_f32(p, p) for p in pw]
        blk = b0
        while blk < c:
            below = ((ti // (2 * blk)) == (si // (2 * blk))) & ((ti // blk) != (si // blk))
            tinv = [t - _dot_f32(_dot_f32(t, jnp.where(below, un["m"], 0.0)), t) for t, un in zip(tinv, units)]
            blk *= 2

        for un, t in zip(units, tinv):
            ub = _dot_f32(t, un["rhs"]).astype(BF16)
            o = un["eg"] * un["qs"] + _dot(un["qk"], ub)
            g_end = un["g_c"][c - 1:c, :]
            kd = (jnp.exp(g_end - un["g_c"]) * un["k"]).astype(BF16)
            h = un["hs"].start // hd
            st_ref[b, h] = jnp.exp(g_end) * un["s_mat"] + _dot_tn(kd, ub)
            o_ref[b, :, un["hs"]] = _merge(o, _silu(g_ref[b, :, un["hs"]]), on_ref[:, un["hs"]])


def _mixers(b, t, dg, z_big, zs, states, prev, prm, layer, depth):
    nh = dg // HEAD_DIM
    hd = HEAD_DIM
    n = b * t
    tp = -(-t // SUBLANES) * SUBLANES
    c = min(CHUNK, tp)
    assert tp % c == 0 and (tp == t or tp == c) and t >= CONV_W - 1 and 4 * nh + GLA_RANK <= SMALL_ROWS
    nc = tp // c
    n_valid = c - (tp - t)
    has_init = states is not None
    bb = 4 if (has_init and b % 4 == 0) else 1

    z3 = z_big.reshape(b, t, 16 * dg)
    zs3 = zs.reshape(b, t, LANES)
    if tp != t:
        z3 = jnp.pad(z3, ((0, 0), (0, tp - t), (0, 0)))
        zs3 = jnp.pad(zs3, ((0, 0), (0, tp - t), (0, 0)))
    zr4 = jnp.swapaxes(zs3[:, :, :SMALL_ROWS].reshape(b, nc, c, SMALL_ROWS), 2, 3)

    grid = (b // bb, nc)
    piece = lambda p: pl.BlockSpec((bb, c, dg), lambda i, j: (i, j, p))
    zc_spec = pl.BlockSpec((bb, c, LANES), lambda i, j: (i, j, 0))
    zr_spec = pl.BlockSpec((bb, None, SMALL_ROWS, c), lambda i, j: (i, j, 0, 0))
    full2 = lambda a: pl.BlockSpec(a.shape, lambda i, j: (0,) * a.ndim)
    mat_spec = pl.BlockSpec((None, bb, nh, hd, hd), lambda i, j: (layer, i, 0, 0, 0))
    n_spec = pl.BlockSpec((None, bb, nh, hd), lambda i, j: (layer, i, 0, 0))
    m_spec = pl.BlockSpec((None, bb, 1, nh), lambda i, j: (layer, i, 0, 0))
    cv_spec = pl.BlockSpec((None, bb, CONV_W - 1, 3, dg), lambda i, j: (layer, i, 0, 0, 0))
    mat_shape = jax.ShapeDtypeStruct((depth, b, nh, hd, hd), F32)
    n_shape = jax.ShapeDtypeStruct((depth, b, nh, hd), F32)
    m_shape = jax.ShapeDtypeStruct((depth, b, 1, nh), F32)
    cv_shape = jax.ShapeDtypeStruct((depth, b, CONV_W - 1, 3, dg), F32)
    any_spec = pl.BlockSpec(memory_space=pl.ANY)
    o_spec = pl.BlockSpec((bb, c, dg), lambda i, j: (i, j, 0))
    o_shape = jax.ShapeDtypeStruct((b, tp, dg), BF16)
    cp = _cparams(("arbitrary", "arbitrary"))
    on = prm["out_norm"][layer].reshape(N_MIXERS, 1, dg)
    br, bc, ar, ac = prm["bias_row"][layer], prm["bias_col"][layer], prm["alog_row"][layer], prm["alog_col"][layer]

    def call(body, ins, specs, init, state_specs, state_shapes, prev_arrays, scratch=()):
        ins, specs = list(ins), list(specs)
        if has_init:
            ins += init
            specs += state_specs
        n_alias = 0 if prev_arrays is None else len(prev_arrays)
        aliases = {}
        if n_alias:
            aliases = {len(ins) + k: 1 + k for k in range(n_alias)}
            ins += list(prev_arrays)
            specs += [any_spec] * n_alias
        return pl.pallas_call(
            functools.partial(body, bb=bb, nh=nh, c=c, n_valid=n_valid, has_init=has_init, n_alias=n_alias),
            grid=grid, in_specs=specs, out_specs=[o_spec] + list(state_specs),
            out_shape=[o_shape] + list(state_shapes), scratch_shapes=list(scratch),
            input_output_aliases=aliases, compiler_params=cp,
        )(*ins)

    st = states
    pv = prev
    o_a, m_c, m_n, m_m = call(
        _mlstm_kernel, [z3, z3, z3, z3, zs3, zr4, br, bc, on[0]],
        [piece(0), piece(1), piece(2), piece(3), zc_spec, zr_spec, full2(br), full2(bc), full2(on[0])],
        None if st is None else [st[0], st[1], st[2].reshape(depth, b, 1, nh)],
        [mat_spec, n_spec, m_spec], [mat_shape, n_shape, m_shape],
        None if pv is None else [pv[0], pv[1], pv[2]])

    lbp = prm["hgrn_lb"]
    o_b, s_hgrn = call(
        functools.partial(_gla_kernel, kind="hgrn", layer=layer), [z3, z3, z3, z3, lbp, on[1]],
        [piece(4), piece(5), piece(6), piece(7), full2(lbp), full2(on[1])],
        None if st is None else [st[3]], [mat_spec], [mat_shape], None if pv is None else [pv[3]])

    cw = prm["gdn_conv_w"][layer].reshape(CONV_W, 3, dg)
    o_c, s_gdn, s_conv = call(
        _gdn_kernel, [z3, z3, z3, z3, zs3, zr4, br, bc, ar, ac, cw, on[2]],
        [piece(8), piece(9), piece(10), piece(11), zc_spec, zr_spec, full2(br), full2(bc), full2(ar),
         full2(ac), full2(cw), full2(on[2])],
        None if st is None else [st[4], st[5].reshape(depth, b, CONV_W - 1, 3, dg)],
        [mat_spec, cv_spec], [mat_shape, cv_shape], None if pv is None else [pv[4], pv[5]],
        scratch=[pltpu.VMEM((bb, 3, SUBLANES, dg), F32)])

    wg, bg = prm["gla_w_pad"][layer], prm["gla_b_gate"][layer].reshape(1, dg)
    o_d, s_gla = call(
        functools.partial(_gla_kernel, kind="gla", layer=layer), [z3, z3, z3, z3, zs3, wg, bg, on[3]],
        [piece(12), piece(13), piece(14), piece(15), zc_spec, full2(wg), full2(bg), full2(on[3])],
        None if st is None else [st[6]], [mat_spec], [mat_shape], None if pv is None else [pv[6]])

    outs = [o[:, :t, :].reshape(n, dg) for o in (o_a, o_b, o_c, o_d)]
    return outs, (m_c, m_n, m_m, s_hgrn, s_gdn, s_conv, s_gla)


def _out_proj_kernel(oa_ref, ob_ref, oc_ref, od_ref, w_ref, x_ref, g_ref, y_ref):
    acc = _dot(oa_ref[...], w_ref[0].astype(BF16))
    acc = acc + _dot(ob_ref[...], w_ref[1].astype(BF16))
    acc = acc + _dot(oc_ref[...], w_ref[2].astype(BF16))
    acc = acc + _dot(od_ref[...], w_ref[3].astype(BF16))
    y_ref[...] = x_ref[...] + g_ref[...] * acc


def _out_proj(til, outs, w_out4, x, mod_arr, layer):
    n, d, tm = til.n, til.d, til.tm
    dg = d // N_MIXERS
    tn = min(1024, d)
    o_spec = pl.BlockSpec((tm, dg), lambda i, j: (i, 0))
    if til.per_batch:
        g_spec = pl.BlockSpec((None, 1, tn), lambda i, j: ((i * tm) // til.t, 0, (2 * d) // tn + j))
    else:
        g_spec = pl.BlockSpec((tm, tn), lambda i, j: (i, (2 * d) // tn + j))
    return pl.pallas_call(
        _out_proj_kernel,
        grid=(til.tiles, d // tn),
        in_specs=[o_spec, o_spec, o_spec, o_spec,
                  pl.BlockSpec((None, N_MIXERS, dg, tn), lambda i, j: (layer, 0, 0, j)),
                  pl.BlockSpec((tm, tn), lambda i, j: (i, j)),
                  g_spec],
        out_specs=pl.BlockSpec((tm, tn), lambda i, j: (i, j)),
        out_shape=jax.ShapeDtypeStruct((n, d), F32),
        compiler_params=_cparams(("arbitrary", "arbitrary")),
    )(*outs, w_out4, x, mod_arr)


def _top_desc(s, count):
    rows = float(s.shape[0])
    ri = _iota2(s.shape, 0).astype(F32)
    vals = []
    for r in range(count):
        mx = jnp.max(s, axis=0, keepdims=True)
        vals.append(mx)
        if r < count - 1:
            first = jnp.min(jnp.where(s == mx, ri, rows), axis=0, keepdims=True)
            s = jnp.where(ri == first, -jnp.inf, s)
    return vals


def _sort16_network():
    def merge(lo, hi, r):
        step = r * 2
        if step < hi - lo:
            yield from merge(lo, hi, step)
            yield from merge(lo + r, hi, step)
            yield from [(i, i + r) for i in range(lo + r, hi - r, step)]
        else:
            yield (lo, lo + r)

    def sort(lo, hi):
        if hi - lo >= 1:
            mid = lo + (hi - lo) // 2
            yield from sort(lo, mid)
            yield from sort(mid + 1, hi)
            yield from merge(lo, hi, 1)

    return list(sort(0, PEER_TOPK - 1))


def _bitonic_merge16():
    out, s = [], PEER_TOPK // 2
    while s >= 1:
        out += [(i, i + s) for i in range(PEER_TOPK) if (i & s) == 0]
        s //= 2
    return out


def _top16_sorted(s):
    assert s.shape[0] == PEER_TOPK * SUBLANES
    slabs = [s[SUBLANES * k:SUBLANES * (k + 1), :] for k in range(PEER_TOPK)]

    def exchange(net):
        for i, j in net:
            slabs[i], slabs[j] = jnp.maximum(slabs[i], slabs[j]), jnp.minimum(slabs[i], slabs[j])

    exchange(_sort16_network())
    merge_net = _bitonic_merge16()
    for shift in (4, 2, 1):
        other = [pltpu.roll(x, shift, axis=0) for x in slabs]
        for k in range(PEER_TOPK):
            slabs[k] = jnp.maximum(slabs[k], other[PEER_TOPK - 1 - k])
        exchange(merge_net)
    return [x[0:1, :] for x in slabs]


def _cand_pairs():
    return [(a, b) for a in range(PEER_TOPK) for b in range(PEER_TOPK) if (a + 1) * (b + 1) <= PEER_TOPK]


def _route_kernel(x_ref, nw_ref, sc_ref, sh_ref, wq_ref, key_ref,
                  h2t_ref, s1_ref, s2_ref, e1_ref, e2_ref, tau_ref, cand_ref, h2_scr):
    @pl.when(pl.program_id(1) == 0)
    def _():
        h2 = _rms_mod(x_ref[...], nw_ref[...], sc_ref[...], sh_ref[...])
        h2_scr[...] = h2.astype(BF16)
        h2t_ref[...] = h2.T.astype(BF16)

    half = PEER_KEYS
    heads = s1_ref.shape[0]
    q = _dot(h2_scr[...], wq_ref[...])
    pairs = _cand_pairs()
    for hh in range(heads):
        q1 = q[:, (2 * hh) * half:(2 * hh + 1) * half]
        q2 = q[:, (2 * hh + 1) * half:(2 * hh + 2) * half]
        s1 = _dot_nt(key_ref[hh, 0].astype(BF16), q1.astype(BF16))
        s2 = _dot_nt(key_ref[hh, 1].astype(BF16), q2.astype(BF16))
        v1 = _top16_sorted(s1)
        v2 = _top16_sorted(s2)
        cand_ref[hh] = jnp.full(cand_ref.shape[1:], -jnp.inf, F32)
        for r, (a, b) in enumerate(pairs):
            cand_ref[hh, r:r + 1, :] = v1[a] + v2[b]
        best = _top_desc(cand_ref[hh], PEER_TOPK)
        zsum = jnp.zeros_like(best[0])
        for r in range(PEER_TOPK):
            zsum = zsum + jnp.exp(best[r] - best[0])
        s1_ref[hh] = s1
        s2_ref[hh] = s2
        e1_ref[hh] = jnp.exp(s1 - v1[0]) / zsum
        e2_ref[hh] = jnp.exp(s2 - v2[0])
        tau_ref[hh] = best[PEER_TOPK - 1]


def _route(til, x1, nw, mod_arr, w_q, sub_keys, layer):
    n, d, tm = til.n, til.d, til.tm
    qd = w_q.shape[2] // PEER_HEADS
    n_cand = -(-len(_cand_pairs()) // SUBLANES) * SUBLANES
    hb = ROUTE_HEADS
    tok = pl.BlockSpec((hb, PEER_KEYS, tm), lambda i, h: (h, 0, i))
    tok_shape = jax.ShapeDtypeStruct((PEER_HEADS, PEER_KEYS, n), F32)
    return pl.pallas_call(
        _route_kernel,
        grid=(til.tiles, PEER_HEADS // hb),
        in_specs=[
            pl.BlockSpec((tm, d), lambda i, h: (i, 0)),
            pl.BlockSpec((None, 1, d), lambda i, h: (layer, 0, 0)),
            til.mod_spec(4, 2),
            til.mod_spec(3, 2),
            pl.BlockSpec((None, d, hb * qd), lambda i, h: (layer, 0, h)),
            pl.BlockSpec((None, hb, 2, PEER_KEYS, qd // 2), lambda i, h: (layer, h, 0, 0, 0)),
        ],
        out_specs=[
            pl.BlockSpec((d, tm), lambda i, h: (0, i)),
            tok, tok, tok, tok,
            pl.BlockSpec((hb, 1, tm), lambda i, h: (h, 0, i)),
        ],
        out_shape=[jax.ShapeDtypeStruct((d, n), BF16), tok_shape, tok_shape, tok_shape, tok_shape,
                   jax.ShapeDtypeStruct((PEER_HEADS, 1, n), F32)],
        scratch_shapes=[pltpu.VMEM((hb, n_cand, tm), F32), pltpu.VMEM((tm, d), BF16)],
        compiler_params=_cparams(("arbitrary", "arbitrary")),
    )(x1, nw, mod_arr, mod_arr, w_q, sub_keys)


def _peer_kernel(h2t_ref, u_ref, vt_ref, s1_ref, s2_ref, e1_ref, e2_ref, tau_ref, o_ref, act_scr, p_scr, *, te, n_et):
    e = pl.program_id(1)

    @pl.when(e == 0)
    def _():
        o_ref[...] = jnp.zeros_like(o_ref)
        act_scr[...] = jnp.zeros_like(act_scr)
        p_scr[...] = jnp.zeros_like(p_scr)

    o_ref[...] += _dot(vt_ref[...], p_scr[...])

    groups = te // PEER_KEYS
    tile = jnp.clip(e - 1, 0, n_et - 1)
    for ii in range(groups):
        rows = slice(ii * PEER_KEYS, (ii + 1) * PEER_KEYS)
        act = act_scr[rows, :]
        gel = 0.5 * act * (1.0 + lax.erf(act * (2.0 ** -0.5)))
        row = tile * groups + ii
        acc = jnp.zeros(act.shape, F32)
        for h in range(PEER_HEADS):
            sm = s1_ref[h, pl.ds(row, 1), :] + s2_ref[h]
            sel = jnp.where(sm >= tau_ref[h], e2_ref[h], 0.0)
            acc = acc + sel * e1_ref[h, pl.ds(row, 1), :]
        p_scr[rows, :] = (acc * gel).astype(BF16)

    act_scr[...] = _dot(u_ref[...].astype(BF16), h2t_ref[...])


def _peer(til, h2, routing, u_tab, vt_tab, layer):
    n, d, tm = til.n, til.d, til.tm
    ne = u_tab.shape[1]
    te = EXPERT_TILE
    n_et = ne // te
    once = pl.Buffered(1)
    tok = pl.BlockSpec((PEER_HEADS, PEER_KEYS, tm), lambda i, e: (0, 0, i), pipeline_mode=once)
    return pl.pallas_call(
        functools.partial(_peer_kernel, te=te, n_et=n_et),
        grid=(til.tiles, n_et + 2),
        in_specs=[
            pl.BlockSpec((d, tm), lambda i, e: (0, i), pipeline_mode=once),
            pl.BlockSpec((None, te, d), lambda i, e: (layer, jnp.minimum(e, n_et - 1), 0)),
            pl.BlockSpec((None, d, te), lambda i, e: (layer, 0, jnp.clip(e - 2, 0, n_et - 1))),
            tok, tok, tok, tok,
            pl.BlockSpec((PEER_HEADS, 1, tm), lambda i, e: (0, 0, i), pipeline_mode=once),
        ],
        out_specs=pl.BlockSpec((d, tm), lambda i, e: (0, i), pipeline_mode=once),
        out_shape=jax.ShapeDtypeStruct((d, n), F32),
        scratch_shapes=[pltpu.VMEM((te, tm), F32), pltpu.VMEM((te, tm), BF16)],
        compiler_params=pltpu.CompilerParams(dimension_semantics=("arbitrary", "arbitrary"),
                                             vmem_limit_bytes=PEER_VMEM_LIMIT),
    )(h2, u_tab, vt_tab, *routing)


def _transpose_cast_kernel(v_ref, o_ref):
    o_ref[...] = v_ref[...].T.astype(o_ref.dtype)


def _transpose_cast(v_tab):
    depth, ne, d = v_tab.shape
    te = EXPERT_TILE
    return pl.pallas_call(
        _transpose_cast_kernel,
        grid=(depth, ne // te),
        in_specs=[pl.BlockSpec((None, te, d), lambda l, e: (l, e, 0))],
        out_specs=pl.BlockSpec((None, d, te), lambda l, e: (l, 0, e)),
        out_shape=jax.ShapeDtypeStruct((depth, d, ne), BF16),
        compiler_params=_cparams(("arbitrary", "arbitrary")),
    )(v_tab)


def _residual_kernel(x_ref, pt_ref, g_ref, o_ref):
    o_ref[...] = x_ref[...] + g_ref[...] * pt_ref[...].T


def _residual_norm_kernel(x_ref, pt_ref, g_ref, nw_ref, o_ref):
    x = x_ref[...] + g_ref[...] * pt_ref[...].T
    o_ref[...] = x * lax.rsqrt(jnp.mean(x * x, axis=-1, keepdims=True) + EPS) * nw_ref[...]


def _residual(til, x1, p, mod_arr, final_norm):
    til = _Tiling(til.b, til.t, til.d, tile=256)
    n, d, tm = til.n, til.d, til.tm
    row = pl.BlockSpec((tm, d), lambda i: (i, 0))
    ins = [x1, p, mod_arr]
    specs = [row, pl.BlockSpec((d, tm), lambda i: (0, i)), til.mod_spec(5, 1)]
    body = _residual_kernel
    if final_norm is not None:
        ins.append(final_norm.reshape(1, d))
        specs.append(pl.BlockSpec((1, d), lambda i: (0, 0)))
        body = _residual_norm_kernel
    return pl.pallas_call(
        body, grid=(til.tiles,), in_specs=specs, out_specs=row,
        out_shape=jax.ShapeDtypeStruct((n, d), F32),
        compiler_params=_cparams(("arbitrary",)),
    )(*ins)


def _trunk(x, mod, states, prm):
    b, t, d = x.shape
    depth = mod.shape[0]
    dg = d // N_MIXERS
    til = _Tiling(b, t, d)
    xf = x.reshape(b * t, d)
    new = None
    for l in range(depth):
        mod_arr = til.mod_array(mod[l])
        z_big, zs = _in_proj(til, xf, prm["norm_mix"], mod_arr, prm["w_in_b"], l)
        outs, new = _mixers(b, t, dg, z_big, zs, states, new, prm, l, depth)
        x1 = _out_proj(til, outs, prm["w_out4"], xf, mod_arr, l)
        h2, *routing = _route(til, x1, prm["norm_ffn"], mod_arr, prm["w_q"], prm["peer_sub_keys"], l)
        p = _peer(til, h2, routing, prm["peer_u"], prm["peer_vt"], l)
        xf = _residual(til, x1, p, mod_arr, prm["final_norm"] if l == depth - 1 else None)
    m_c, m_n, m_m, s_hgrn, s_gdn, s_conv, s_gla = new
    nh = dg // HEAD_DIM
    new_states = [m_c, m_n, m_m.reshape(depth, b, nh), s_hgrn, s_gdn,
                  s_conv.reshape(depth, b, CONV_W - 1, 3 * dg), s_gla]
    return xf.reshape(b, t, d), new_states


def _prepare(mlstm_b_i, mlstm_b_f, gdn_a_log, gdn_dt_bias, gla_w_gate, d):
    depth = mlstm_b_i.shape[0]
    dg = d // N_MIXERS
    nh = dg // HEAD_DIM
    n_small = 4 * nh + GLA_RANK
    zeros = lambda k: jnp.zeros((depth, k), F32)
    bias = jnp.concatenate([mlstm_b_i, mlstm_b_f, gdn_dt_bias, zeros(LANES - 3 * nh)], axis=1)
    alog = jnp.concatenate([zeros(2 * nh), gdn_a_log, zeros(LANES - 3 * nh)], axis=1)
    gla_w_pad = jnp.concatenate(
        [jnp.zeros((depth, 4 * nh, dg), F32), gla_w_gate, jnp.zeros((depth, LANES - n_small, dg), F32)], axis=1)
    return dict(
        bias_row=bias.reshape(depth, 1, LANES), bias_col=bias[:, :SMALL_ROWS].reshape(depth, SMALL_ROWS, 1),
        alog_row=alog.reshape(depth, 1, LANES), alog_col=alog[:, :SMALL_ROWS].reshape(depth, SMALL_ROWS, 1),
        gla_w_pad=gla_w_pad)


def kernel(x_prompt, x_sample, c_prompt, c_sample, state_mlstm_C, state_mlstm_n, state_mlstm_m, state_hgrn, state_gdn, state_gdn_conv, state_gla, w_ada, b_ada, norm_mix, norm_ffn, w_in, mlstm_b_i, mlstm_b_f, hgrn_lb, gdn_conv_w, gdn_a_log, gdn_dt_bias, gla_w_gate, gla_b_gate, out_norm, w_out, peer_w_q, peer_sub_keys, peer_u, peer_v, final_norm):
    depth, d = norm_mix.shape
    dg = d // N_MIXERS
    assert d % (N_MIXERS * HEAD_DIM) == 0
    prm = _prepare(mlstm_b_i, mlstm_b_f, gdn_a_log, gdn_dt_bias, gla_w_gate, d)
    prm.update(
        norm_mix=norm_mix.reshape(depth, 1, d), norm_ffn=norm_ffn.reshape(depth, 1, d), hgrn_lb=hgrn_lb,
        gdn_conv_w=gdn_conv_w, gla_b_gate=gla_b_gate, out_norm=out_norm, final_norm=final_norm,
        peer_sub_keys=peer_sub_keys, w_in_b=jnp.swapaxes(w_in, 1, 2).astype(BF16),
        w_out4=w_out.reshape(depth, N_MIXERS, dg, d),
        w_q=peer_w_q.astype(BF16), peer_u=peer_u,
        peer_vt=_transpose_cast(peer_v))

    bp, bs = c_prompt.shape[0], c_sample.shape[0]
    rows = -(-(bp + bs) // SUBLANES) * SUBLANES
    c_all = jnp.concatenate([c_prompt, c_sample, jnp.zeros((rows - bp - bs, d), F32)], axis=0)
    mod = _ada(c_all, w_ada, b_ada)

    y_prompt, p_states = _trunk(x_prompt, mod[:, :bp], None, prm)
    past = (state_mlstm_C, state_mlstm_n, state_mlstm_m, state_hgrn, state_gdn, state_gdn_conv, state_gla)
    y_sample, s_states = _trunk(x_sample, mod[:, bp:bp + bs], past, prm)
    return (y_prompt, y_sample, *p_states, *s_states)
```

```python
import functools
import math

import jax
import jax.numpy as jnp
from jax import lax
from jax.experimental import pallas as pl
from jax.experimental.pallas import tpu as pltpu

F32 = jnp.float32
BF16 = jnp.bfloat16

HEAD_DIM = 256
N_MIXERS = 4
CONV_W = 4
GLA_RANK = 16
GLA_TAU = 16.0
PEER_HEADS = 8
PEER_KEYS = 128
PEER_TOPK = 16
N_MOD = 6
EPS = 1e-6
NEG_BIG = -1e30
MIN_FORGET = 1e-6

LANES = 128
SUBLANES = 8
SMALL_ROWS = 32
CHUNK = 128
SUB = 8
INV_BLOCK = 16
TOKEN_TILE = 512
EXPERT_TILE = 512
ROUTE_HEADS = 2
VMEM_LIMIT = 56 * 1024 * 1024
PEER_VMEM_LIMIT = 60 * 1024 * 1024


def _cparams(sem):
    return pltpu.CompilerParams(dimension_semantics=sem, vmem_limit_bytes=VMEM_LIMIT)


def _dot(a, b):
    return jnp.dot(a, b, preferred_element_type=F32)


def _dot_nt(a, b):
    return lax.dot_general(a, b, (((1,), (1,)), ((), ())), preferred_element_type=F32)


def _dot_tn(a, b):
    return lax.dot_general(a, b, (((0,), (0,)), ((), ())), preferred_element_type=F32)


def _split2(a):
    hi = a.astype(BF16)
    return hi, (a - hi.astype(F32)).astype(BF16)


def _dot_f32(a, b):
    a_hi, a_lo = _split2(a)
    b_hi, b_lo = _split2(b)
    return _dot(a_hi, b_hi) + (_dot(a_hi, b_lo) + _dot(a_lo, b_hi))


def _sigmoid(x):
    return 1.0 / (1.0 + jnp.exp(-x))


def _silu(x):
    return x * _sigmoid(x)


def _log_sigmoid(x):
    return jnp.minimum(x, 0.0) - jnp.log1p(jnp.exp(-jnp.abs(x)))


def _softplus(x):
    return jnp.maximum(x, 0.0) + jnp.log1p(jnp.exp(-jnp.abs(x)))


def _rms_mod(x, nw, sc, sh):
    y = x * lax.rsqrt(jnp.mean(x * x, axis=-1, keepdims=True) + EPS) * nw
    return y * (1.0 + sc) + sh


def _rms_mod_rows(x_ref, nw_ref, sc_ref, sh_ref, out_ref, rows=LANES):
    tm = x_ref.shape[0]
    step = rows if tm % rows == 0 else tm
    for r in range(0, tm, step):
        sl = slice(r, r + step)
        sc = sc_ref[...] if sc_ref.shape[0] == 1 else sc_ref[sl, :]
        sh = sh_ref[...] if sh_ref.shape[0] == 1 else sh_ref[sl, :]
        out_ref[sl, :] = _rms_mod(x_ref[sl, :], nw_ref[...], sc, sh).astype(out_ref.dtype)


def _merge(h, gate, onorm):
    hn = h * lax.rsqrt(jnp.mean(h * h, axis=-1, keepdims=True) + EPS)
    return (hn * onorm * gate).astype(BF16)


def _iota2(shape, dim):
    return lax.broadcasted_iota(jnp.int32, shape, dim)


def _row_to_col(r):
    n = r.shape[1]
    eye = _iota2((n, n), 0) == _iota2((n, n), 1)
    return jnp.sum(jnp.where(eye, r, 0.0), axis=1, keepdims=True)


def _interleave(units):
    units = list(units)
    while units:
        alive = []
        for u in units:
            try:
                next(u)
                alive.append(u)
            except StopIteration:
                pass
        units = alive


def _cumsum_pair(x_c, x_r, incl, incl_t):
    f_c = jnp.sum(jnp.where(incl, x_r, 0.0), axis=1, keepdims=True)
    f_r = jnp.sum(jnp.where(incl_t, x_c, 0.0), axis=0, keepdims=True)
    return f_c, f_r


def _cumsum_rows(x, tri_b):
    hi = x.astype(BF16)
    r1 = x - hi.astype(F32)
    mid = r1.astype(BF16)
    lo = (r1 - mid.astype(F32)).astype(BF16)
    return _dot(tri_b, hi) + _dot(tri_b, mid) + _dot(tri_b, lo)


def _ada_kernel(c_ref, w_ref, b_ref, o_ref):
    cs = _silu(c_ref[...]).astype(BF16)
    o_ref[...] = _dot(cs, w_ref[...].astype(BF16)) + b_ref[...]


def _ada(c_all, w_ada, b_ada):
    depth, d, n6 = w_ada.shape
    rows = c_all.shape[0]
    tn = 512
    return pl.pallas_call(
        _ada_kernel,
        grid=(depth, n6 // tn),
        in_specs=[
            pl.BlockSpec((rows, d), lambda l, j: (0, 0)),
            pl.BlockSpec((None, d, tn), lambda l, j: (l, 0, j)),
            pl.BlockSpec((None, 1, tn), lambda l, j: (l, 0, j)),
        ],
        out_specs=pl.BlockSpec((None, rows, tn), lambda l, j: (l, 0, j)),
        out_shape=jax.ShapeDtypeStruct((depth, rows, n6), F32),
        compiler_params=_cparams(("arbitrary", "arbitrary")),
    )(c_all, w_ada, b_ada.reshape(depth, 1, n6))


class _Tiling:
    def __init__(self, b, t, d, tile=TOKEN_TILE):
        self.b, self.t, self.d = b, t, d
        self.n = b * t
        self.per_batch = t % LANES == 0
        if self.per_batch:
            self.tm = next(m for m in (tile, 256, LANES) if m <= tile and t % m == 0)
        else:
            self.tm = self.n if self.n <= tile else tile
            assert self.n % self.tm == 0 and self.tm % SUBLANES == 0
        self.tiles = self.n // self.tm

    def mod_array(self, mod_l):
        if self.per_batch:
            return mod_l.reshape(self.b, 1, mod_l.shape[-1])
        return jnp.repeat(mod_l, self.t, axis=0)

    def mod_spec(self, k, grid_rank):
        d, tm, t = self.d, self.tm, self.t
        if self.per_batch:
            if grid_rank == 1:
                return pl.BlockSpec((None, 1, d), lambda i: ((i * tm) // t, 0, k))
            return pl.BlockSpec((None, 1, d), lambda i, j: ((i * tm) // t, 0, k))
        if grid_rank == 1:
            return pl.BlockSpec((tm, d), lambda i: (i, k))
        return pl.BlockSpec((tm, d), lambda i, j: (i, k))


def _in_proj_kernel(x_ref, nw_ref, sc_ref, sh_ref, w_ref, z_ref, zs_ref, h_scr, prev_scr, *, nh, tiles_per_piece):
    j = pl.program_id(1)
    n_tiles = pl.num_programs(1) - 1
    tn = z_ref.shape[1]
    width = tn + LANES
    groups = ((4, 0, 2 * nh), (12, 2 * nh, 2 * nh), (16, 4 * nh, GLA_RANK))

    def emit(head):
        piece = (j - 1) // tiles_per_piece
        amount = jnp.where(piece < 4, 0, width - jnp.where(piece < 12, 2 * nh, 4 * nh))
        zz = jnp.concatenate([prev_scr[...], head], axis=1)
        z_ref[...] = pltpu.roll(zz, amount, axis=1)[:, :tn]
        for hi, s, n_gate in groups:
            @pl.when(j == hi * tiles_per_piece)
            def _():
                lane = _iota2(head.shape, 1)
                zs_ref[...] = jnp.where((lane >= s) & (lane < s + n_gate), head, zs_ref[...])

    @pl.when(j == 0)
    def _():
        _rms_mod_rows(x_ref, nw_ref, sc_ref, sh_ref, h_scr)
        zs_ref[...] = jnp.zeros_like(zs_ref)
        prev_scr[...] = _dot_nt(h_scr[...], w_ref[...])

    @pl.when((j > 0) & (j < n_tiles))
    def _():
        za = _dot_nt(h_scr[...], w_ref[...])
        emit(za[:, :LANES])
        prev_scr[...] = za

    @pl.when(j == n_tiles)
    def _():
        emit(_dot_nt(h_scr[...], w_ref[:LANES, :]))


def _in_proj(til, x, nw, mod_arr, w_in_b, layer):
    if not til.per_batch:
        til = _Tiling(til.b, til.t, til.d, tile=256)
    n, d, tm = til.n, til.d, til.tm
    dg = d // N_MIXERS
    nh = dg // HEAD_DIM
    nbig = 16 * dg
    tn = min(1024, dg)
    assert 4 * nh + GLA_RANK <= LANES and w_in_b.shape[1] == nbig + 4 * nh + GLA_RANK
    return pl.pallas_call(
        functools.partial(_in_proj_kernel, nh=nh, tiles_per_piece=dg // tn),
        grid=(til.tiles, nbig // tn + 1),
        in_specs=[
            pl.BlockSpec((tm, d), lambda i, j: (i, 0)),
            pl.BlockSpec((None, 1, d), lambda i, j: (layer, 0, 0)),
            til.mod_spec(1, 2),
            til.mod_spec(0, 2),
            pl.BlockSpec((None, tn, d), lambda i, j: (layer, j, 0)),
        ],
        out_specs=[
            pl.BlockSpec((tm, tn), lambda i, j: (i, jnp.maximum(j - 1, 0))),
            pl.BlockSpec((tm, LANES), lambda i, j: (i, 0)),
        ],
        out_shape=[jax.ShapeDtypeStruct((n, nbig), F32), jax.ShapeDtypeStruct((n, LANES), F32)],
        scratch_shapes=[pltpu.VMEM((tm, d), BF16), pltpu.VMEM((tm, tn), F32)],
        compiler_params=_cparams(("arbitrary", "arbitrary")),
    )(x, nw, mod_arr, mod_arr, w_in_b)


def _mlstm_kernel(*refs, bb, nh, c, n_valid, has_init, n_alias):
    q_ref, k_ref, v_ref, g_ref, zc_ref, zr_ref, br_ref, bc_ref, on_ref = refs[:9]
    n_in = 9
    if has_init:
        c0_ref, n0_ref, m0_ref = refs[9:12]
        n_in = 12
    o_ref, cs_ref, ns_ref, ms_ref = refs[n_in + n_alias:]

    @pl.when(pl.program_id(1) == 0)
    def _():
        if has_init:
            cs_ref[...] = c0_ref[...]
            ns_ref[...] = n0_ref[...]
            ms_ref[...] = m0_ref[...]
        else:
            cs_ref[...] = jnp.zeros_like(cs_ref)
            ns_ref[...] = jnp.zeros_like(ns_ref)
            ms_ref[...] = jnp.zeros_like(ms_ref)

    hd = HEAD_DIM
    scale = hd ** -0.5
    ti = _iota2((c, c), 0)
    si = _iota2((c, c), 1)
    incl = si <= ti
    incl_t = ti <= si
    valid_c = _iota2((c, 1), 0) < n_valid
    valid_r = _iota2((1, c), 1) < n_valid
    def unit(b, h, zc, zr):
        hs = slice(h * hd, (h + 1) * hd)
        q = q_ref[b, :, hs]
        k = k_ref[b, :, hs] * scale
        v = v_ref[b, :, hs]
        ig_c = zc[:, h:h + 1]
        lf_c = _log_sigmoid(zc[:, nh + h:nh + h + 1])
        ig_r = zr[h:h + 1, :]
        lf_r = _log_sigmoid(zr[nh + h:nh + h + 1, :])
        if n_valid < c:
            ig_c = jnp.where(valid_c, ig_c, NEG_BIG)
            lf_c = jnp.where(valid_c, lf_c, 0.0)
            ig_r = jnp.where(valid_r, ig_r, NEG_BIG)
            lf_r = jnp.where(valid_r, lf_r, 0.0)
        cm = cs_ref[b, h]
        nv = ns_ref[b, h:h + 1, :]
        m0 = ms_ref[b, :, h:h + 1]
        f_c, f_r = _cumsum_pair(lf_c, lf_r, incl, incl_t)
        yield
        raw = f_c - f_r + ig_r
        a = f_c + m0
        m_t = jnp.maximum(a, jnp.max(jnp.where(incl, raw, NEG_BIG), axis=1, keepdims=True))
        p = jnp.where(incl, jnp.exp(jnp.where(incl, raw - m_t, 0.0)), 0.0)
        qb = q.astype(BF16)
        kb = k.astype(BF16)
        vb = v.astype(BF16)
        yield
        s = _dot_nt(qb, kb) * p
        inter = jnp.exp(a - m_t)
        qc = _dot(qb, cm.astype(BF16))
        yield
        num = inter * qc + _dot(s.astype(BF16), vb)
        den = inter * jnp.sum(q * nv, axis=1, keepdims=True) + jnp.sum(s, axis=1, keepdims=True)
        m_end = m_t[c - 1:c, :]
        w_end = jnp.exp(f_c[c - 1:c, :] - f_c + ig_c - m_end)
        dec = jnp.exp(a[c - 1:c, :] - m_end)
        kw = w_end * k
        yield
        hh = num / jnp.maximum(jnp.abs(den), jnp.exp(-m_t))
        cs_ref[b, h] = dec * cm + _dot_tn(kw.astype(BF16), vb)
        ns_ref[b, h:h + 1, :] = dec * nv + jnp.sum(kw, axis=0, keepdims=True)
        ms_ref[b, :, h:h + 1] = m_end
        o_ref[b, :, hs] = _merge(hh, _sigmoid(g_ref[b, :, hs]), on_ref[:, hs])

    for b in range(bb):
        zc = zc_ref[b] + br_ref[...]
        zr = zr_ref[b] + bc_ref[...]
        _interleave([unit(b, h, zc, zr) for h in range(nh)])


def _gla_unit(q, k, v, lg, s_mat, tri_b, emit):
    c, hd = q.shape
    sb = min(SUB, c)
    nb = c // sb
    g = _cumsum_rows(lg, tri_b)
    yield
    o = _dot((q * jnp.exp(g)).astype(BF16), s_mat.astype(BF16))

    if nb > 1:
        qparts, kparts = [], []
        for j in range(nb - 1):
            r1 = (j + 1) * sb
            g_end = g[r1 - 1:r1, :]
            qj = q[r1:, :] * jnp.exp(g[r1:, :] - g_end)
            kj = k[j * sb:r1, :] * jnp.exp(g_end - g[j * sb:r1, :])
            qparts.append(jnp.concatenate([jnp.zeros((r1, hd), F32), qj], axis=0).astype(BF16))
            pieces = [kj]
            if j > 0:
                pieces.insert(0, jnp.zeros((j * sb, hd), F32))
            pieces.append(jnp.zeros((c - r1, hd), F32))
            kparts.append(jnp.concatenate(pieces, axis=0).astype(BF16))
        a_off = _dot_nt(jnp.concatenate(qparts, axis=1), jnp.concatenate(kparts, axis=1))
    else:
        a_off = jnp.zeros((c, c), F32)
    yield

    lane = _iota2((sb, c), 1)
    trow = _iota2((sb, 1), 0)
    strips = []
    for i in range(nb):
        r0 = i * sb
        qi, ki, gi = q[r0:r0 + sb, :], k[r0:r0 + sb, :], g[r0:r0 + sb, :]
        strip = a_off[r0:r0 + sb, :]
        for s in range(sb):
            msk = trow >= s
            w = jnp.where(msk, jnp.exp(jnp.where(msk, gi - gi[s:s + 1, :], 0.0)), 0.0)
            col = jnp.sum(qi * ki[s:s + 1, :] * w, axis=1, keepdims=True)
            strip = jnp.where(lane == r0 + s, col, strip)
        strips.append(strip)
        yield
    a = strips[0] if nb == 1 else jnp.concatenate(strips, axis=0)
    vb = v.astype(BF16)
    o = o + _dot(a.astype(BF16), vb)

    g_end = g[c - 1:c, :]
    kt = (k * jnp.exp(g_end - g)).astype(BF16)
    yield
    emit(o, _row_to_col(jnp.exp(g_end)) * s_mat + _dot_tn(kt, vb))


def _gla_kernel(*refs, bb, nh, c, n_valid, has_init, n_alias, kind, layer):
    if kind == "hgrn":
        q_ref, k_ref, v_ref, g_ref, par_ref, on_ref = refs[:6]
        rest = refs[6:]
    else:
        q_ref, k_ref, v_ref, g_ref, zc_ref, wg_ref, bg_ref, on_ref = refs[:8]
        rest = refs[8:]
    if has_init:
        s0_ref = rest[0]
        rest = rest[1:]
    o_ref, st_ref = rest[n_alias:]

    @pl.when(pl.program_id(1) == 0)
    def _():
        if has_init:
            st_ref[...] = s0_ref[...]
        else:
            st_ref[...] = jnp.zeros_like(st_ref)

    hd = HEAD_DIM
    tri_b = (_iota2((c, c), 1) <= _iota2((c, c), 0)).astype(BF16)
    valid_c = _iota2((c, 1), 0) < n_valid
    if kind == "hgrn":
        lbp = par_ref[...]
        ex = jnp.exp(lbp - jnp.max(lbp, axis=0, keepdims=True))
        lbs = ex / jnp.sum(ex, axis=0, keepdims=True)
        lb = jnp.zeros_like(lbs[0:1, :])
        for j in range(1, layer + 1):
            lb = lb + lbs[j:j + 1, :]
    def finish(b, h, hs):
        def emit(o, s_new):
            st_ref[b, h] = s_new
            o_ref[b, :, hs] = _merge(o, _silu(g_ref[b, :, hs]), on_ref[:, hs])
        return emit

    for b in range(bb):
        if kind == "gla":
            gate_in = _dot(zc_ref[b].astype(BF16), wg_ref[...].astype(BF16)) + bg_ref[...]
        units = []
        for h in range(nh):
            hs = slice(h * hd, (h + 1) * hd)
            if kind == "hgrn":
                fg = k_ref[b, :, hs]
                lbh = lb[:, hs]
                f = lbh + (1.0 - lbh) * _sigmoid(fg)
                lg = jnp.log(jnp.maximum(f, MIN_FORGET))
                k = (1.0 - lbh) * _sigmoid(-fg)
                q = _silu(q_ref[b, :, hs])
            else:
                lg = _log_sigmoid(gate_in[:, hs]) / GLA_TAU
                k = k_ref[b, :, hs]
                q = q_ref[b, :, hs] * (hd ** -0.5)
            v = v_ref[b, :, hs]
            if n_valid < c:
                lg = jnp.where(valid_c, lg, 0.0)
                k = jnp.where(valid_c, k, 0.0)
            units.append(_gla_unit(q, k, v, lg, st_ref[b, h], tri_b, finish(b, h, hs)))
        _interleave(units)


def _gdn_kernel(*refs, bb, nh, c, n_valid, has_init, n_alias):
    (q_ref, k_ref, v_ref, g_ref, zc_ref, zr_ref, br_ref, bc_ref, ar_ref, ac_ref, cw_ref, on_ref) = refs[:12]
    n_in = 12
    if has_init:
        s0_ref, cv0_ref = refs[12:14]
        n_in = 14
    o_ref, st_ref, cvo_ref, tail_ref = refs[n_in + n_alias:]
    t_id = pl.program_id(1)
    n_chunks = pl.num_programs(1)
    tail_rows = SUBLANES
    n_buf = CONV_W - 1

    @pl.when(t_id == 0)
    def _():
        tail_ref[...] = jnp.zeros_like(tail_ref)
        if has_init:
            st_ref[...] = s0_ref[...]
            for b in range(bb):
                for pc in range(3):
                    tail_ref[b, pc, tail_rows - n_buf:tail_rows, :] = cv0_ref[b, :, pc, :]
        else:
            st_ref[...] = jnp.zeros_like(st_ref)

    hd = HEAD_DIM
    scale = hd ** -0.5
    ti = _iota2((c, c), 0)
    si = _iota2((c, c), 1)
    incl = si <= ti
    incl_t = ti <= si
    strict = si < ti
    eye = (si == ti).astype(F32)
    valid_c = _iota2((c, 1), 0) < n_valid
    valid_r = _iota2((1, c), 1) < n_valid
    raw_refs = (q_ref, k_ref, v_ref)
    for b in range(bb):
        conv = []
        for pc in range(3):
            u = raw_refs[pc][b]
            ext = jnp.concatenate([tail_ref[b, pc], u], axis=0)
            acc = u * cw_ref[CONV_W - 1:CONV_W, pc, :]
            for j in range(1, CONV_W):
                shifted = pltpu.roll(ext, j, axis=0)[tail_rows:tail_rows + c, :]
                acc = acc + shifted * cw_ref[CONV_W - 1 - j:CONV_W - j, pc, :]
            conv.append(_silu(acc))
            tail_ref[b, pc] = u[c - tail_rows:c, :]

        @pl.when(t_id == n_chunks - 1)
        def _():
            for pc in range(3):
                cvo_ref[b, :, pc, :] = raw_refs[pc][b, n_valid - n_buf:n_valid, :]

        zc = zc_ref[b] + br_ref[...]
        zr = zr_ref[b] + bc_ref[...]
        units = []
        for h in range(nh):
            hs = slice(h * hd, (h + 1) * hd)
            q = conv[0][:, hs]
            k = conv[1][:, hs]
            v = conv[2][:, hs]
            q = q * lax.rsqrt(jnp.sum(q * q, axis=1, keepdims=True) + EPS) * scale
            k = k * lax.rsqrt(jnp.sum(k * k, axis=1, keepdims=True) + EPS)
            ca, cb = 2 * nh + h, 3 * nh + h
            lg_c = -jnp.exp(ar_ref[:, ca:ca + 1]) * _softplus(zc[:, ca:ca + 1])
            lg_r = -jnp.exp(ac_ref[ca:ca + 1, :]) * _softplus(zr[ca:ca + 1, :])
            beta = _sigmoid(zc[:, cb:cb + 1])
            if n_valid < c:
                lg_c = jnp.where(valid_c, lg_c, 0.0)
                lg_r = jnp.where(valid_r, lg_r, 0.0)
                beta = jnp.where(valid_c, beta, 0.0)
            s_mat = st_ref[b, h]

            g_c, g_r = _cumsum_pair(lg_c, lg_r, incl, incl_t)
            eg = jnp.exp(g_c)
            rel = jnp.where(incl, jnp.exp(jnp.where(incl, g_c - g_r, 0.0)), 0.0)
            qb = q.astype(BF16)
            kb = k.astype(BF16)
            kq_s = _dot(jnp.concatenate([kb, qb], axis=0), s_mat.astype(BF16))
            m = jnp.where(strict, beta * rel * _dot_nt(kb, kb), 0.0)
            rhs = beta * (v - eg * kq_s[:c, :])
            qk = (_dot_nt(qb, kb) * rel).astype(BF16)
            units.append(dict(hs=hs, k=k, s_mat=s_mat, g_c=g_c, eg=eg, m=m, rhs=rhs, qk=qk, qs=kq_s[c:, :]))

        b0 = min(INV_BLOCK, c)
        md = [jnp.where((ti // b0) == (si // b0), un["m"], 0.0) for un in units]
        tinv = [eye - x for x in md]
        pw = [_dot_f32(x, x) for x in md]
        n_it = int(math.log2(b0)) - 1
        for it in range(n_it):
            tinv = [t + _dot_f32(t, p) for t, p in zip(tinv, pw)]
            if it < n_it - 1:
                pw = [_dot_f32(p, p) for p in pw]
        blk = b0
        while blk < c:
            below = ((ti // (2 * blk)) == (si // (2 * blk))) & ((ti // blk) != (si // blk))
            tinv = [t - _dot_f32(_dot_f32(t, jnp.where(below, un["m"], 0.0)), t) for t, un in zip(tinv, units)]
            blk *= 2

        for un, t in zip(units, tinv):
            ub = _dot_f32(t, un["rhs"]).astype(BF16)
            o = un["eg"] * un["qs"] + _dot(un["qk"], ub)
            g_end = un["g_c"][c - 1:c, :]
            kd = (jnp.exp(g_end - un["g_c"]) * un["k"]).astype(BF16)
            h = un["hs"].start // hd
            st_ref[b, h] = jnp.exp(g_end) * un["s_mat"] + _dot_tn(kd, ub)
            o_ref[b, :, un["hs"]] = _merge(o, _silu(g_ref[b, :, un["hs"]]), on_ref[:, un["hs"]])


def _mixers(b, t, dg, z_big, zs, states, prev, prm, layer, depth):
    nh = dg // HEAD_DIM
    hd = HEAD_DIM
    n = b * t
    tp = -(-t // SUBLANES) * SUBLANES
    c = min(CHUNK, tp)
    assert tp % c == 0 and (tp == t or tp == c) and t >= CONV_W - 1 and 4 * nh + GLA_RANK <= SMALL_ROWS
    nc = tp // c
    n_valid = c - (tp - t)
    has_init = states is not None
    bb = 4 if (has_init and b % 4 == 0) else 1

    z3 = z_big.reshape(b, t, 16 * dg)
    zs3 = zs.reshape(b, t, LANES)
    if tp != t:
        z3 = jnp.pad(z3, ((0, 0), (0, tp - t), (0, 0)))
        zs3 = jnp.pad(zs3, ((0, 0), (0, tp - t), (0, 0)))
    zr4 = jnp.swapaxes(zs3[:, :, :SMALL_ROWS].reshape(b, nc, c, SMALL_ROWS), 2, 3)

    grid = (b // bb, nc)
    piece = lambda p: pl.BlockSpec((bb, c, dg), lambda i, j: (i, j, p))
    zc_spec = pl.BlockSpec((bb, c, LANES), lambda i, j: (i, j, 0))
    zr_spec = pl.BlockSpec((bb, None, SMALL_ROWS, c), lambda i, j: (i, j, 0, 0))
    full2 = lambda a: pl.BlockSpec(a.shape, lambda i, j: (0,) * a.ndim)
    mat_spec = pl.BlockSpec((None, bb, nh, hd, hd), lambda i, j: (layer, i, 0, 0, 0))
    n_spec = pl.BlockSpec((None, bb, nh, hd), lambda i, j: (layer, i, 0, 0))
    m_spec = pl.BlockSpec((None, bb, 1, nh), lambda i, j: (layer, i, 0, 0))
    cv_spec = pl.BlockSpec((None, bb, CONV_W - 1, 3, dg), lambda i, j: (layer, i, 0, 0, 0))
    mat_shape = jax.ShapeDtypeStruct((depth, b, nh, hd, hd), F32)
    n_shape = jax.ShapeDtypeStruct((depth, b, nh, hd), F32)
    m_shape = jax.ShapeDtypeStruct((depth, b, 1, nh), F32)
    cv_shape = jax.ShapeDtypeStruct((depth, b, CONV_W - 1, 3, dg), F32)
    any_spec = pl.BlockSpec(memory_space=pl.ANY)
    o_spec = pl.BlockSpec((bb, c, dg), lambda i, j: (i, j, 0))
    o_shape = jax.ShapeDtypeStruct((b, tp, dg), BF16)
    cp = _cparams(("arbitrary", "arbitrary"))
    on = prm["out_norm"][layer].reshape(N_MIXERS, 1, dg)
    br, bc, ar, ac = prm["bias_row"][layer], prm["bias_col"][layer], prm["alog_row"][layer], prm["alog_col"][layer]

    def call(body, ins, specs, init, state_specs, state_shapes, prev_arrays, scratch=()):
        ins, specs = list(ins), list(specs)
        if has_init:
            ins += init
            specs += state_specs
        n_alias = 0 if prev_arrays is None else len(prev_arrays)
        aliases = {}
        if n_alias:
            aliases = {len(ins) + k: 1 + k for k in range(n_alias)}
            ins += list(prev_arrays)
            specs += [any_spec] * n_alias
        return pl.pallas_call(
            functools.partial(body, bb=bb, nh=nh, c=c, n_valid=n_valid, has_init=has_init, n_alias=n_alias),
            grid=grid, in_specs=specs, out_specs=[o_spec] + list(state_specs),
            out_shape=[o_shape] + list(state_shapes), scratch_shapes=list(scratch),
            input_output_aliases=aliases, compiler_params=cp,
        )(*ins)

    st = states
    pv = prev
    o_a, m_c, m_n, m_m = call(
        _mlstm_kernel, [z3, z3, z3, z3, zs3, zr4, br, bc, on[0]],
        [piece(0), piece(1), piece(2), piece(3), zc_spec, zr_spec, full2(br), full2(bc), full2(on[0])],
        None if st is None else [st[0], st[1], st[2].reshape(depth, b, 1, nh)],
        [mat_spec, n_spec, m_spec], [mat_shape, n_shape, m_shape],
        None if pv is None else [pv[0], pv[1], pv[2]])

    lbp = prm["hgrn_lb"]
    o_b, s_hgrn = call(
        functools.partial(_gla_kernel, kind="hgrn", layer=layer), [z3, z3, z3, z3, lbp, on[1]],
        [piece(4), piece(5), piece(6), piece(7), full2(lbp), full2(on[1])],
        None if st is None else [st[3]], [mat_spec], [mat_shape], None if pv is None else [pv[3]])

    cw = prm["gdn_conv_w"][layer].reshape(CONV_W, 3, dg)
    o_c, s_gdn, s_conv = call(
        _gdn_kernel, [z3, z3, z3, z3, zs3, zr4, br, bc, ar, ac, cw, on[2]],
        [piece(8), piece(9), piece(10), piece(11), zc_spec, zr_spec, full2(br), full2(bc), full2(ar),
         full2(ac), full2(cw), full2(on[2])],
        None if st is None else [st[4], st[5].reshape(depth, b, CONV_W - 1, 3, dg)],
        [mat_spec, cv_spec], [mat_shape, cv_shape], None if pv is None else [pv[4], pv[5]],
        scratch=[pltpu.VMEM((bb, 3, SUBLANES, dg), F32)])

    wg, bg = prm["gla_w_pad"][layer], prm["gla_b_gate"][layer].reshape(1, dg)
    o_d, s_gla = call(
        functools.partial(_gla_kernel, kind="gla", layer=layer), [z3, z3, z3, z3, zs3, wg, bg, on[3]],
        [piece(12), piece(13), piece(14), piece(15), zc_spec, full2(wg), full2(bg), full2(on[3])],
        None if st is None else [st[6]], [mat_spec], [mat_shape], None if pv is None else [pv[6]])

    outs = [o[:, :t, :].reshape(n, dg) for o in (o_a, o_b, o_c, o_d)]
    return outs, (m_c, m_n, m_m, s_hgrn, s_gdn, s_conv, s_gla)


def _out_proj_kernel(oa_ref, ob_ref, oc_ref, od_ref, w_ref, x_ref, g_ref, y_ref):
    acc = _dot(oa_ref[...], w_ref[0])
    acc = acc + _dot(ob_ref[...], w_ref[1])
    acc = acc + _dot(oc_ref[...], w_ref[2])
    acc = acc + _dot(od_ref[...], w_ref[3])
    y_ref[...] = x_ref[...] + g_ref[...] * acc


def _out_proj(til, outs, w_out4, x, mod_arr, layer):
    n, d, tm = til.n, til.d, til.tm
    dg = d // N_MIXERS
    tn = min(1024, d)
    o_spec = pl.BlockSpec((tm, dg), lambda i, j: (i, 0))
    if til.per_batch:
        g_spec = pl.BlockSpec((None, 1, tn), lambda i, j: ((i * tm) // til.t, 0, (2 * d) // tn + j))
    else:
        g_spec = pl.BlockSpec((tm, tn), lambda i, j: (i, (2 * d) // tn + j))
    return pl.pallas_call(
        _out_proj_kernel,
        grid=(til.tiles, d // tn),
        in_specs=[o_spec, o_spec, o_spec, o_spec,
                  pl.BlockSpec((None, N_MIXERS, dg, tn), lambda i, j: (layer, 0, 0, j)),
                  pl.BlockSpec((tm, tn), lambda i, j: (i, j)),
                  g_spec],
        out_specs=pl.BlockSpec((tm, tn), lambda i, j: (i, j)),
        out_shape=jax.ShapeDtypeStruct((n, d), F32),
        compiler_params=_cparams(("arbitrary", "arbitrary")),
    )(*outs, w_out4, x, mod_arr)


def _top_desc(s, count):
    rows = float(s.shape[0])
    ri = _iota2(s.shape, 0).astype(F32)
    vals = []
    for r in range(count):
        mx = jnp.max(s, axis=0, keepdims=True)
        vals.append(mx)
        if r < count - 1:
            first = jnp.min(jnp.where(s == mx, ri, rows), axis=0, keepdims=True)
            s = jnp.where(ri == first, -jnp.inf, s)
    return vals


def _sort16_network():
    def merge(lo, hi, r):
        step = r * 2
        if step < hi - lo:
            yield from merge(lo, hi, step)
            yield from merge(lo + r, hi, step)
            yield from [(i, i + r) for i in range(lo + r, hi - r, step)]
        else:
            yield (lo, lo + r)

    def sort(lo, hi):
        if hi - lo >= 1:
            mid = lo + (hi - lo) // 2
            yield from sort(lo, mid)
            yield from sort(mid + 1, hi)
            yield from merge(lo, hi, 1)

    return list(sort(0, PEER_TOPK - 1))


def _bitonic_merge16():
    out, s = [], PEER_TOPK // 2
    while s >= 1:
        out += [(i, i + s) for i in range(PEER_TOPK) if (i & s) == 0]
        s //= 2
    return out


def _top16_sorted(s):
    assert s.shape[0] == PEER_TOPK * SUBLANES
    slabs = [s[SUBLANES * k:SUBLANES * (k + 1), :] for k in range(PEER_TOPK)]

    def exchange(net):
        for i, j in net:
            slabs[i], slabs[j] = jnp.maximum(slabs[i], slabs[j]), jnp.minimum(slabs[i], slabs[j])

    exchange(_sort16_network())
    merge_net = _bitonic_merge16()
    for shift in (4, 2, 1):
        other = [pltpu.roll(x, shift, axis=0) for x in slabs]
        for k in range(PEER_TOPK):
            slabs[k] = jnp.maximum(slabs[k], other[PEER_TOPK - 1 - k])
        exchange(merge_net)
    return [x[0:1, :] for x in slabs]


def _cand_pairs():
    return [(a, b) for a in range(PEER_TOPK) for b in range(PEER_TOPK) if (a + 1) * (b + 1) <= PEER_TOPK]


def _route_kernel(x_ref, nw_ref, sc_ref, sh_ref, wq_ref, key_ref,
                  h2t_ref, s1_ref, s2_ref, e1_ref, e2_ref, tau_ref, cand_ref, h2_scr):
    @pl.when(pl.program_id(1) == 0)
    def _():
        h2 = _rms_mod(x_ref[...], nw_ref[...], sc_ref[...], sh_ref[...])
        h2_scr[...] = h2.astype(BF16)
        h2t_ref[...] = h2.T.astype(BF16)

    half = PEER_KEYS
    heads = s1_ref.shape[0]
    q = _dot(h2_scr[...], wq_ref[...])
    pairs = _cand_pairs()
    for hh in range(heads):
        q1 = q[:, (2 * hh) * half:(2 * hh + 1) * half]
        q2 = q[:, (2 * hh + 1) * half:(2 * hh + 2) * half]
        s1 = _dot_nt(key_ref[hh, 0].astype(BF16), q1.astype(BF16))
        s2 = _dot_nt(key_ref[hh, 1].astype(BF16), q2.astype(BF16))
        v1 = _top16_sorted(s1)
        v2 = _top16_sorted(s2)
        cand_ref[hh] = jnp.full(cand_ref.shape[1:], -jnp.inf, F32)
        for r, (a, b) in enumerate(pairs):
            cand_ref[hh, r:r + 1, :] = v1[a] + v2[b]
        best = _top_desc(cand_ref[hh], PEER_TOPK)
        zsum = jnp.zeros_like(best[0])
        for r in range(PEER_TOPK):
            zsum = zsum + jnp.exp(best[r] - best[0])
        s1_ref[hh] = s1
        s2_ref[hh] = s2
        e1_ref[hh] = jnp.exp(s1 - v1[0]) / zsum
        e2_ref[hh] = jnp.exp(s2 - v2[0])
        tau_ref[hh] = best[PEER_TOPK - 1]


def _route(til, x1, nw, mod_arr, w_q, sub_keys, layer):
    n, d, tm = til.n, til.d, til.tm
    qd = w_q.shape[2] // PEER_HEADS
    n_cand = -(-len(_cand_pairs()) // SUBLANES) * SUBLANES
    hb = ROUTE_HEADS
    tok = pl.BlockSpec((hb, PEER_KEYS, tm), lambda i, h: (h, 0, i))
    tok_shape = jax.ShapeDtypeStruct((PEER_HEADS, PEER_KEYS, n), F32)
    return pl.pallas_call(
        _route_kernel,
        grid=(til.tiles, PEER_HEADS // hb),
        in_specs=[
            pl.BlockSpec((tm, d), lambda i, h: (i, 0)),
            pl.BlockSpec((None, 1, d), lambda i, h: (layer, 0, 0)),
            til.mod_spec(4, 2),
            til.mod_spec(3, 2),
            pl.BlockSpec((None, d, hb * qd), lambda i, h: (layer, 0, h)),
            pl.BlockSpec((None, hb, 2, PEER_KEYS, qd // 2), lambda i, h: (layer, h, 0, 0, 0)),
        ],
        out_specs=[
            pl.BlockSpec((d, tm), lambda i, h: (0, i)),
            tok, tok, tok, tok,
            pl.BlockSpec((hb, 1, tm), lambda i, h: (h, 0, i)),
        ],
        out_shape=[jax.ShapeDtypeStruct((d, n), BF16), tok_shape, tok_shape, tok_shape, tok_shape,
                   jax.ShapeDtypeStruct((PEER_HEADS, 1, n), F32)],
        scratch_shapes=[pltpu.VMEM((hb, n_cand, tm), F32), pltpu.VMEM((tm, d), BF16)],
        compiler_params=_cparams(("arbitrary", "arbitrary")),
    )(x1, nw, mod_arr, mod_arr, w_q, sub_keys)


def _peer_kernel(h2t_ref, u_ref, vt_ref, s1_ref, s2_ref, e1_ref, e2_ref, tau_ref, o_ref, act_scr, p_scr, *, te, n_et):
    e = pl.program_id(1)

    @pl.when(e == 0)
    def _():
        o_ref[...] = jnp.zeros_like(o_ref)
        act_scr[...] = jnp.zeros_like(act_scr)
        p_scr[...] = jnp.zeros_like(p_scr)

    o_ref[...] += _dot(vt_ref[...], p_scr[...])

    groups = te // PEER_KEYS
    tile = jnp.clip(e - 1, 0, n_et - 1)
    for ii in range(groups):
        rows = slice(ii * PEER_KEYS, (ii + 1) * PEER_KEYS)
        act = act_scr[rows, :]
        gel = 0.5 * act * (1.0 + lax.erf(act * (2.0 ** -0.5)))
        row = tile * groups + ii
        acc = jnp.zeros(act.shape, F32)
        for h in range(PEER_HEADS):
            sm = s1_ref[h, pl.ds(row, 1), :] + s2_ref[h]
            sel = jnp.where(sm >= tau_ref[h], e2_ref[h], 0.0)
            acc = acc + sel * e1_ref[h, pl.ds(row, 1), :]
        p_scr[rows, :] = (acc * gel).astype(BF16)

    act_scr[...] = _dot(u_ref[...].astype(BF16), h2t_ref[...])


def _peer(til, h2, routing, u_tab, vt_tab, layer):
    n, d, tm = til.n, til.d, til.tm
    ne = u_tab.shape[1]
    te = EXPERT_TILE
    n_et = ne // te
    once = pl.Buffered(1)
    tok = pl.BlockSpec((PEER_HEADS, PEER_KEYS, tm), lambda i, e: (0, 0, i), pipeline_mode=once)
    return pl.pallas_call(
        functools.partial(_peer_kernel, te=te, n_et=n_et),
        grid=(til.tiles, n_et + 2),
        in_specs=[
            pl.BlockSpec((d, tm), lambda i, e: (0, i), pipeline_mode=once),
            pl.BlockSpec((None, te, d), lambda i, e: (layer, jnp.minimum(e, n_et - 1), 0)),
            pl.BlockSpec((None, d, te), lambda i, e: (layer, 0, jnp.clip(e - 2, 0, n_et - 1))),
            tok, tok, tok, tok,
            pl.BlockSpec((PEER_HEADS, 1, tm), lambda i, e: (0, 0, i), pipeline_mode=once),
        ],
        out_specs=pl.BlockSpec((d, tm), lambda i, e: (0, i), pipeline_mode=once),
        out_shape=jax.ShapeDtypeStruct((d, n), F32),
        scratch_shapes=[pltpu.VMEM((te, tm), F32), pltpu.VMEM((te, tm), BF16)],
        compiler_params=pltpu.CompilerParams(dimension_semantics=("arbitrary", "arbitrary"),
                                             vmem_limit_bytes=PEER_VMEM_LIMIT),
    )(h2, u_tab, vt_tab, *routing)


def _transpose_cast_kernel(v_ref, o_ref):
    o_ref[...] = v_ref[...].T.astype(o_ref.dtype)


def _transpose_cast(v_tab):
    depth, ne, d = v_tab.shape
    te = EXPERT_TILE
    return pl.pallas_call(
        _transpose_cast_kernel,
        grid=(depth, ne // te),
        in_specs=[pl.BlockSpec((None, te, d), lambda l, e: (l, e, 0))],
        out_specs=pl.BlockSpec((None, d, te), lambda l, e: (l, 0, e)),
        out_shape=jax.ShapeDtypeStruct((depth, d, ne), BF16),
        compiler_params=_cparams(("arbitrary", "arbitrary")),
    )(v_tab)


def _residual_kernel(x_ref, pt_ref, g_ref, o_ref):
    o_ref[...] = x_ref[...] + g_ref[...] * pt_ref[...].T


def _residual_norm_kernel(x_ref, pt_ref, g_ref, nw_ref, o_ref):
    x = x_ref[...] + g_ref[...] * pt_ref[...].T
    o_ref[...] = x * lax.rsqrt(jnp.mean(x * x, axis=-1, keepdims=True) + EPS) * nw_ref[...]


def _residual(til, x1, p, mod_arr, final_norm):
    til = _Tiling(til.b, til.t, til.d, tile=256)
    n, d, tm = til.n, til.d, til.tm
    row = pl.BlockSpec((tm, d), lambda i: (i, 0))
    ins = [x1, p, mod_arr]
    specs = [row, pl.BlockSpec((d, tm), lambda i: (0, i)), til.mod_spec(5, 1)]
    body = _residual_kernel
    if final_norm is not None:
        ins.append(final_norm.reshape(1, d))
        specs.append(pl.BlockSpec((1, d), lambda i: (0, 0)))
        body = _residual_norm_kernel
    return pl.pallas_call(
        body, grid=(til.tiles,), in_specs=specs, out_specs=row,
        out_shape=jax.ShapeDtypeStruct((n, d), F32),
        compiler_params=_cparams(("arbitrary",)),
    )(*ins)


def _trunk(x, mod, states, prm):
    b, t, d = x.shape
    depth = mod.shape[0]
    dg = d // N_MIXERS
    til = _Tiling(b, t, d)
    xf = x.reshape(b * t, d)
    new = None
    for l in range(depth):
        mod_arr = til.mod_array(mod[l])
        z_big, zs = _in_proj(til, xf, prm["norm_mix"], mod_arr, prm["w_in_b"], l)
        outs, new = _mixers(b, t, dg, z_big, zs, states, new, prm, l, depth)
        x1 = _out_proj(til, outs, prm["w_out4"], xf, mod_arr, l)
        h2, *routing = _route(til, x1, prm["norm_ffn"], mod_arr, prm["w_q"], prm["peer_sub_keys"], l)
        p = _peer(til, h2, routing, prm["peer_u"], prm["peer_vt"], l)
        xf = _residual(til, x1, p, mod_arr, prm["final_norm"] if l == depth - 1 else None)
    m_c, m_n, m_m, s_hgrn, s_gdn, s_conv, s_gla = new
    nh = dg // HEAD_DIM
    new_states = [m_c, m_n, m_m.reshape(depth, b, nh), s_hgrn, s_gdn,
                  s_conv.reshape(depth, b, CONV_W - 1, 3 * dg), s_gla]
    return xf.reshape(b, t, d), new_states


def _prepare(mlstm_b_i, mlstm_b_f, gdn_a_log, gdn_dt_bias, gla_w_gate, d):
    depth = mlstm_b_i.shape[0]
    dg = d // N_MIXERS
    nh = dg // HEAD_DIM
    n_small = 4 * nh + GLA_RANK
    zeros = lambda k: jnp.zeros((depth, k), F32)
    bias = jnp.concatenate([mlstm_b_i, mlstm_b_f, gdn_dt_bias, zeros(LANES - 3 * nh)], axis=1)
    alog = jnp.concatenate([zeros(2 * nh), gdn_a_log, zeros(LANES - 3 * nh)], axis=1)
    gla_w_pad = jnp.concatenate(
        [jnp.zeros((depth, 4 * nh, dg), F32), gla_w_gate, jnp.zeros((depth, LANES - n_small, dg), F32)], axis=1)
    return dict(
        bias_row=bias.reshape(depth, 1, LANES), bias_col=bias[:, :SMALL_ROWS].reshape(depth, SMALL_ROWS, 1),
        alog_row=alog.reshape(depth, 1, LANES), alog_col=alog[:, :SMALL_ROWS].reshape(depth, SMALL_ROWS, 1),
        gla_w_pad=gla_w_pad)


def kernel(x_prompt, x_sample, c_prompt, c_sample, state_mlstm_C, state_mlstm_n, state_mlstm_m, state_hgrn, state_gdn, state_gdn_conv, state_gla, w_ada, b_ada, norm_mix, norm_ffn, w_in, mlstm_b_i, mlstm_b_f, hgrn_lb, gdn_conv_w, gdn_a_log, gdn_dt_bias, gla_w_gate, gla_b_gate, out_norm, w_out, peer_w_q, peer_sub_keys, peer_u, peer_v, final_norm):
    depth, d = norm_mix.shape
    dg = d // N_MIXERS
    assert d % (N_MIXERS * HEAD_DIM) == 0
    prm = _prepare(mlstm_b_i, mlstm_b_f, gdn_a_log, gdn_dt_bias, gla_w_gate, d)
    prm.update(
        norm_mix=norm_mix.reshape(depth, 1, d), norm_ffn=norm_ffn.reshape(depth, 1, d), hgrn_lb=hgrn_lb,
        gdn_conv_w=gdn_conv_w, gla_b_gate=gla_b_gate, out_norm=out_norm, final_norm=final_norm,
        peer_sub_keys=peer_sub_keys, w_in_b=jnp.swapaxes(w_in, 1, 2).astype(BF16),
        w_out4=w_out.astype(BF16).reshape(depth, N_MIXERS, dg, d),
        w_q=peer_w_q.astype(BF16), peer_u=peer_u,
        peer_vt=_transpose_cast(peer_v))

    bp, bs = c_prompt.shape[0], c_sample.shape[0]
    rows = -(-(bp + bs) // SUBLANES) * SUBLANES
    c_all = jnp.concatenate([c_prompt, c_sample, jnp.zeros((rows - bp - bs, d), F32)], axis=0)
    mod = _ada(c_all, w_ada, b_ada)

    y_prompt, p_states = _trunk(x_prompt, mod[:, :bp], None, prm)
    past = (state_mlstm_C, state_mlstm_n, state_mlstm_m, state_hgrn, state_gdn, state_gdn_conv, state_gla)
    y_sample, s_states = _trunk(x_sample, mod[:, bp:bp + bs], past, prm)
    return (y_prompt, y_sample, *p_states, *s_states)
```
